```python
import math
import jax, jax.numpy as jnp
from jax import lax
import numpy as np

D_MODEL = 1024
BATCH = 16
SEQ = 2048
DEPTH = 2

CTX_LEN = 256
GRID_W = 64
EPS = 1e-6
ROPE_BASE = 10000.0
Q_BLOCK = 128

MLA_HEADS = 4
MLA_Q_RANK = 256
MLA_KV_RANK = 128
MLA_NOPE_DIM = 128
MLA_ROPE_DIM = 64
MLA_V_DIM = 128

GLA_HEADS = 4
GLA_DK = 64
GLA_DV = 128
GLA_GATE_RANK = 16
GLA_TAU = 16.0
GLA_CHUNK = 64

DIFF_HEADS = 8
DIFF_DIM = 64

FFN_DIM = 2816
N_EXPERTS = 8
TOP_K = 2
EXPERT_DIM = 3584
MOE_BLOCK = 128

N_EVEN = (DEPTH + 1) // 2
N_ODD = DEPTH // 2

IN_A_SIZES = (MLA_Q_RANK, MLA_KV_RANK, MLA_ROPE_DIM,
              GLA_HEADS * GLA_DK, GLA_HEADS * GLA_DK, GLA_HEADS * GLA_DV, GLA_HEADS * GLA_DV,
              GLA_GATE_RANK, GLA_GATE_RANK)
IN_A_DIM = sum(IN_A_SIZES)
MIX_A_DIM = MLA_HEADS * MLA_V_DIM + GLA_HEADS * GLA_DV
MIX_C_DIM = DIFF_HEADS * 2 * DIFF_DIM

kernel_name = 'hybrid_mla_gla_diffattn_moe_dit'


def _rmsnorm(x, g):
    xf = x.astype(jnp.float32)
    y = xf * lax.rsqrt(jnp.mean(xf * xf, axis=-1, keepdims=True) + EPS)
    return (y * g.astype(jnp.float32)).astype(x.dtype)


def _axial_rope(n_tokens, rot_dim):
    rows = n_tokens // GRID_W
    row = jnp.repeat(jnp.arange(rows, dtype=jnp.float32), GRID_W)
    col = jnp.tile(jnp.arange(GRID_W, dtype=jnp.float32), rows)
    n_freq = rot_dim // 4
    freq = ROPE_BASE ** (-jnp.arange(n_freq, dtype=jnp.float32) / n_freq)
    ang = jnp.concatenate([row[:, None] * freq, col[:, None] * freq], axis=-1)
    return jnp.cos(ang), jnp.sin(ang)


def _apply_rope(x, cos, sin):
    half = x.shape[-1] // 2
    x1 = x[..., :half].astype(jnp.float32)
    x2 = x[..., half:].astype(jnp.float32)
    cs, sn = cos[None, :, None, :], sin[None, :, None, :]
    return jnp.concatenate([x1 * cs - x2 * sn, x2 * cs + x1 * sn], axis=-1).astype(x.dtype)


def _split_cols(p, sizes):
    return jnp.split(p, [int(o) for o in np.cumsum(sizes)[:-1]], axis=-1)


def _query_blocks(a):
    b, t = a.shape[:2]
    nb = t // Q_BLOCK
    return a.reshape((b, nb, Q_BLOCK) + a.shape[2:]).swapaxes(0, 1)


def _unblock(o):
    nb, b, qb = o.shape[:3]
    return o.swapaxes(0, 1).reshape((b, nb * qb) + o.shape[3:])


def _attend(q, k, v):
    scale = q.shape[-1] ** -0.5

    def one(qb):
        s = jnp.einsum('bqhd,bkhd->bhqk', qb, k, preferred_element_type=jnp.float32) * scale
        p = jax.nn.softmax(s, axis=-1)
        return jnp.einsum('bhqk,bkhd->bqhd', p.astype(v.dtype), v)

    return _unblock(lax.map(one, _query_blocks(q)))


def _diff_attend(q1, q2, k1, k2, v, lam):
    scale = q1.shape[-1] ** -0.5

    def one(args):
        a1, a2 = args
        s1 = jnp.einsum('bqhd,bkhd->bhqk', a1, k1, preferred_element_type=jnp.float32) * scale
        s2 = jnp.einsum('bqhd,bkhd->bhqk', a2, k2, preferred_element_type=jnp.float32) * scale
        p = jax.nn.softmax(s1, axis=-1) - lam * jax.nn.softmax(s2, axis=-1)
        return jnp.einsum('bhqk,bkhd->bqhd', p.astype(v.dtype), v)

    return _unblock(lax.map(one, (_query_blocks(q1), _query_blocks(q2))))


def _mla_qkv(c_q, c_kv, k_r, g_q, w_uq, g_kv, w_ukv, cos, sin):
    b, t, _ = c_q.shape
    q = (_rmsnorm(c_q, g_q) @ w_uq).reshape(b, t, MLA_HEADS, MLA_NOPE_DIM + MLA_ROPE_DIM)
    kv = (_rmsnorm(c_kv, g_kv) @ w_ukv).reshape(b, t, MLA_HEADS, MLA_NOPE_DIM + MLA_V_DIM)
    q_nope, q_rope = q[..., :MLA_NOPE_DIM], q[..., MLA_NOPE_DIM:]
    k_nope, v = kv[..., :MLA_NOPE_DIM], kv[..., MLA_NOPE_DIM:]
    k_rope = k_r[:, :, None, :]
    if cos is not None:
        q_rope = _apply_rope(q_rope, cos, sin)
        k_rope = _apply_rope(k_rope, cos, sin)
    q = jnp.concatenate([q_nope, q_rope], axis=-1)
    k = jnp.concatenate([k_nope, jnp.broadcast_to(k_rope, (b, t, MLA_HEADS, MLA_ROPE_DIM))], axis=-1)
    return q, k, v


def _gla_scan(q, k, v, log_a, s0):
    b, t, h, dk = q.shape
    dv = v.shape[-1]
    n = t // GLA_CHUNK

    def chunks(a):
        return a.astype(jnp.float32).reshape(b, n, GLA_CHUNK, h, a.shape[-1]).swapaxes(0, 1)

    mask = jnp.tril(jnp.ones((GLA_CHUNK, GLA_CHUNK), dtype=bool))[None, :, :, None, None]

    def step(state, inp):
        qc, kc, vc, lac = inp
        cum = jnp.cumsum(lac, axis=1)
        o_inter = jnp.einsum('bchk,bhkv->bchv', qc * jnp.exp(cum), state)
        rel = cum[:, :, None] - cum[:, None, :]
        decay = jnp.exp(jnp.where(mask, rel, -jnp.inf))
        att = jnp.einsum('bthk,btshk,bshk->bths', qc, decay, kc)
        o_intra = jnp.einsum('bths,bshv->bthv', att, vc)
        last = cum[:, -1]
        new_state = jnp.exp(last)[..., None] * state + jnp.einsum(
            'bshk,bshv->bhkv', kc * jnp.exp(last[:, None] - cum), vc)
        return new_state, o_inter + o_intra

    s_fin, o = lax.scan(step, s0.astype(jnp.float32), (chunks(q), chunks(k), chunks(v), chunks(log_a)))
    return o.swapaxes(0, 1).reshape(b, t, h, dv).astype(v.dtype), s_fin


def _gla_bidir(parts_l, parts_x, w_gf, b_gf, w_gb, b_gb, g_o, need_ctx):
    def prep(parts):
        q, k, v, r, af, ab = parts
        b, t, _ = q.shape
        q = q.reshape(b, t, GLA_HEADS, GLA_DK) * (GLA_DK ** -0.5)
        k = k.reshape(b, t, GLA_HEADS, GLA_DK)
        v = v.reshape(b, t, GLA_HEADS, GLA_DV)
        la_f = (jax.nn.log_sigmoid((af @ w_gf + b_gf).astype(jnp.float32)) / GLA_TAU).reshape(b, t, GLA_HEADS, GLA_DK)
        la_b = (jax.nn.log_sigmoid((ab @ w_gb + b_gb).astype(jnp.float32)) / GLA_TAU).reshape(b, t, GLA_HEADS, GLA_DK)
        return q, k, v, r, la_f, la_b

    ql, kl, vl, rl, lfl, lbl = prep(parts_l)
    qx, kx, vx, rx, lfx, lbx = prep(parts_x)
    flip = lambda a: jnp.flip(a, axis=1)
    s0 = jnp.zeros((ql.shape[0], GLA_HEADS, GLA_DK, GLA_DV), jnp.float32)
    ox_f, sx_f = _gla_scan(qx, kx, vx, lfx, s0)
    ox_b, sx_b = _gla_scan(flip(qx), flip(kx), flip(vx), flip(lbx), s0)
    ol_f, _ = _gla_scan(ql, kl, vl, lfl, sx_f)
    ol_b, _ = _gla_scan(flip(ql), flip(kl), flip(vl), flip(lbl), sx_b)

    def finish(o, r):
        b, t = o.shape[:2]
        gate = jax.nn.silu(r.reshape(b, t, GLA_HEADS, GLA_DV))
        return (_rmsnorm(o, g_o) * gate).reshape(b, t, GLA_HEADS * GLA_DV)

    out_l = finish(ol_f + flip(ol_b), rl)
    out_x = finish(ox_f + flip(ox_b), rx) if need_ctx else None
    return out_l, out_x


def _mixer_mla_gla(h, hx, w_in, g_q, w_uq, g_kv, w_ukv, w_gf, b_gf, w_gb, b_gb, g_o, w_out,
                   cos, sin, need_ctx):
    b, t, _ = h.shape
    parts = _split_cols(h @ w_in, IN_A_SIZES)
    parts_x = _split_cols(hx @ w_in, IN_A_SIZES)
    q_l, k_l, v_l = _mla_qkv(parts[0], parts[1], parts[2], g_q, w_uq, g_kv, w_ukv, cos, sin)
    q_x, k_x, v_x = _mla_qkv(parts_x[0], parts_x[1], parts_x[2], g_q, w_uq, g_kv, w_ukv, None, None)
    a_l = _attend(q_l, jnp.concatenate([k_x, k_l], axis=1), jnp.concatenate([v_x, v_l], axis=1))
    g_l, g_x = _gla_bidir(parts[3:], parts_x[3:], w_gf, b_gf, w_gb, b_gb, g_o, need_ctx)
    y = jnp.concatenate([a_l.reshape(b, t, MLA_HEADS * MLA_V_DIM), g_l], axis=-1) @ w_out
    yx = None
    if need_ctx:
        a_x = _attend(q_x, k_x, v_x)
        bx, tx = hx.shape[:2]
        yx = jnp.concatenate([a_x.reshape(bx, tx, MLA_HEADS * MLA_V_DIM), g_x], axis=-1) @ w_out
    return y, yx


def _diff_qkv(h, w_qkv, cos, sin):
    b, t, _ = h.shape
    q, k, v = jnp.split(h @ w_qkv, 3, axis=-1)
    q = q.reshape(b, t, 2 * DIFF_HEADS, DIFF_DIM)
    k = k.reshape(b, t, 2 * DIFF_HEADS, DIFF_DIM)
    if cos is not None:
        q = _apply_rope(q, cos, sin)
        k = _apply_rope(k, cos, sin)
    q = q.reshape(b, t, DIFF_HEADS, 2, DIFF_DIM)
    k = k.reshape(b, t, DIFF_HEADS, 2, DIFF_DIM)
    v = v.reshape(b, t, DIFF_HEADS, 2 * DIFF_DIM)
    return q[..., 0, :], q[..., 1, :], k[..., 0, :], k[..., 1, :], v


def _mixer_diff(h, hx, w_qkv, lq1, lk1, lq2, lk2, g_o, w_out, lam_init, cos, sin, need_ctx):
    f32 = jnp.float32
    lam = (jnp.exp(jnp.sum(lq1.astype(f32) * lk1.astype(f32)))
           - jnp.exp(jnp.sum(lq2.astype(f32) * lk2.astype(f32))) + lam_init)
    q1, q2, k1, k2, v = _diff_qkv(h, w_qkv, cos, sin)
    q1x, q2x, k1x, k2x, vx = _diff_qkv(hx, w_qkv, None, None)
    cat = lambda a, b_: jnp.concatenate([a, b_], axis=1)

    def finish(o):
        b, t = o.shape[:2]
        return (_rmsnorm(o, g_o) * (1.0 - lam_init)).reshape(b, t, MIX_C_DIM) @ w_out

    y = finish(_diff_attend(q1, q2, cat(k1x, k1), cat(k2x, k2), cat(vx, v), lam))
    yx = finish(_diff_attend(q1x, q2x, k1x, k2x, vx, lam)) if need_ctx else None
    return y, yx


def _swiglu(h, w_gate, w_up, w_down):
    return (jax.nn.silu(h @ w_gate) * (h @ w_up)) @ w_down


def _moe_swiglu(h, w_router, w_gate, w_up, w_down):
    n, d = h.shape
    logits = jnp.dot(h, w_router, preferred_element_type=jnp.float32)
    top_val, top_idx = lax.top_k(logits, TOP_K)
    gates = jax.nn.softmax(top_val, axis=-1)
    flat_e = top_idx.reshape(-1)
    flat_t = jnp.repeat(jnp.arange(n, dtype=jnp.int32), TOP_K)
    flat_g = gates.reshape(-1)
    order = jnp.argsort(flat_e, stable=True)
    e_s, t_s, g_s = flat_e[order], flat_t[order], flat_g[order]
    sizes = jnp.bincount(flat_e, length=N_EXPERTS)
    padded = (sizes + MOE_BLOCK - 1) // MOE_BLOCK * MOE_BLOCK
    start = jnp.cumsum(sizes) - sizes
    pend = jnp.cumsum(padded)
    pstart = pend - padded
    dest = pstart[e_s] + (jnp.arange(n * TOP_K) - start[e_s])
    cap = -(-(n * TOP_K) // MOE_BLOCK) * MOE_BLOCK + N_EXPERTS * MOE_BLOCK
    nb = cap // MOE_BLOCK
    buf_t = jnp.zeros((cap,), jnp.int32).at[dest].set(t_s)
    buf_g = jnp.zeros((cap,), jnp.float32).at[dest].set(g_s)
    blk_e = jnp.minimum(jnp.searchsorted(pend, jnp.arange(nb) * MOE_BLOCK, side='right'), N_EXPERTS - 1)
    xs = h[buf_t].reshape(nb, MOE_BLOCK, d)

    def expert_block(args):
        xb, e = args
        return (jax.nn.silu(xb @ w_gate[e]) * (xb @ w_up[e])) @ w_down[e]

    ys = lax.map(expert_block, (xs, blk_e)).reshape(cap, d)
    return jnp.zeros_like(h).at[buf_t].add(ys * buf_g[:, None].astype(h.dtype))


def setup_inputs(seed: int = 0) -> dict:
    key = jax.random.key(seed)
    ks = iter(jax.random.split(key, 48))
    D = D_MODEL

    def nrm(shape, scale):
        return scale * jax.random.normal(next(ks), shape, jnp.float32)

    def gain(shape):
        return 1.0 + nrm(shape, 0.05)

    return {
        'x': nrm((BATCH, SEQ, D), 1.0),
        'c': nrm((BATCH, D), 1.0),
        'ctx': nrm((BATCH, CTX_LEN, D), 1.0),
        'c_ctx': nrm((D,), 1.0),
        'w_mod': nrm((DEPTH, D, 6 * D), D ** -0.5),
        'b_mod': nrm((DEPTH, 6 * D), 0.02),
        'g_pre_mix': gain((DEPTH, D)),
        'g_post_mix': gain((DEPTH, D)),
        'g_pre_ffn': gain((DEPTH, D)),
        'g_post_ffn': gain((DEPTH, D)),
        'w_in_a': nrm((N_EVEN, D, IN_A_DIM), D ** -0.5),
        'mla_g_q': gain((N_EVEN, MLA_Q_RANK)),
        'mla_w_uq': nrm((N_EVEN, MLA_Q_RANK, MLA_HEADS * (MLA_NOPE_DIM + MLA_ROPE_DIM)), MLA_Q_RANK ** -0.5),
        'mla_g_kv': gain((N_EVEN, MLA_KV_RANK)),
        'mla_w_ukv': nrm((N_EVEN, MLA_KV_RANK, MLA_HEADS * (MLA_NOPE_DIM + MLA_V_DIM)), MLA_KV_RANK ** -0.5),
        'gla_w_gate_f': nrm((N_EVEN, GLA_GATE_RANK, GLA_HEADS * GLA_DK), GLA_GATE_RANK ** -0.5),
        'gla_b_gate_f': nrm((N_EVEN, GLA_HEADS * GLA_DK), 0.1),
        'gla_w_gate_b': nrm((N_EVEN, GLA_GATE_RANK, GLA_HEADS * GLA_DK), GLA_GATE_RANK ** -0.5),
        'gla_b_gate_b': nrm((N_EVEN, GLA_HEADS * GLA_DK), 0.1),
        'gla_g_out': gain((N_EVEN, GLA_DV)),
        'w_out_a': nrm((N_EVEN, MIX_A_DIM, D), MIX_A_DIM ** -0.5),
        'w_qkv_c': nrm((N_ODD, D, 3 * MIX_C_DIM), D ** -0.5),
        'diff_lq1': nrm((N_ODD, DIFF_DIM), 0.1),
        'diff_lk1': nrm((N_ODD, DIFF_DIM), 0.1),
        'diff_lq2': nrm((N_ODD, DIFF_DIM), 0.1),
        'diff_lk2': nrm((N_ODD, DIFF_DIM), 0.1),
        'diff_g_out': gain((N_ODD, 2 * DIFF_DIM)),
        'w_out_c': nrm((N_ODD, MIX_C_DIM, D), MIX_C_DIM ** -0.5),
        'ffn_w_gate': nrm((N_EVEN, D, FFN_DIM), D ** -0.5),
        'ffn_w_up': nrm((N_EVEN, D, FFN_DIM), D ** -0.5),
        'ffn_w_down': nrm((N_EVEN, FFN_DIM, D), FFN_DIM ** -0.5),
        'moe_w_router': nrm((N_ODD, D, N_EXPERTS), D ** -0.5),
        'moe_w_gate': nrm((N_ODD, N_EXPERTS, D, EXPERT_DIM), D ** -0.5),
        'moe_w_up': nrm((N_ODD, N_EXPERTS, D, EXPERT_DIM), D ** -0.5),
        'moe_w_down': nrm((N_ODD, N_EXPERTS, EXPERT_DIM, D), EXPERT_DIM ** -0.5),
    }


def reference(x, c, ctx, c_ctx, w_mod, b_mod, g_pre_mix, g_post_mix, g_pre_ffn, g_post_ffn,
              w_in_a, mla_g_q, mla_w_uq, mla_g_kv, mla_w_ukv,
              gla_w_gate_f, gla_b_gate_f, gla_w_gate_b, gla_b_gate_b, gla_g_out, w_out_a,
              w_qkv_c, diff_lq1, diff_lk1, diff_lq2, diff_lk2, diff_g_out, w_out_c,
              ffn_w_gate, ffn_w_up, ffn_w_down,
              moe_w_router, moe_w_gate, moe_w_up, moe_w_down):
    b, t, d = x.shape
    cos_a, sin_a = _axial_rope(t, MLA_ROPE_DIM)
    cos_c, sin_c = _axial_rope(t, DIFF_DIM)
    silu_c = jax.nn.silu(c)[:, None, :]
    silu_cx = jax.nn.silu(c_ctx)

    for i in range(DEPTH):
        last = i == DEPTH - 1
        j = i // 2
        mod = jnp.split(silu_c @ w_mod[i] + b_mod[i], 6, axis=-1)
        modx = jnp.split(silu_cx @ w_mod[i] + b_mod[i], 6, axis=-1)

        h = _rmsnorm(x, g_pre_mix[i]) * (1.0 + mod[1]) + mod[0]
        hx = _rmsnorm(ctx, g_pre_mix[i]) * (1.0 + modx[1]) + modx[0]
        if i % 2 == 0:
            y, yx = _mixer_mla_gla(h, hx, w_in_a[j], mla_g_q[j], mla_w_uq[j], mla_g_kv[j], mla_w_ukv[j],
                                   gla_w_gate_f[j], gla_b_gate_f[j], gla_w_gate_b[j], gla_b_gate_b[j],
                                   gla_g_out[j], w_out_a[j], cos_a, sin_a, not last)
        else:
            lam_init = 0.8 - 0.6 * math.exp(-0.3 * i)
            y, yx = _mixer_diff(h, hx, w_qkv_c[j], diff_lq1[j], diff_lk1[j], diff_lq2[j], diff_lk2[j],
                                diff_g_out[j], w_out_c[j], lam_init, cos_c, sin_c, not last)
        x = x + mod[2] * _rmsnorm(y, g_post_mix[i])
        if not last:
            ctx = ctx + modx[2] * _rmsnorm(yx, g_post_mix[i])

        h = _rmsnorm(x, g_pre_ffn[i]) * (1.0 + mod[4]) + mod[3]
        if i % 2 == 0:
            f = _swiglu(h, ffn_w_gate[j], ffn_w_up[j], ffn_w_down[j])
        else:
            f = _moe_swiglu(h.reshape(-1, d), moe_w_router[j], moe_w_gate[j], moe_w_up[j],
                            moe_w_down[j]).reshape(b, t, d)
        x = x + mod[5] * _rmsnorm(f, g_post_ffn[i])
        if not last:
            hx = _rmsnorm(ctx, g_pre_ffn[i]) * (1.0 + modx[4]) + modx[3]
            if i % 2 == 0:
                fx = _swiglu(hx, ffn_w_gate[j], ffn_w_up[j], ffn_w_down[j])
            else:
                fx = _moe_swiglu(hx.reshape(-1, d), moe_w_router[j], moe_w_gate[j], moe_w_up[j],
                                 moe_w_down[j]).reshape(ctx.shape)
            ctx = ctx + modx[5] * _rmsnorm(fx, g_post_ffn[i])
    return x
```

```python
import functools
import math

import jax
import jax.numpy as jnp
from jax import lax
from jax.experimental import pallas as pl
from jax.experimental.pallas import tpu as pltpu

F32 = jnp.float32
BF16 = jnp.bfloat16
I32 = jnp.int32

EPS = 1e-6
ROPE_BASE = 10000.0
GRID_W = 64

D_MODEL = 1024
MLA_HEADS = 4
MLA_Q_RANK = 256
MLA_KV_RANK = 128
MLA_NOPE = 128
MLA_ROPE = 64
MLA_V = 128
GLA_HEADS = 4
GLA_DK = 64
GLA_DV = 128
GLA_RANK = 16
GLA_TAU = 16.0
DIFF_HEADS = 8
DIFF_DIM = 64
N_EXPERTS = 8
LANES = 128
GLA_CHUNK = 128
GLA_SUB = 16
MOE_ROWS = 512
VMEM_LIMIT = 56 * 1024 * 1024

_NT = (((1,), (1,)), ((), ()))


def _cparams(sem):
    return pltpu.CompilerParams(dimension_semantics=sem, vmem_limit_bytes=VMEM_LIMIT)


def _rms(xf, g):
    return xf * lax.rsqrt(jnp.mean(xf * xf, axis=-1, keepdims=True) + EPS) * g


def _normmod(x, g, shift, scale):
    return _rms(x.astype(F32), g) * (1.0 + scale) + shift


def _silu(x):
    return x * (1.0 / (1.0 + jnp.exp(-x)))


def _const_spec(shape):
    n = len(shape)
    return pl.BlockSpec(shape, lambda *_: (0,) * n)


def _mod_kernel(c_ref, w_ref, b_ref, o_ref):
    s = _silu(c_ref[...])
    o_ref[0] = jnp.dot(s.astype(BF16), w_ref[0].astype(BF16), preferred_element_type=F32) + b_ref[0]


def _modulation(c_all, w_mod, b_mod):
    depth, d, n6 = w_mod.shape
    rows = c_all.shape[0]
    tn = 1536
    return pl.pallas_call(
        _mod_kernel,
        grid=(depth, n6 // tn),
        in_specs=[pl.BlockSpec((rows, d), lambda i, j: (0, 0)),
                  pl.BlockSpec((1, d, tn), lambda i, j: (i, 0, j)),
                  pl.BlockSpec((1, 1, tn), lambda i, j: (i, 0, j))],
        out_specs=pl.BlockSpec((1, rows, tn), lambda i, j: (i, 0, j)),
        out_shape=jax.ShapeDtypeStruct((depth, rows, n6), F32),
        compiler_params=_cparams(("arbitrary", "arbitrary")),
        name="modulation",
    )(c_all, w_mod, b_mod.reshape(depth, 1, n6))


def _proj_kernel(x_ref, g_ref, sh_ref, sc_ref, w_ref, o_ref):
    h = _normmod(x_ref[...], g_ref[...], sh_ref[0], sc_ref[0])
    o_ref[...] = jnp.dot(h.astype(BF16), w_ref[...], preferred_element_type=F32).astype(o_ref.dtype)


def _norm_proj(x, g, shift, scale, w, rows_per_mod, tm, out_dtype, name):
    n, d = x.shape
    nout = w.shape[1]
    mod_idx = lambda i: ((i * tm) // rows_per_mod, 0, 0)
    return pl.pallas_call(
        _proj_kernel,
        grid=(n // tm,),
        in_specs=[pl.BlockSpec((tm, d), lambda i: (i, 0)),
                  _const_spec((1, d)),
                  pl.BlockSpec((1, 1, d), mod_idx),
                  pl.BlockSpec((1, 1, d), mod_idx),
                  _const_spec((d, nout))],
        out_specs=pl.BlockSpec((tm, nout), lambda i: (i, 0)),
        out_shape=jax.ShapeDtypeStruct((n, nout), out_dtype),
        compiler_params=_cparams(("arbitrary",)),
        name=name,
    )(x, g, shift, scale, w)


def _mla_prep_kernel(cq_ref, ckv_ref, kr_ref, gq_ref, gkv_ref, wq_ref, wkv_ref, c2_ref, s2_ref,
                     q_ref, k_ref, v_ref, *, scale):
    c2 = c2_ref[...]
    s2 = s2_ref[...]

    def rope2(v):
        return v * c2 + pltpu.roll(v, 64, 1) * s2

    q = jnp.dot(_rms(cq_ref[...], gq_ref[...]).astype(BF16), wq_ref[...], preferred_element_type=F32)
    kv = jnp.dot(_rms(ckv_ref[...], gkv_ref[...]).astype(BF16), wkv_ref[...], preferred_element_type=F32)
    krope = rope2(kr_ref[...]).astype(BF16)
    for h in range(MLA_HEADS):
        b = h * 2 * LANES
        q_ref[:, b:b + LANES] = (q[:, b:b + LANES] * scale).astype(BF16)
        q_ref[:, b + LANES:b + 2 * LANES] = (rope2(q[:, b + LANES:b + 2 * LANES]) * scale).astype(BF16)
        k_ref[:, b:b + LANES] = kv[:, h * LANES:(h + 1) * LANES].astype(BF16)
        k_ref[:, b + LANES:b + 2 * LANES] = krope
    v_ref[...] = kv[:, MLA_HEADS * LANES:].astype(BF16)


def _mla_prep(proj, gq, gkv, wq, wkv, c2, s2, tm, rope_blocks):
    n = proj.shape[0]
    hq = MLA_HEADS * 2 * LANES
    rope_idx = (lambda i: (i % rope_blocks, 0)) if rope_blocks else (lambda i: (0, 0))
    return pl.pallas_call(
        functools.partial(_mla_prep_kernel, scale=float((MLA_NOPE + MLA_ROPE) ** -0.5)),
        grid=(n // tm,),
        in_specs=[pl.BlockSpec((tm, 256), lambda i: (i, 0)),
                  pl.BlockSpec((tm, 128), lambda i: (i, 2)),
                  pl.BlockSpec((tm, 128), lambda i: (i, 3)),
                  _const_spec((1, MLA_Q_RANK)), _const_spec((1, MLA_KV_RANK)),
                  _const_spec(wq.shape), _const_spec(wkv.shape),
                  pl.BlockSpec((tm, LANES), rope_idx), pl.BlockSpec((tm, LANES), rope_idx)],
        out_specs=[pl.BlockSpec((tm, hq), lambda i: (i, 0)),
                   pl.BlockSpec((tm, hq), lambda i: (i, 0)),
                   pl.BlockSpec((tm, MLA_HEADS * MLA_V), lambda i: (i, 0))],
        out_shape=[jax.ShapeDtypeStruct((n, hq), BF16), jax.ShapeDtypeStruct((n, hq), BF16),
                   jax.ShapeDtypeStruct((n, MLA_HEADS * MLA_V), BF16)],
        compiler_params=_cparams(("arbitrary",)),
        name="mla_prep",
    )(proj, proj, proj, gq, gkv, wq, wkv, c2, s2)


def _attn_kernel(*refs, n_src):
    q_ref = refs[0]
    k_refs = refs[1:1 + n_src]
    v_refs = refs[1 + n_src:1 + 2 * n_src]
    o_ref = refs[1 + 2 * n_src]
    q = q_ref[...]
    s = [lax.dot_general(q, k[...], _NT, preferred_element_type=F32) for k in k_refs]
    m = functools.reduce(jnp.maximum, [jnp.max(a, axis=-1, keepdims=True) for a in s])
    p = [jnp.exp(a - m) for a in s]
    l = functools.reduce(jnp.add, [jnp.sum(a, axis=-1, keepdims=True) for a in p])
    o = functools.reduce(jnp.add, [jnp.dot(a.astype(BF16), v[...], preferred_element_type=F32)
                                   for a, v in zip(p, v_refs)])
    o_ref[...] = (o / l).astype(o_ref.dtype)


def _attention(q, ks, vs, batch, heads, dq, dv, tq, name):
    nq = q.shape[0] // batch // tq
    n_src = len(ks)
    in_specs = [pl.BlockSpec((tq, dq), lambda b, h, i: (b * nq + i, h))]
    for k in ks:
        in_specs.append(pl.BlockSpec((k.shape[0] // batch, dq), lambda b, h, i: (b, h)))
    for v in vs:
        in_specs.append(pl.BlockSpec((v.shape[0] // batch, dv), lambda b, h, i: (b, h)))
    return pl.pallas_call(
        functools.partial(_attn_kernel, n_src=n_src),
        grid=(batch, heads, nq),
        in_specs=in_specs,
        out_specs=pl.BlockSpec((tq, dv), lambda b, h, i: (b * nq + i, h)),
        out_shape=jax.ShapeDtypeStruct((q.shape[0], heads * dv), BF16),
        compiler_params=_cparams(("arbitrary", "arbitrary", "arbitrary")),
        name=name,
    )(q, *ks, *vs)


def _log_sigmoid(z):
    return jnp.minimum(z, 0.0) - jnp.log(1.0 + jnp.exp(-jnp.abs(z)))


def _gla_chunk(q, k, v, la, s_ref, e_mat, rev):
    c = GLA_CHUNK
    hk = GLA_HEADS * GLA_DK
    ri = lax.broadcasted_iota(I32, (c, c), 0)
    ci = lax.broadcasted_iota(I32, (c, c), 1)
    tri = jnp.where((ci >= ri) if rev else (ci <= ri), 1.0, 0.0).astype(F32)
    cum = jnp.dot(tri, la, precision=lax.Precision.HIGHEST, preferred_element_type=F32)
    tot = cum[0:1] if rev else cum[c - 1:c]
    s_old = s_ref[...]
    qs = q * (GLA_DK ** -0.5)
    qh = (qs * jnp.exp(cum)).astype(BF16)
    o_inter = lax.dot_general(qh, s_old.astype(BF16), _NT, preferred_element_type=F32)
    kh = (k * jnp.exp(tot - cum)).astype(BF16)
    vb = v.astype(BF16)
    u = jnp.dot(v.T.astype(BF16), kh, preferred_element_type=F32)
    bd = (lax.broadcasted_iota(I32, u.shape, 0) // GLA_DV) == (lax.broadcasted_iota(I32, u.shape, 1) // GLA_DK)
    s_ref[...] = s_old * jnp.exp(tot) + jnp.where(bd, u, 0.0)

    lane_head = lax.broadcasted_iota(I32, (GLA_SUB, hk), 1) // GLA_DK
    row_id = lax.broadcasted_iota(I32, (c, hk), 0)
    tt = lax.broadcasted_iota(I32, (GLA_SUB, hk), 0)
    n_sub = c // GLA_SUB
    outs = []
    for blk in range(n_sub):
        r0 = blk * GLA_SUB
        q_i = qs[r0:r0 + GLA_SUB]
        c_i = cum[r0:r0 + GLA_SUB]
        k_i = k[r0:r0 + GLA_SUB]
        v_i = v[r0:r0 + GLA_SUB]
        o_i = o_inter[r0:r0 + GLA_SUB]
        has_prev = (blk < n_sub - 1) if rev else (blk > 0)
        if has_prev:
            edge = r0 + GLA_SUB if rev else r0 - 1
            anchor = cum[edge:edge + 1]
            qt = q_i * jnp.exp(c_i - anchor)
            prev = (row_id >= r0 + GLA_SUB) if rev else (row_id < r0)
            kt = jnp.where(prev, k * jnp.exp(jnp.minimum(anchor - cum, 0.0)), 0.0).astype(BF16)
            q_st = jnp.concatenate([jnp.where(lane_head == h, qt, 0.0) for h in range(GLA_HEADS)],
                                   axis=0).astype(BF16)
            att = lax.dot_general(q_st, kt, _NT, preferred_element_type=F32).astype(BF16)
            o_i = o_i + jnp.concatenate(
                [jnp.dot(att[h * GLA_SUB:(h + 1) * GLA_SUB], vb[:, h * GLA_DV:(h + 1) * GLA_DV],
                         preferred_element_type=F32) for h in range(GLA_HEADS)], axis=1)
        terms = []
        for s in range(GLA_SUB):
            valid = (tt <= s) if rev else (tt >= s)
            w = q_i * jnp.exp(jnp.minimum(c_i - c_i[s:s + 1], 0.0)) * k_i[s:s + 1]
            terms.append(jnp.where(valid, w, 0.0).astype(BF16))
        wsum = jnp.dot(jnp.concatenate(terms, axis=0), e_mat, preferred_element_type=F32)
        for s in range(GLA_SUB):
            o_i = o_i + wsum[s * GLA_SUB:(s + 1) * GLA_SUB] * v_i[s:s + 1]
        outs.append(o_i)
    return jnp.concatenate(outs, axis=0)


def _gla_kernel(qf_ref, kf_ref, vf_ref, gf_ref, qb_ref, kb_ref, vb_ref, gb_ref,
                wf_ref, bf_ref, wb_ref, bb_ref, e_ref, s0f_ref, s0b_ref,
                of_ref, ob_ref, sf_ref, sb_ref, stf, stb):
    j = pl.program_id(1)

    @pl.when(j == 0)
    def _():
        stf[...] = s0f_ref[0]
        stb[...] = s0b_ref[0]

    e_mat = e_ref[...]

    def log_decay(g_ref, w_ref, b_ref):
        z = jnp.dot(g_ref[...].astype(BF16), w_ref[...], preferred_element_type=F32) + b_ref[...]
        return _log_sigmoid(z) * (1.0 / GLA_TAU)

    of_ref[...] = _gla_chunk(qf_ref[...], kf_ref[...], vf_ref[...], log_decay(gf_ref, wf_ref, bf_ref),
                             stf, e_mat, False)
    ob_ref[...] = _gla_chunk(qb_ref[...], kb_ref[...], vb_ref[...], log_decay(gb_ref, wb_ref, bb_ref),
                             stb, e_mat, True)

    @pl.when(j == pl.num_programs(1) - 1)
    def _():
        sf_ref[0] = stf[...]
        sb_ref[0] = stb[...]


def _gla_scan(proj, wf, bf, wb, bb, e_mat, s0f, s0b, batch):
    n = proj.shape[0]
    c = GLA_CHUNK
    nch = n // batch // c
    hk = GLA_HEADS * GLA_DK
    hv = GLA_HEADS * GLA_DV
    fwd = lambda b, j: b * nch + j
    bwd = lambda b, j: b * nch + (nch - 1 - j)

    def specs(row):
        return [pl.BlockSpec((c, hk), lambda b, j: (row(b, j), 2)),
                pl.BlockSpec((c, hk), lambda b, j: (row(b, j), 3)),
                pl.BlockSpec((c, hv), lambda b, j: (row(b, j), 2)),
                pl.BlockSpec((c, LANES), lambda b, j: (row(b, j), 16))]

    st_spec = pl.BlockSpec((1, hv, hk), lambda b, j: (b, 0, 0))
    return pl.pallas_call(
        _gla_kernel,
        grid=(batch, nch),
        in_specs=specs(fwd) + specs(bwd) + [
            _const_spec((LANES, hk)), _const_spec((1, hk)), _const_spec((LANES, hk)), _const_spec((1, hk)),
            _const_spec(e_mat.shape), st_spec, st_spec],
        out_specs=[pl.BlockSpec((c, hv), lambda b, j: (fwd(b, j), 0)),
                   pl.BlockSpec((c, hv), lambda b, j: (bwd(b, j), 0)),
                   st_spec, st_spec],
        out_shape=[jax.ShapeDtypeStruct((n, hv), F32), jax.ShapeDtypeStruct((n, hv), F32),
                   jax.ShapeDtypeStruct((batch, hv, hk), F32), jax.ShapeDtypeStruct((batch, hv, hk), F32)],
        scratch_shapes=[pltpu.VMEM((hv, hk), F32), pltpu.VMEM((hv, hk), F32)],
        compiler_params=_cparams(("arbitrary", "arbitrary")),
        name="gla_scan",
    )(proj, proj, proj, proj, proj, proj, proj, proj, wf, bf, wb, bb, e_mat, s0f, s0b)


def _out_a_kernel(a_ref, of_ref, ob_ref, r_ref, go_ref, x_ref, gate_ref, gp_ref, w_ref, o_ref):
    o = of_ref[...] + ob_ref[...]
    r = r_ref[...]
    go = go_ref[...]
    parts = []
    for h in range(GLA_HEADS):
        sl = slice(h * GLA_DV, (h + 1) * GLA_DV)
        parts.append(_rms(o[:, sl], go) * _silu(r[:, sl]))
    g = jnp.concatenate(parts, axis=-1).astype(BF16)
    na = a_ref.shape[1]
    y = (jnp.dot(a_ref[...], w_ref[0:na, :], preferred_element_type=F32)
         + jnp.dot(g, w_ref[na:, :], preferred_element_type=F32))
    o_ref[...] = x_ref[...] + gate_ref[0] * _rms(y, gp_ref[...])


def _out_proj_a(a, o_f, o_b, proj, g_o, x, gate, g_post, w, rows_per_mod, tm):
    n, d = x.shape
    hv = GLA_HEADS * GLA_DV
    return pl.pallas_call(
        _out_a_kernel,
        grid=(n // tm,),
        in_specs=[pl.BlockSpec((tm, a.shape[1]), lambda i: (i, 0)),
                  pl.BlockSpec((tm, hv), lambda i: (i, 0)),
                  pl.BlockSpec((tm, hv), lambda i: (i, 0)),
                  pl.BlockSpec((tm, hv), lambda i: (i, 3)),
                  _const_spec((1, GLA_DV)),
                  pl.BlockSpec((tm, d), lambda i: (i, 0)),
                  pl.BlockSpec((1, 1, d), lambda i: ((i * tm) // rows_per_mod, 0, 0)),
                  _const_spec((1, d)),
                  _const_spec(w.shape)],
        out_specs=pl.BlockSpec((tm, d), lambda i: (i, 0)),
        out_shape=jax.ShapeDtypeStruct((n, d), F32),
        compiler_params=_cparams(("arbitrary",)),
        name="out_proj_a",
    )(a, o_f, o_b, proj, g_o, x, gate, g_post, w)


def _out_c_kernel(a_ref, x_ref, gate_ref, gp_ref, w_ref, o_ref):
    y = jnp.dot(a_ref[...], w_ref[...], preferred_element_type=F32)
    o_ref[...] = x_ref[...] + gate_ref[0] * _rms(y, gp_ref[...])


def _out_proj_c(a, x, gate, g_post, w, rows_per_mod, tm):
    n, d = x.shape
    return pl.pallas_call(
        _out_c_kernel,
        grid=(n // tm,),
        in_specs=[pl.BlockSpec((tm, a.shape[1]), lambda i: (i, 0)),
                  pl.BlockSpec((tm, d), lambda i: (i, 0)),
                  pl.BlockSpec((1, 1, d), lambda i: ((i * tm) // rows_per_mod, 0, 0)),
                  _const_spec((1, d)),
                  _const_spec(w.shape)],
        out_specs=pl.BlockSpec((tm, d), lambda i: (i, 0)),
        out_shape=jax.ShapeDtypeStruct((n, d), F32),
        compiler_params=_cparams(("arbitrary",)),
        name="out_proj_c",
    )(a, x, gate, g_post, w)


def _ffn_kernel(x_ref, gpre_ref, sh_ref, sc_ref, gate_ref, gpost_ref, wg_ref, wu_ref, wd_ref, o_ref, *, fc):
    x = x_ref[...]
    h = _normmod(x, gpre_ref[...], sh_ref[0], sc_ref[0]).astype(BF16)
    f = wg_ref.shape[1]
    acc = jnp.zeros(x.shape, F32)
    for c0 in range(0, f, fc):
        g = jnp.dot(h, wg_ref[:, c0:c0 + fc], preferred_element_type=F32)
        u = jnp.dot(h, wu_ref[:, c0:c0 + fc], preferred_element_type=F32)
        act = (_silu(g) * u).astype(BF16)
        acc = acc + jnp.dot(act, wd_ref[c0:c0 + fc, :], preferred_element_type=F32)
    o_ref[...] = x + gate_ref[0] * _rms(acc, gpost_ref[...])


def _ffn(x, g_pre, shift, scale, gate, g_post, wg, wu, wd, rows_per_mod, tm):
    n, d = x.shape
    mod_idx = lambda i: ((i * tm) // rows_per_mod, 0, 0)
    mod_spec = pl.BlockSpec((1, 1, d), mod_idx)
    single = pl.Buffered(1)
    return pl.pallas_call(
        functools.partial(_ffn_kernel, fc=256),
        grid=(n // tm,),
        in_specs=[pl.BlockSpec((tm, d), lambda i: (i, 0)),
                  _const_spec((1, d)), mod_spec, mod_spec, mod_spec, _const_spec((1, d)),
                  pl.BlockSpec(wg.shape, lambda i: (0, 0), pipeline_mode=single),
                  pl.BlockSpec(wu.shape, lambda i: (0, 0), pipeline_mode=single),
                  pl.BlockSpec(wd.shape, lambda i: (0, 0), pipeline_mode=single)],
        out_specs=pl.BlockSpec((tm, d), lambda i: (i, 0)),
        out_shape=jax.ShapeDtypeStruct((n, d), F32),
        compiler_params=_cparams(("arbitrary",)),
        name="ffn_swiglu",
    )(x, g_pre, shift, scale, gate, g_post, wg, wu, wd)


def _diff_prep_kernel(x_ref, g_ref, sh_ref, sc_ref, w_ref, c_ref, sa_ref, sb_ref, q_ref, k_ref, v_ref):
    h = _normmod(x_ref[...], g_ref[...], sh_ref[0], sc_ref[0]).astype(BF16)
    qkv = jnp.dot(h, w_ref[...], preferred_element_type=F32)
    cc, sa, sb = c_ref[...], sa_ref[...], sb_ref[...]
    width = q_ref.shape[1]

    def rope(v):
        return v * cc + pltpu.roll(v, 96, 1) * sa + pltpu.roll(v, 32, 1) * sb

    for j in range(width // LANES):
        sl = slice(j * LANES, (j + 1) * LANES)
        q_ref[:, sl] = (rope(qkv[:, sl]) * (DIFF_DIM ** -0.5)).astype(BF16)
        k_ref[:, sl] = rope(qkv[:, width + j * LANES:width + (j + 1) * LANES]).astype(BF16)
    v_ref[...] = qkv[:, 2 * width:].astype(BF16)


def _diff_prep(x, g, shift, scale, w, cc, sa, sb, rows_per_mod, tm, rope_blocks):
    n, d = x.shape
    width = w.shape[1] // 3
    mod_idx = lambda i: ((i * tm) // rows_per_mod, 0, 0)
    rope_spec = pl.BlockSpec((tm, LANES), lambda i: (i % rope_blocks, 0))
    out_spec = pl.BlockSpec((tm, width), lambda i: (i, 0))
    return pl.pallas_call(
        _diff_prep_kernel,
        grid=(n // tm,),
        in_specs=[pl.BlockSpec((tm, d), lambda i: (i, 0)), _const_spec((1, d)),
                  pl.BlockSpec((1, 1, d), mod_idx), pl.BlockSpec((1, 1, d), mod_idx),
                  _const_spec(w.shape), rope_spec, rope_spec, rope_spec],
        out_specs=[out_spec, out_spec, out_spec],
        out_shape=[jax.ShapeDtypeStruct((n, width), BF16)] * 3,
        compiler_params=_cparams(("arbitrary",)),
        name="diff_prep",
    )(x, g, shift, scale, w, cc, sa, sb)


def _diff_attn_kernel(q_ref, kx_ref, kl_ref, vx_ref, vl_ref, lam_ref, go_ref, o_ref, *, lam_init):
    q = q_ref[...]
    lane = lax.broadcasted_iota(I32, q.shape, 1)
    lv = lam_ref[...]
    lam = (jnp.exp(jnp.sum(lv[0:1] * lv[1:2], axis=-1, keepdims=True))
           - jnp.exp(jnp.sum(lv[2:3] * lv[3:4], axis=-1, keepdims=True)) + lam_init)
    kx, kl, vx, vl = kx_ref[...], kl_ref[...], vx_ref[...], vl_ref[...]

    def softmax_pv(qm):
        sx = lax.dot_general(qm, kx, _NT, preferred_element_type=F32)
        sl = lax.dot_general(qm, kl, _NT, preferred_element_type=F32)
        m = jnp.maximum(jnp.max(sx, axis=-1, keepdims=True), jnp.max(sl, axis=-1, keepdims=True))
        px = jnp.exp(sx - m)
        p_l = jnp.exp(sl - m)
        l = jnp.sum(px, axis=-1, keepdims=True) + jnp.sum(p_l, axis=-1, keepdims=True)
        o = (jnp.dot(px.astype(BF16), vx, preferred_element_type=F32)
             + jnp.dot(p_l.astype(BF16), vl, preferred_element_type=F32))
        return o / l

    zero = jnp.zeros_like(q)
    o = softmax_pv(jnp.where(lane < DIFF_DIM, q, zero)) - lam * softmax_pv(jnp.where(lane >= DIFF_DIM, q, zero))
    o_ref[...] = (_rms(o, go_ref[...]) * (1.0 - lam_init)).astype(o_ref.dtype)


def _diff_attention(q, kx, kl, vx, vl, lamvec, g_o, batch, tq, lam_init):
    n = q.shape[0]
    nq = n // batch // tq
    tx = kx.shape[0] // batch
    tl = kl.shape[0] // batch
    return pl.pallas_call(
        functools.partial(_diff_attn_kernel, lam_init=lam_init),
        grid=(batch, DIFF_HEADS, nq),
        in_specs=[pl.BlockSpec((tq, LANES), lambda b, h, i: (b * nq + i, h)),
                  pl.BlockSpec((tx, LANES), lambda b, h, i: (b, h)),
                  pl.BlockSpec((tl, LANES), lambda b, h, i: (b, h)),
                  pl.BlockSpec((tx, LANES), lambda b, h, i: (b, DIFF_HEADS + h)),
                  pl.BlockSpec((tl, LANES), lambda b, h, i: (b, h)),
                  _const_spec(lamvec.shape), _const_spec((1, LANES))],
        out_specs=pl.BlockSpec((tq, LANES), lambda b, h, i: (b * nq + i, h)),
        out_shape=jax.ShapeDtypeStruct((n, DIFF_HEADS * LANES), BF16),
        compiler_params=_cparams(("arbitrary", "arbitrary", "arbitrary")),
        name="diff_attention",
    )(q, kx, kl, vx, vl, lamvec, g_o)


def _router_kernel(x_ref, g_ref, sh_ref, sc_ref, wr_ref, h_ref, meta_ref, gates_ref, cnt_ref, carry_ref):
    i = pl.program_id(0)

    @pl.when(i == 0)
    def _():
        carry_ref[...] = jnp.zeros_like(carry_ref)

    h = _normmod(x_ref[...], g_ref[...], sh_ref[0], sc_ref[0])
    h_ref[...] = h
    tm = h.shape[0]
    logits = jnp.dot(h, wr_ref[...], precision=lax.Precision.HIGHEST, preferred_element_type=F32)
    lane = lax.broadcasted_iota(I32, logits.shape, 1).astype(F32)
    neg = jnp.float32(-jnp.inf)
    logits = jnp.where(lane < N_EXPERTS, logits, neg)
    m0 = jnp.max(logits, axis=-1, keepdims=True)
    i0 = jnp.min(jnp.where(logits == m0, lane, float(LANES)), axis=-1, keepdims=True)
    rest = jnp.where(lane == i0, neg, logits)
    m1 = jnp.max(rest, axis=-1, keepdims=True)
    i1 = jnp.min(jnp.where(rest == m1, lane, float(LANES)), axis=-1, keepdims=True)
    e = jnp.exp(m1 - m0)
    g0 = 1.0 / (1.0 + e)
    g1 = e / (1.0 + e)
    hit = jnp.where(lane == i0, 1.0, jnp.where(lane == i1, 1.0, 0.0)).astype(F32)
    ri = lax.broadcasted_iota(I32, (tm, tm), 0)
    ci = lax.broadcasted_iota(I32, (tm, tm), 1)
    below = jnp.where(ci < ri, 1.0, 0.0).astype(BF16)
    prefix = jnp.dot(below, hit.astype(BF16), preferred_element_type=F32) + carry_ref[...]
    r0 = jnp.sum(jnp.where(lane == i0, prefix, 0.0), axis=-1, keepdims=True)
    r1 = jnp.sum(jnp.where(lane == i1, prefix, 0.0), axis=-1, keepdims=True)
    carry_ref[...] = carry_ref[...] + jnp.sum(hit, axis=0, keepdims=True)
    meta = jnp.where(lane == 0.0, i0, jnp.where(lane == 1.0, i1, jnp.where(lane == 2.0, r0, r1)))
    meta_ref[...] = meta.astype(I32)
    gates_ref[...] = jnp.where(lane == 0.0, g0, g1)
    cnt_ref[...] = carry_ref[...]


def _router(x, g, shift, scale, w_router, rows_per_mod, tm):
    n, d = x.shape
    mod_idx = lambda i: ((i * tm) // rows_per_mod, 0, 0)
    row = lambda i: (i, 0)
    return pl.pallas_call(
        _router_kernel,
        grid=(n // tm,),
        in_specs=[pl.BlockSpec((tm, d), row), _const_spec((1, d)),
                  pl.BlockSpec((1, 1, d), mod_idx), pl.BlockSpec((1, 1, d), mod_idx),
                  _const_spec(w_router.shape)],
        out_specs=[pl.BlockSpec((tm, d), row), pl.BlockSpec((tm, LANES), row), pl.BlockSpec((tm, LANES), row),
                   _const_spec((1, LANES))],
        out_shape=[jax.ShapeDtypeStruct((n, d), F32), jax.ShapeDtypeStruct((n, LANES), I32),
                   jax.ShapeDtypeStruct((n, LANES), F32), jax.ShapeDtypeStruct((1, LANES), F32)],
        scratch_shapes=[pltpu.VMEM((1, LANES), F32)],
        compiler_params=_cparams(("arbitrary",)),
        name="moe_router",
    )(x, g, shift, scale, w_router)


def _row_gather(src_hbm, idx_ref, buf, sem, rows):
    def issue(r, carry):
        t = idx_ref[0, 0, r]
        pltpu.make_async_copy(src_hbm.at[pl.ds(t, 1), :], buf.at[pl.ds(r, 1), :], sem).start()
        return carry

    lax.fori_loop(0, rows, issue, 0, unroll=8)
    pltpu.make_async_copy(src_hbm.at[pl.ds(0, rows), :], buf, sem).wait()


def _dispatch_kernel(nb_ref, idx_ref, h_hbm, o_ref, buf, sem):
    i = pl.program_id(0)

    @pl.when(i < nb_ref[0])
    def _():
        _row_gather(h_hbm, idx_ref, buf, sem, buf.shape[0])
        o_ref[...] = buf[...].astype(o_ref.dtype)

    @pl.when(i >= nb_ref[0])
    def _():
        o_ref[...] = jnp.zeros_like(o_ref)


def _dispatch(h, row_token, nb_used, nb):
    d = h.shape[1]
    bm = MOE_ROWS
    return pl.pallas_call(
        _dispatch_kernel,
        grid_spec=pltpu.PrefetchScalarGridSpec(
            num_scalar_prefetch=1,
            grid=(nb,),
            in_specs=[pl.BlockSpec((1, 1, bm), lambda i, nbu: (i, 0, 0), memory_space=pltpu.SMEM),
                      pl.BlockSpec(memory_space=pl.ANY)],
            out_specs=pl.BlockSpec((bm, d), lambda i, nbu: (i, 0)),
            scratch_shapes=[pltpu.VMEM((bm, d), F32), pltpu.SemaphoreType.DMA(())]),
        out_shape=jax.ShapeDtypeStruct((nb * bm, d), BF16),
        compiler_params=_cparams(("arbitrary",)),
        name="moe_dispatch",
    )(nb_used, row_token.reshape(nb, 1, bm), h)


def _expert_kernel(nb_ref, be_ref, x_ref, wg_ref, wu_ref, wd_ref, o_ref, acc_ref):
    i = pl.program_id(0)
    f = pl.program_id(1)
    nf = pl.num_programs(1)
    live = i < nb_ref[0]

    @pl.when(live & (f == 0))
    def _():
        acc_ref[...] = jnp.zeros_like(acc_ref)

    @pl.when(live)
    def _():
        x = x_ref[...]
        g = jnp.dot(x, wg_ref[0], preferred_element_type=F32)
        u = jnp.dot(x, wu_ref[0], preferred_element_type=F32)
        act = (_silu(g) * u).astype(BF16)
        acc_ref[...] += jnp.dot(act, wd_ref[0], preferred_element_type=F32)

    @pl.when(live & (f == nf - 1))
    def _():
        o_ref[...] = acc_ref[...]

    @pl.when(jnp.logical_not(live) & (f == nf - 1))
    def _():
        o_ref[...] = jnp.zeros_like(o_ref)


def _experts(xs, blk_e, nb_used, wg, wu, wd, fc):
    cap, d = xs.shape
    bm = MOE_ROWS
    nb = cap // bm
    nf = wg.shape[2] // fc

    def blk(i, nbu):
        return jnp.minimum(i, nbu[0] - 1)

    def fidx(i, f, nbu):
        return jnp.where(i < nbu[0], f, nf - 1)

    return pl.pallas_call(
        _expert_kernel,
        grid_spec=pltpu.PrefetchScalarGridSpec(
            num_scalar_prefetch=2,
            grid=(nb, nf),
            in_specs=[pl.BlockSpec((bm, d), lambda i, f, nbu, be: (blk(i, nbu), 0)),
                      pl.BlockSpec((1, d, fc), lambda i, f, nbu, be: (be[blk(i, nbu)], 0, fidx(i, f, nbu))),
                      pl.BlockSpec((1, d, fc), lambda i, f, nbu, be: (be[blk(i, nbu)], 0, fidx(i, f, nbu))),
                      pl.BlockSpec((1, fc, d), lambda i, f, nbu, be: (be[blk(i, nbu)], fidx(i, f, nbu), 0))],
            out_specs=pl.BlockSpec((bm, d), lambda i, f, nbu, be: (i, 0)),
            scratch_shapes=[pltpu.VMEM((bm, d), F32)]),
        out_shape=jax.ShapeDtypeStruct((cap, d), F32),
        compiler_params=_cparams(("arbitrary", "arbitrary")),
        name="moe_experts",
    )(nb_used, blk_e, xs, wg, wu, wd)


def _combine_kernel(d0_ref, d1_ref, ys_hbm, gates_ref, x_ref, gate_ref, gp_ref, o_ref, y0, y1, sem0, sem1):
    rows = y0.shape[0]

    def issue(r, carry):
        pltpu.make_async_copy(ys_hbm.at[pl.ds(d0_ref[0, 0, r], 1), :], y0.at[pl.ds(r, 1), :], sem0).start()
        pltpu.make_async_copy(ys_hbm.at[pl.ds(d1_ref[0, 0, r], 1), :], y1.at[pl.ds(r, 1), :], sem1).start()
        return carry

    lax.fori_loop(0, rows, issue, 0, unroll=8)
    pltpu.make_async_copy(ys_hbm.at[pl.ds(0, rows), :], y0, sem0).wait()
    pltpu.make_async_copy(ys_hbm.at[pl.ds(0, rows), :], y1, sem1).wait()
    gt = gates_ref[...]
    f = y0[...] * gt[:, 0:1] + y1[...] * gt[:, 1:2]
    o_ref[...] = x_ref[...] + gate_ref[0] * _rms(f, gp_ref[...])


def _combine(ys, dest, gates, x, gate, g_post, rows_per_mod, tm):
    n, d = x.shape
    nt = n // tm
    d0 = dest[:, 0].reshape(nt, 1, tm)
    d1 = dest[:, 1].reshape(nt, 1, tm)
    smem_idx = pl.BlockSpec((1, 1, tm), lambda i: (i, 0, 0), memory_space=pltpu.SMEM)
    return pl.pallas_call(
        _combine_kernel,
        grid=(nt,),
        in_specs=[smem_idx, smem_idx, pl.BlockSpec(memory_space=pl.ANY),
                  pl.BlockSpec((tm, LANES), lambda i: (i, 0)),
                  pl.BlockSpec((tm, d), lambda i: (i, 0)),
                  pl.BlockSpec((1, 1, d), lambda i: ((i * tm) // rows_per_mod, 0, 0)),
                  _const_spec((1, d))],
        out_specs=pl.BlockSpec((tm, d), lambda i: (i, 0)),
        out_shape=jax.ShapeDtypeStruct((n, d), F32),
        scratch_shapes=[pltpu.VMEM((tm, d), F32), pltpu.VMEM((tm, d), F32),
                        pltpu.SemaphoreType.DMA(()), pltpu.SemaphoreType.DMA(())],
        compiler_params=_cparams(("arbitrary",)),
        name="moe_combine",
    )(d0, d1, ys, gates, x, gate, g_post)


def _moe(x, g_pre, shift, scale, gate, g_post, w_router, wg, wu, wd, rows_per_mod):
    n, d = x.shape
    bm = MOE_ROWS
    wr = jnp.zeros((d, LANES), F32).at[:, :N_EXPERTS].set(w_router)
    h, meta, gates, counts = _router(x, g_pre, shift, scale, wr, rows_per_mod, 512)
    cnt = counts[0, :N_EXPERTS].astype(I32)
    padded = (cnt + bm - 1) // bm * bm
    pend = jnp.cumsum(padded)
    pstart = pend - padded
    dest = pstart[meta[:, 0:2]] + meta[:, 2:4]
    cap = 2 * n + N_EXPERTS * bm
    nb = cap // bm
    tok = jnp.broadcast_to(jnp.arange(n, dtype=I32)[:, None], (n, 2))
    row_token = jnp.zeros((cap,), I32).at[dest.reshape(-1)].set(tok.reshape(-1))
    nb_used = (pend[-1] // bm).astype(I32).reshape(1)
    blk_e = jnp.minimum(jnp.searchsorted(pend, jnp.arange(nb, dtype=I32) * bm, side='right'),
                        N_EXPERTS - 1).astype(I32)
    xs = _dispatch(h, row_token, nb_used, nb)
    ys = _experts(xs, blk_e, nb_used, wg, wu, wd, 512)
    return _combine(ys, dest, gates, x, gate, g_post, rows_per_mod, 256)


def _rope_angles(n_tokens, rot_dim):
    rows = n_tokens // GRID_W
    row = jnp.repeat(jnp.arange(rows, dtype=F32), GRID_W)
    col = jnp.tile(jnp.arange(GRID_W, dtype=F32), rows)
    n_freq = rot_dim // 4
    freq = ROPE_BASE ** (-jnp.arange(n_freq, dtype=F32) / n_freq)
    ang = jnp.concatenate([row[:, None] * freq, col[:, None] * freq], axis=-1)
    return jnp.cos(ang), jnp.sin(ang)


def _rot_cols(w):
    half = w.shape[-1] // 2
    return jnp.concatenate([-w[..., half:], w[..., :half]], axis=-1)


def kernel(x, c, ctx, c_ctx, w_mod, b_mod, g_pre_mix, g_post_mix, g_pre_ffn, g_post_ffn,
           w_in_a, mla_g_q, mla_w_uq, mla_g_kv, mla_w_ukv,
           gla_w_gate_f, gla_b_gate_f, gla_w_gate_b, gla_b_gate_b, gla_g_out, w_out_a,
           w_qkv_c, diff_lq1, diff_lk1, diff_lq2, diff_lk2, diff_g_out, w_out_c,
           ffn_w_gate, ffn_w_up, ffn_w_down,
           moe_w_router, moe_w_gate, moe_w_up, moe_w_down):
    bsz, t, d = x.shape
    tx = ctx.shape[1]
    n, nx = bsz * t, bsz * tx
    xl = x.reshape(n, d)
    xc = ctx.reshape(nx, d)

    rows = -(-(bsz + 1) // 8) * 8
    c_all = jnp.zeros((rows, d), F32).at[:bsz].set(c).at[bsz].set(c_ctx)
    mod_all = _modulation(c_all, w_mod, b_mod).reshape(2, rows, 6, d)

    def mods(i):
        lat = [mod_all[i, :bsz, k].reshape(bsz, 1, d) for k in range(6)]
        cx = [mod_all[i, bsz, k].reshape(1, 1, d) for k in range(6)]
        return lat, cx

    row2 = lambda v: v.reshape(1, -1)

    ml, mx = mods(0)
    wi = w_in_a[0]
    cq, ckv, kr, gq, gk, gv, gr, af, ab = jnp.split(
        wi, [256, 384, 448, 704, 960, 1472, 1984, 2000], axis=-1)
    w_in = jnp.concatenate([cq, ckv, kr, _rot_cols(kr), gq, gk, gv, gr, af, ab,
                            jnp.zeros((d, LANES - 2 * GLA_RANK), F32)], axis=-1).astype(BF16)
    uq = mla_w_uq[0].reshape(MLA_Q_RANK, MLA_HEADS, MLA_NOPE + MLA_ROPE)
    wq = jnp.concatenate([uq[..., :MLA_NOPE], uq[..., MLA_NOPE:], _rot_cols(uq[..., MLA_NOPE:])],
                         axis=-1).reshape(MLA_Q_RANK, MLA_HEADS * 2 * LANES).astype(BF16)
    ukv = mla_w_ukv[0].reshape(MLA_KV_RANK, MLA_HEADS, MLA_NOPE + MLA_V)
    wkv = jnp.concatenate([ukv[..., :MLA_NOPE].reshape(MLA_KV_RANK, -1),
                           ukv[..., MLA_NOPE:].reshape(MLA_KV_RANK, -1)], axis=-1).astype(BF16)
    cos_a, sin_a = _rope_angles(t, MLA_ROPE)
    zpad = jnp.zeros((t, LANES - MLA_ROPE), F32)
    c2 = jnp.concatenate([cos_a, cos_a, zpad], axis=-1)
    s2 = jnp.concatenate([sin_a, sin_a, zpad], axis=-1)
    lane = jnp.arange(LANES)
    c2x = jnp.broadcast_to(jnp.where(lane < MLA_ROPE, 1.0, 0.0).astype(F32), (tx, LANES))
    s2x = jnp.zeros((tx, LANES), F32)

    proj_l = _norm_proj(xl, row2(g_pre_mix[0]), ml[0], ml[1], w_in, t, 512, F32, "in_proj_a")
    proj_x = _norm_proj(xc, row2(g_pre_mix[0]), mx[0], mx[1], w_in, nx, 512, F32, "in_proj_a_ctx")
    gq_, gkv_ = row2(mla_g_q[0]), row2(mla_g_kv[0])
    q_l, k_l, v_l = _mla_prep(proj_l, gq_, gkv_, wq, wkv, c2, s2, 512, t // 512)
    q_x, k_x, v_x = _mla_prep(proj_x, gq_, gkv_, wq, wkv, c2x, s2x, tx, 0)
    a_l = _attention(q_l, [k_x, k_l], [v_x, v_l], bsz, MLA_HEADS, 2 * LANES, MLA_V, 512, "mla_attention")
    a_x = _attention(q_x, [k_x], [v_x], bsz, MLA_HEADS, 2 * LANES, MLA_V, tx, "mla_attention_ctx")

    hk = GLA_HEADS * GLA_DK
    hv = GLA_HEADS * GLA_DV
    wf = jnp.zeros((LANES, hk), F32).at[:GLA_RANK].set(gla_w_gate_f[0]).astype(BF16)
    wb = jnp.zeros((LANES, hk), F32).at[GLA_RANK:2 * GLA_RANK].set(gla_w_gate_b[0]).astype(BF16)
    e_mat = ((jnp.arange(hk)[:, None] // GLA_DK) == (jnp.arange(hv)[None, :] // GLA_DV)).astype(BF16)
    s0 = jnp.zeros((bsz, hv, hk), F32)
    gla_args = (wf, row2(gla_b_gate_f[0]), wb, row2(gla_b_gate_b[0]), e_mat)
    ox_f, ox_b, sx_f, sx_b = _gla_scan(proj_x, *gla_args, s0, s0, bsz)
    ol_f, ol_b, _, _ = _gla_scan(proj_l, *gla_args, sx_f, sx_b, bsz)

    w_out = w_out_a[0].astype(BF16)
    g_o = row2(gla_g_out[0])
    xl = _out_proj_a(a_l, ol_f, ol_b, proj_l, g_o, xl, ml[2], row2(g_post_mix[0]), w_out, t, 512)
    xc = _out_proj_a(a_x, ox_f, ox_b, proj_x, g_o, xc, mx[2], row2(g_post_mix[0]), w_out, nx, 512)

    wg, wu, wd = ffn_w_gate[0].astype(BF16), ffn_w_up[0].astype(BF16), ffn_w_down[0].astype(BF16)
    gpf, gqf = row2(g_pre_ffn[0]), row2(g_post_ffn[0])
    xl = _ffn(xl, gpf, ml[3], ml[4], ml[5], gqf, wg, wu, wd, t, 512)
    xc = _ffn(xc, gpf, mx[3], mx[4], mx[5], gqf, wg, wu, wd, nx, 512)

    ml, mx = mods(1)
    lam_init = 0.8 - 0.6 * math.exp(-0.3 * 1)
    w_qkv = w_qkv_c[0].astype(BF16)
    width = DIFF_HEADS * 2 * DIFF_DIM
    cos_c, sin_c = _rope_angles(t, DIFF_DIM)
    z32 = jnp.zeros_like(sin_c)
    cc = jnp.concatenate([cos_c] * 4, axis=-1)
    sa = jnp.concatenate([-sin_c, z32, -sin_c, z32], axis=-1)
    sb = jnp.concatenate([z32, sin_c, z32, sin_c], axis=-1)
    q1, k1, v1 = _diff_prep(xl, row2(g_pre_mix[1]), ml[0], ml[1], w_qkv, cc, sa, sb, t, 512, t // 512)
    kv_x = _norm_proj(xc, row2(g_pre_mix[1]), mx[0], mx[1], w_qkv[:, width:], nx, 512, BF16, "diff_kv_ctx")
    lamvec = jnp.zeros((8, DIFF_DIM), F32).at[0].set(diff_lq1[0]).at[1].set(diff_lk1[0]) \
        .at[2].set(diff_lq2[0]).at[3].set(diff_lk2[0])
    a1 = _diff_attention(q1, kv_x, k1, kv_x, v1, lamvec, row2(diff_g_out[0]), bsz, 256, lam_init)
    xl = _out_proj_c(a1, xl, ml[2], row2(g_post_mix[1]), w_out_c[0].astype(BF16), t, 512)

    xl = _moe(xl, row2(g_pre_ffn[1]), ml[3], ml[4], ml[5], row2(g_post_ffn[1]), moe_w_router[0],
              moe_w_gate[0].astype(BF16), moe_w_up[0].astype(BF16), moe_w_down[0].astype(BF16), t)
    return xl.reshape(bsz, t, d)
```

```python
import functools
import math

import jax
import jax.numpy as jnp
from jax import lax
from jax.experimental import pallas as pl
from jax.experimental.pallas import tpu as pltpu

F32 = jnp.float32
BF16 = jnp.bfloat16
I32 = jnp.int32

EPS = 1e-6
ROPE_BASE = 10000.0
GRID_W = 64

D_MODEL = 1024
MLA_HEADS = 4
MLA_Q_RANK = 256
MLA_KV_RANK = 128
MLA_NOPE = 128
MLA_ROPE = 64
MLA_V = 128
GLA_HEADS = 4
GLA_DK = 64
GLA_DV = 128
GLA_RANK = 16
GLA_TAU = 16.0
DIFF_HEADS = 8
DIFF_DIM = 64
N_EXPERTS = 8
LANES = 128
GLA_CHUNK = 128
GLA_SUB = 16
MOE_ROWS = 512
VMEM_LIMIT = 56 * 1024 * 1024

LOG2E = math.log2(math.e)
_NT = (((1,), (1,)), ((), ()))


def _cparams(sem):
    return pltpu.CompilerParams(dimension_semantics=sem, vmem_limit_bytes=VMEM_LIMIT)


def _rms(xf, g):
    return xf * lax.rsqrt(jnp.mean(xf * xf, axis=-1, keepdims=True) + EPS) * g


def _normmod(x, g, shift, scale):
    return _rms(x.astype(F32), g) * (1.0 + scale) + shift


def _silu(x):
    return x * (1.0 / (1.0 + jnp.exp(-x)))


def _const_spec(shape):
    n = len(shape)
    return pl.BlockSpec(shape, lambda *_: (0,) * n)


def _mod_kernel(c_ref, w_ref, b_ref, o_ref):
    s = _silu(c_ref[...])
    o_ref[0] = jnp.dot(s.astype(BF16), w_ref[0].astype(BF16), preferred_element_type=F32) + b_ref[0]


def _modulation(c_all, w_mod, b_mod):
    depth, d, n6 = w_mod.shape
    rows = c_all.shape[0]
    tn = 1536
    return pl.pallas_call(
        _mod_kernel,
        grid=(depth, n6 // tn),
        in_specs=[pl.BlockSpec((rows, d), lambda i, j: (0, 0)),
                  pl.BlockSpec((1, d, tn), lambda i, j: (i, 0, j)),
                  pl.BlockSpec((1, 1, tn), lambda i, j: (i, 0, j))],
        out_specs=pl.BlockSpec((1, rows, tn), lambda i, j: (i, 0, j)),
        out_shape=jax.ShapeDtypeStruct((depth, rows, n6), F32),
        compiler_params=_cparams(("arbitrary", "arbitrary")),
        name="modulation",
    )(c_all, w_mod, b_mod.reshape(depth, 1, n6))


def _proj_kernel(x_ref, g_ref, sh_ref, sc_ref, w_ref, o_ref):
    h = _normmod(x_ref[...], g_ref[...], sh_ref[0], sc_ref[0])
    o_ref[...] = jnp.dot(h.astype(BF16), w_ref[...], preferred_element_type=F32).astype(o_ref.dtype)


def _norm_proj(x, g, shift, scale, w, rows_per_mod, tm, out_dtype, name):
    n, d = x.shape
    nout = w.shape[1]
    mod_idx = lambda i: ((i * tm) // rows_per_mod, 0, 0)
    return pl.pallas_call(
        _proj_kernel,
        grid=(n // tm,),
        in_specs=[pl.BlockSpec((tm, d), lambda i: (i, 0)),
                  _const_spec((1, d)),
                  pl.BlockSpec((1, 1, d), mod_idx),
                  pl.BlockSpec((1, 1, d), mod_idx),
                  _const_spec((d, nout))],
        out_specs=pl.BlockSpec((tm, nout), lambda i: (i, 0)),
        out_shape=jax.ShapeDtypeStruct((n, nout), out_dtype),
        compiler_params=_cparams(("arbitrary",)),
        name=name,
    )(x, g, shift, scale, w)


def _mla_prep_kernel(cq_ref, ckv_ref, kr_ref, gq_ref, gkv_ref, wq_ref, wkv_ref, c2_ref, s2_ref,
                     q_ref, k_ref, v_ref, *, scale):
    c2 = c2_ref[...]
    s2 = s2_ref[...]

    def rope2(v):
        return v * c2 + pltpu.roll(v, 64, 1) * s2

    q = jnp.dot(_rms(cq_ref[...], gq_ref[...]).astype(BF16), wq_ref[...], preferred_element_type=F32)
    kv = jnp.dot(_rms(ckv_ref[...], gkv_ref[...]).astype(BF16), wkv_ref[...], preferred_element_type=F32)
    krope = rope2(kr_ref[...]).astype(BF16)
    for h in range(MLA_HEADS):
        b = h * 2 * LANES
        q_ref[:, b:b + LANES] = (q[:, b:b + LANES] * scale).astype(BF16)
        q_ref[:, b + LANES:b + 2 * LANES] = (rope2(q[:, b + LANES:b + 2 * LANES]) * scale).astype(BF16)
        k_ref[:, b:b + LANES] = kv[:, h * LANES:(h + 1) * LANES].astype(BF16)
        k_ref[:, b + LANES:b + 2 * LANES] = krope
    v_ref[...] = kv[:, MLA_HEADS * LANES:].astype(BF16)


def _mla_prep(proj, gq, gkv, wq, wkv, c2, s2, tm, rope_blocks):
    n = proj.shape[0]
    hq = MLA_HEADS * 2 * LANES
    rope_idx = (lambda i: (i % rope_blocks, 0)) if rope_blocks else (lambda i: (0, 0))
    return pl.pallas_call(
        functools.partial(_mla_prep_kernel, scale=float((MLA_NOPE + MLA_ROPE) ** -0.5) * LOG2E),
        grid=(n // tm,),
        in_specs=[pl.BlockSpec((tm, 256), lambda i: (i, 0)),
                  pl.BlockSpec((tm, 128), lambda i: (i, 2)),
                  pl.BlockSpec((tm, 128), lambda i: (i, 3)),
                  _const_spec((1, MLA_Q_RANK)), _const_spec((1, MLA_KV_RANK)),
                  _const_spec(wq.shape), _const_spec(wkv.shape),
                  pl.BlockSpec((tm, LANES), rope_idx), pl.BlockSpec((tm, LANES), rope_idx)],
        out_specs=[pl.BlockSpec((tm, hq), lambda i: (i, 0)),
                   pl.BlockSpec((tm, hq), lambda i: (i, 0)),
                   pl.BlockSpec((tm, MLA_HEADS * MLA_V), lambda i: (i, 0))],
        out_shape=[jax.ShapeDtypeStruct((n, hq), BF16), jax.ShapeDtypeStruct((n, hq), BF16),
                   jax.ShapeDtypeStruct((n, MLA_HEADS * MLA_V), BF16)],
        compiler_params=_cparams(("arbitrary",)),
        name="mla_prep",
    )(proj, proj, proj, gq, gkv, wq, wkv, c2, s2)


def _fill_kv(k_refs, v_refs, k_scr, v_scr):
    r0 = 0
    dv = v_refs[0].shape[1]
    for k_ref, v_ref in zip(k_refs, v_refs):
        r1 = r0 + k_ref.shape[0]
        k_scr[r0:r1, :] = k_ref[...]
        v_scr[r0:r1, 0:dv] = v_ref[...]
        r0 = r1
    v_scr[:, dv:] = jnp.ones((v_scr.shape[0], v_scr.shape[1] - dv), v_scr.dtype)


def _softmax_pv(q, k, v1, dv):
    s = lax.dot_general(q, k, _NT, preferred_element_type=F32)
    p = jnp.exp2(s - jnp.max(s, axis=-1, keepdims=True)).astype(BF16)
    o = jnp.dot(p, v1, preferred_element_type=F32)
    return o[:, 0:dv] / o[:, dv:dv + 1]


def _attn_kernel(*refs, n_src, sub):
    q_ref = refs[0]
    k_refs = refs[1:1 + n_src]
    v_refs = refs[1 + n_src:1 + 2 * n_src]
    o_ref, k_scr, v_scr = refs[1 + 2 * n_src:]

    @pl.when(pl.program_id(2) == 0)
    def _():
        _fill_kv(k_refs, v_refs, k_scr, v_scr)

    k = k_scr[...]
    v1 = v_scr[...]
    dv = o_ref.shape[1]
    for r0 in range(0, q_ref.shape[0], sub):
        o_ref[r0:r0 + sub, :] = _softmax_pv(q_ref[r0:r0 + sub, :], k, v1, dv).astype(o_ref.dtype)


def _attention(q, ks, vs, batch, heads, dq, dv, tq, sub, name):
    nq = q.shape[0] // batch // tq
    n_src = len(ks)
    tk = sum(k.shape[0] for k in ks) // batch
    in_specs = [pl.BlockSpec((tq, dq), lambda b, h, i: (b * nq + i, h))]
    for k in ks:
        in_specs.append(pl.BlockSpec((k.shape[0] // batch, dq), lambda b, h, i: (b, h)))
    for v in vs:
        in_specs.append(pl.BlockSpec((v.shape[0] // batch, dv), lambda b, h, i: (b, h)))
    return pl.pallas_call(
        functools.partial(_attn_kernel, n_src=n_src, sub=sub),
        grid=(batch, heads, nq),
        in_specs=in_specs,
        out_specs=pl.BlockSpec((tq, dv), lambda b, h, i: (b * nq + i, h)),
        out_shape=jax.ShapeDtypeStruct((q.shape[0], heads * dv), BF16),
        scratch_shapes=[pltpu.VMEM((tk, dq), BF16), pltpu.VMEM((tk, 2 * dv), BF16)],
        compiler_params=_cparams(("arbitrary", "arbitrary", "arbitrary")),
        name=name,
    )(q, *ks, *vs)


def _log_sigmoid(z):
    return jnp.minimum(z, 0.0) - jnp.log(1.0 + jnp.exp(-jnp.abs(z)))


def _gla_chunk(q, k, v, la, s_ref, e_mat, rev):
    c = GLA_CHUNK
    hk = GLA_HEADS * GLA_DK
    ri = lax.broadcasted_iota(I32, (c, c), 0)
    ci = lax.broadcasted_iota(I32, (c, c), 1)
    tri = jnp.where((ci >= ri) if rev else (ci <= ri), 1.0, 0.0).astype(F32)
    cum = jnp.dot(tri, la, precision=lax.Precision.HIGHEST, preferred_element_type=F32)
    tot = cum[0:1] if rev else cum[c - 1:c]
    s_old = s_ref[...]
    qs = q * (GLA_DK ** -0.5)
    qh = (qs * jnp.exp(cum)).astype(BF16)
    o_inter = lax.dot_general(qh, s_old.astype(BF16), _NT, preferred_element_type=F32)
    kh = (k * jnp.exp(tot - cum)).astype(BF16)
    vb = v.astype(BF16)
    u = jnp.dot(v.T.astype(BF16), kh, preferred_element_type=F32)
    bd = (lax.broadcasted_iota(I32, u.shape, 0) // GLA_DV) == (lax.broadcasted_iota(I32, u.shape, 1) // GLA_DK)
    s_ref[...] = s_old * jnp.exp(tot) + jnp.where(bd, u, 0.0)

    lane_head = lax.broadcasted_iota(I32, (GLA_SUB, hk), 1) // GLA_DK
    row_id = lax.broadcasted_iota(I32, (c, hk), 0)
    tt = lax.broadcasted_iota(I32, (GLA_SUB, hk), 0)
    n_sub = c // GLA_SUB
    outs = []
    for blk in range(n_sub):
        r0 = blk * GLA_SUB
        q_i = qs[r0:r0 + GLA_SUB]
        c_i = cum[r0:r0 + GLA_SUB]
        k_i = k[r0:r0 + GLA_SUB]
        v_i = v[r0:r0 + GLA_SUB]
        o_i = o_inter[r0:r0 + GLA_SUB]
        has_prev = (blk < n_sub - 1) if rev else (blk > 0)
        if has_prev:
            edge = r0 + GLA_SUB if rev else r0 - 1
            anchor = cum[edge:edge + 1]
            qt = q_i * jnp.exp(c_i - anchor)
            prev = (row_id >= r0 + GLA_SUB) if rev else (row_id < r0)
            kt = jnp.where(prev, k * jnp.exp(jnp.minimum(anchor - cum, 0.0)), 0.0).astype(BF16)
            q_st = jnp.concatenate([jnp.where(lane_head == h, qt, 0.0) for h in range(GLA_HEADS)],
                                   axis=0).astype(BF16)
            att = lax.dot_general(q_st, kt, _NT, preferred_element_type=F32).astype(BF16)
            o_i = o_i + jnp.concatenate(
                [jnp.dot(att[h * GLA_SUB:(h + 1) * GLA_SUB], vb[:, h * GLA_DV:(h + 1) * GLA_DV],
                         preferred_element_type=F32) for h in range(GLA_HEADS)], axis=1)
        terms = []
        for s in range(GLA_SUB):
            valid = (tt <= s) if rev else (tt >= s)
            w = q_i * jnp.exp(jnp.minimum(c_i - c_i[s:s + 1], 0.0)) * k_i[s:s + 1]
            terms.append(jnp.where(valid, w, 0.0).astype(BF16))
        wsum = jnp.dot(jnp.concatenate(terms, axis=0), e_mat, preferred_element_type=F32)
        for s in range(GLA_SUB):
            o_i = o_i + wsum[s * GLA_SUB:(s + 1) * GLA_SUB] * v_i[s:s + 1]
        outs.append(o_i)
    return jnp.concatenate(outs, axis=0)


def _gla_kernel(qf_ref, kf_ref, vf_ref, gf_ref, qb_ref, kb_ref, vb_ref, gb_ref,
                wf_ref, bf_ref, wb_ref, bb_ref, e_ref, s0f_ref, s0b_ref,
                of_ref, ob_ref, sf_ref, sb_ref, stf, stb):
    j = pl.program_id(1)

    @pl.when(j == 0)
    def _():
        stf[...] = s0f_ref[0]
        stb[...] = s0b_ref[0]

    e_mat = e_ref[...]

    def log_decay(g_ref, w_ref, b_ref):
        z = jnp.dot(g_ref[...].astype(BF16), w_ref[...], preferred_element_type=F32) + b_ref[...]
        return _log_sigmoid(z) * (1.0 / GLA_TAU)

    of_ref[...] = _gla_chunk(qf_ref[...], kf_ref[...], vf_ref[...], log_decay(gf_ref, wf_ref, bf_ref),
                             stf, e_mat, False)
    ob_ref[...] = _gla_chunk(qb_ref[...], kb_ref[...], vb_ref[...], log_decay(gb_ref, wb_ref, bb_ref),
                             stb, e_mat, True)

    @pl.when(j == pl.num_programs(1) - 1)
    def _():
        sf_ref[0] = stf[...]
        sb_ref[0] = stb[...]


def _gla_scan(proj, wf, bf, wb, bb, e_mat, s0f, s0b, batch):
    n = proj.shape[0]
    c = GLA_CHUNK
    nch = n // batch // c
    hk = GLA_HEADS * GLA_DK
    hv = GLA_HEADS * GLA_DV
    fwd = lambda b, j: b * nch + j
    bwd = lambda b, j: b * nch + (nch - 1 - j)

    def specs(row):
        return [pl.BlockSpec((c, hk), lambda b, j: (row(b, j), 2)),
                pl.BlockSpec((c, hk), lambda b, j: (row(b, j), 3)),
                pl.BlockSpec((c, hv), lambda b, j: (row(b, j), 2)),
                pl.BlockSpec((c, LANES), lambda b, j: (row(b, j), 16))]

    st_spec = pl.BlockSpec((1, hv, hk), lambda b, j: (b, 0, 0))
    return pl.pallas_call(
        _gla_kernel,
        grid=(batch, nch),
        in_specs=specs(fwd) + specs(bwd) + [
            _const_spec((LANES, hk)), _const_spec((1, hk)), _const_spec((LANES, hk)), _const_spec((1, hk)),
            _const_spec(e_mat.shape), st_spec, st_spec],
        out_specs=[pl.BlockSpec((c, hv), lambda b, j: (fwd(b, j), 0)),
                   pl.BlockSpec((c, hv), lambda b, j: (bwd(b, j), 0)),
                   st_spec, st_spec],
        out_shape=[jax.ShapeDtypeStruct((n, hv), F32), jax.ShapeDtypeStruct((n, hv), F32),
                   jax.ShapeDtypeStruct((batch, hv, hk), F32), jax.ShapeDtypeStruct((batch, hv, hk), F32)],
        scratch_shapes=[pltpu.VMEM((hv, hk), F32), pltpu.VMEM((hv, hk), F32)],
        compiler_params=_cparams(("arbitrary", "arbitrary")),
        name="gla_scan",
    )(proj, proj, proj, proj, proj, proj, proj, proj, wf, bf, wb, bb, e_mat, s0f, s0b)


def _out_a_kernel(a_ref, of_ref, ob_ref, r_ref, go_ref, x_ref, gate_ref, gp_ref, w_ref, o_ref):
    o = of_ref[...] + ob_ref[...]
    r = r_ref[...]
    go = go_ref[...]
    parts = []
    for h in range(GLA_HEADS):
        sl = slice(h * GLA_DV, (h + 1) * GLA_DV)
        parts.append(_rms(o[:, sl], go) * _silu(r[:, sl]))
    g = jnp.concatenate(parts, axis=-1).astype(BF16)
    na = a_ref.shape[1]
    y = (jnp.dot(a_ref[...], w_ref[0:na, :], preferred_element_type=F32)
         + jnp.dot(g, w_ref[na:, :], preferred_element_type=F32))
    o_ref[...] = x_ref[...] + gate_ref[0] * _rms(y, gp_ref[...])


def _out_proj_a(a, o_f, o_b, proj, g_o, x, gate, g_post, w, rows_per_mod, tm):
    n, d = x.shape
    hv = GLA_HEADS * GLA_DV
    return pl.pallas_call(
        _out_a_kernel,
        grid=(n // tm,),
        in_specs=[pl.BlockSpec((tm, a.shape[1]), lambda i: (i, 0)),
                  pl.BlockSpec((tm, hv), lambda i: (i, 0)),
                  pl.BlockSpec((tm, hv), lambda i: (i, 0)),
                  pl.BlockSpec((tm, hv), lambda i: (i, 3)),
                  _const_spec((1, GLA_DV)),
                  pl.BlockSpec((tm, d), lambda i: (i, 0)),
                  pl.BlockSpec((1, 1, d), lambda i: ((i * tm) // rows_per_mod, 0, 0)),
                  _const_spec((1, d)),
                  _const_spec(w.shape)],
        out_specs=pl.BlockSpec((tm, d), lambda i: (i, 0)),
        out_shape=jax.ShapeDtypeStruct((n, d), F32),
        compiler_params=_cparams(("arbitrary",)),
        name="out_proj_a",
    )(a, o_f, o_b, proj, g_o, x, gate, g_post, w)


def _out_c_kernel(a_ref, x_ref, gate_ref, gp_ref, w_ref, o_ref):
    y = jnp.dot(a_ref[...], w_ref[...], preferred_element_type=F32)
    o_ref[...] = x_ref[...] + gate_ref[0] * _rms(y, gp_ref[...])


def _out_proj_c(a, x, gate, g_post, w, rows_per_mod, tm):
    n, d = x.shape
    return pl.pallas_call(
        _out_c_kernel,
        grid=(n // tm,),
        in_specs=[pl.BlockSpec((tm, a.shape[1]), lambda i: (i, 0)),
                  pl.BlockSpec((tm, d), lambda i: (i, 0)),
                  pl.BlockSpec((1, 1, d), lambda i: ((i * tm) // rows_per_mod, 0, 0)),
                  _const_spec((1, d)),
                  _const_spec(w.shape)],
        out_specs=pl.BlockSpec((tm, d), lambda i: (i, 0)),
        out_shape=jax.ShapeDtypeStruct((n, d), F32),
        compiler_params=_cparams(("arbitrary",)),
        name="out_proj_c",
    )(a, x, gate, g_post, w)


def _ffn_kernel(x_ref, gpre_ref, sh_ref, sc_ref, gate_ref, gpost_ref, wg_ref, wu_ref, wd_ref, o_ref, *, fc):
    x = x_ref[...]
    h = _normmod(x, gpre_ref[...], sh_ref[0], sc_ref[0]).astype(BF16)
    f = wg_ref.shape[1]
    acc = jnp.zeros(x.shape, F32)
    for c0 in range(0, f, fc):
        g = jnp.dot(h, wg_ref[:, c0:c0 + fc], preferred_element_type=F32)
        u = jnp.dot(h, wu_ref[:, c0:c0 + fc], preferred_element_type=F32)
        act = (_silu(g) * u).astype(BF16)
        acc = acc + jnp.dot(act, wd_ref[c0:c0 + fc, :], preferred_element_type=F32)
    o_ref[...] = x + gate_ref[0] * _rms(acc, gpost_ref[...])


def _ffn(x, g_pre, shift, scale, gate, g_post, wg, wu, wd, rows_per_mod, tm):
    n, d = x.shape
    mod_idx = lambda i: ((i * tm) // rows_per_mod, 0, 0)
    mod_spec = pl.BlockSpec((1, 1, d), mod_idx)
    single = pl.Buffered(1)
    return pl.pallas_call(
        functools.partial(_ffn_kernel, fc=256),
        grid=(n // tm,),
        in_specs=[pl.BlockSpec((tm, d), lambda i: (i, 0)),
                  _const_spec((1, d)), mod_spec, mod_spec, mod_spec, _const_spec((1, d)),
                  pl.BlockSpec(wg.shape, lambda i: (0, 0), pipeline_mode=single),
                  pl.BlockSpec(wu.shape, lambda i: (0, 0), pipeline_mode=single),
                  pl.BlockSpec(wd.shape, lambda i: (0, 0), pipeline_mode=single)],
        out_specs=pl.BlockSpec((tm, d), lambda i: (i, 0)),
        out_shape=jax.ShapeDtypeStruct((n, d), F32),
        compiler_params=_cparams(("arbitrary",)),
        name="ffn_swiglu",
    )(x, g_pre, shift, scale, gate, g_post, wg, wu, wd)


def _diff_prep_kernel(x_ref, g_ref, sh_ref, sc_ref, w_ref, c_ref, sa_ref, sb_ref, q_ref, k_ref, v_ref):
    h = _normmod(x_ref[...], g_ref[...], sh_ref[0], sc_ref[0]).astype(BF16)
    qkv = jnp.dot(h, w_ref[...], preferred_element_type=F32)
    cc, sa, sb = c_ref[...], sa_ref[...], sb_ref[...]
    width = q_ref.shape[1]

    def rope(v):
        return v * cc + pltpu.roll(v, 96, 1) * sa + pltpu.roll(v, 32, 1) * sb

    for j in range(width // LANES):
        sl = slice(j * LANES, (j + 1) * LANES)
        q_ref[:, sl] = (rope(qkv[:, sl]) * (DIFF_DIM ** -0.5 * LOG2E)).astype(BF16)
        k_ref[:, sl] = rope(qkv[:, width + j * LANES:width + (j + 1) * LANES]).astype(BF16)
    v_ref[...] = qkv[:, 2 * width:].astype(BF16)


def _diff_prep(x, g, shift, scale, w, cc, sa, sb, rows_per_mod, tm, rope_blocks):
    n, d = x.shape
    width = w.shape[1] // 3
    mod_idx = lambda i: ((i * tm) // rows_per_mod, 0, 0)
    rope_spec = pl.BlockSpec((tm, LANES), lambda i: (i % rope_blocks, 0))
    out_spec = pl.BlockSpec((tm, width), lambda i: (i, 0))
    return pl.pallas_call(
        _diff_prep_kernel,
        grid=(n // tm,),
        in_specs=[pl.BlockSpec((tm, d), lambda i: (i, 0)), _const_spec((1, d)),
                  pl.BlockSpec((1, 1, d), mod_idx), pl.BlockSpec((1, 1, d), mod_idx),
                  _const_spec(w.shape), rope_spec, rope_spec, rope_spec],
        out_specs=[out_spec, out_spec, out_spec],
        out_shape=[jax.ShapeDtypeStruct((n, width), BF16)] * 3,
        compiler_params=_cparams(("arbitrary",)),
        name="diff_prep",
    )(x, g, shift, scale, w, cc, sa, sb)


def _diff_attn_kernel(q_ref, kx_ref, kl_ref, vx_ref, vl_ref, lam_ref, go_ref, o_ref, k_scr, v_scr, *,
                      lam_init, sub):
    @pl.when(pl.program_id(2) == 0)
    def _():
        _fill_kv((kx_ref, kl_ref), (vx_ref, vl_ref), k_scr, v_scr)

    lv = lam_ref[...]
    lam = (jnp.exp(jnp.sum(lv[0:1] * lv[1:2], axis=-1, keepdims=True))
           - jnp.exp(jnp.sum(lv[2:3] * lv[3:4], axis=-1, keepdims=True)) + lam_init)
    k = k_scr[...]
    v1 = v_scr[...]
    go = go_ref[...]
    lane = lax.broadcasted_iota(I32, (sub, LANES), 1)
    for r0 in range(0, q_ref.shape[0], sub):
        q = q_ref[r0:r0 + sub, :]
        zero = jnp.zeros_like(q)
        o = (_softmax_pv(jnp.where(lane < DIFF_DIM, q, zero), k, v1, LANES)
             - lam * _softmax_pv(jnp.where(lane >= DIFF_DIM, q, zero), k, v1, LANES))
        o_ref[r0:r0 + sub, :] = (_rms(o, go) * (1.0 - lam_init)).astype(o_ref.dtype)


def _diff_attention(q, kx, kl, vx, vl, lamvec, g_o, batch, tq, sub, lam_init):
    n = q.shape[0]
    nq = n // batch // tq
    tx = kx.shape[0] // batch
    tl = kl.shape[0] // batch
    return pl.pallas_call(
        functools.partial(_diff_attn_kernel, lam_init=lam_init, sub=sub),
        grid=(batch, DIFF_HEADS, nq),
        in_specs=[pl.BlockSpec((tq, LANES), lambda b, h, i: (b * nq + i, h)),
                  pl.BlockSpec((tx, LANES), lambda b, h, i: (b, h)),
                  pl.BlockSpec((tl, LANES), lambda b, h, i: (b, h)),
                  pl.BlockSpec((tx, LANES), lambda b, h, i: (b, DIFF_HEADS + h)),
                  pl.BlockSpec((tl, LANES), lambda b, h, i: (b, h)),
                  _const_spec(lamvec.shape), _const_spec((1, LANES))],
        out_specs=pl.BlockSpec((tq, LANES), lambda b, h, i: (b * nq + i, h)),
        out_shape=jax.ShapeDtypeStruct((n, DIFF_HEADS * LANES), BF16),
        scratch_shapes=[pltpu.VMEM((tx + tl, LANES), BF16), pltpu.VMEM((tx + tl, 2 * LANES), BF16)],
        compiler_params=_cparams(("arbitrary", "arbitrary", "arbitrary")),
        name="diff_attention",
    )(q, kx, kl, vx, vl, lamvec, g_o)


def _router_kernel(x_ref, g_ref, sh_ref, sc_ref, wr_ref, h_ref, meta_ref, gates_ref, cnt_ref, carry_ref):
    i = pl.program_id(0)

    @pl.when(i == 0)
    def _():
        carry_ref[...] = jnp.zeros_like(carry_ref)

    h = _normmod(x_ref[...], g_ref[...], sh_ref[0], sc_ref[0])
    h_ref[...] = h
    tm = h.shape[0]
    logits = jnp.dot(h, wr_ref[...], precision=lax.Precision.HIGHEST, preferred_element_type=F32)
    lane = lax.broadcasted_iota(I32, logits.shape, 1).astype(F32)
    neg = jnp.float32(-jnp.inf)
    logits = jnp.where(lane < N_EXPERTS, logits, neg)
    m0 = jnp.max(logits, axis=-1, keepdims=True)
    i0 = jnp.min(jnp.where(logits == m0, lane, float(LANES)), axis=-1, keepdims=True)
    rest = jnp.where(lane == i0, neg, logits)
    m1 = jnp.max(rest, axis=-1, keepdims=True)
    i1 = jnp.min(jnp.where(rest == m1, lane, float(LANES)), axis=-1, keepdims=True)
    e = jnp.exp(m1 - m0)
    g0 = 1.0 / (1.0 + e)
    g1 = e / (1.0 + e)
    hit = jnp.where(lane == i0, 1.0, jnp.where(lane == i1, 1.0, 0.0)).astype(F32)
    ri = lax.broadcasted_iota(I32, (tm, tm), 0)
    ci = lax.broadcasted_iota(I32, (tm, tm), 1)
    below = jnp.where(ci < ri, 1.0, 0.0).astype(BF16)
    prefix = jnp.dot(below, hit.astype(BF16), preferred_element_type=F32) + carry_ref[...]
    r0 = jnp.sum(jnp.where(lane == i0, prefix, 0.0), axis=-1, keepdims=True)
    r1 = jnp.sum(jnp.where(lane == i1, prefix, 0.0), axis=-1, keepdims=True)
    carry_ref[...] = carry_ref[...] + jnp.sum(hit, axis=0, keepdims=True)
    meta = jnp.where(lane == 0.0, i0, jnp.where(lane == 1.0, i1, jnp.where(lane == 2.0, r0, r1)))
    meta_ref[...] = meta.astype(I32)
    gates_ref[...] = jnp.where(lane == 0.0, g0, g1)
    cnt_ref[...] = carry_ref[...]


def _router(x, g, shift, scale, w_router, rows_per_mod, tm):
    n, d = x.shape
    mod_idx = lambda i: ((i * tm) // rows_per_mod, 0, 0)
    row = lambda i: (i, 0)
    return pl.pallas_call(
        _router_kernel,
        grid=(n // tm,),
        in_specs=[pl.BlockSpec((tm, d), row), _const_spec((1, d)),
                  pl.BlockSpec((1, 1, d), mod_idx), pl.BlockSpec((1, 1, d), mod_idx),
                  _const_spec(w_router.shape)],
        out_specs=[pl.BlockSpec((tm, d), row), pl.BlockSpec((tm, LANES), row), pl.BlockSpec((tm, LANES), row),
                   _const_spec((1, LANES))],
        out_shape=[jax.ShapeDtypeStruct((n, d), F32), jax.ShapeDtypeStruct((n, LANES), I32),
                   jax.ShapeDtypeStruct((n, LANES), F32), jax.ShapeDtypeStruct((1, LANES), F32)],
        scratch_shapes=[pltpu.VMEM((1, LANES), F32)],
        compiler_params=_cparams(("arbitrary",)),
        name="moe_router",
    )(x, g, shift, scale, w_router)


def _expert_stream_kernel(be_ref, nb_ref, idx_in_ref, idx_out_ref, h_hbm, wg_ref, wu_ref, wd_ref, y_hbm,
                          xbuf, ybuf, sem_in, sem_out, *, sub, nb):
    j = pl.program_id(0)
    bm = xbuf.shape[1]
    c = j - 1
    live = (c >= 0) & (c < nb_ref[0])
    gather_ok = j < nb
    scatter_ok = j >= 2
    in_slot = j % 2
    cur = (j + 1) % 2

    def start_in(r):
        t = idx_in_ref[0, 0, r]
        pltpu.make_async_copy(h_hbm.at[pl.ds(t, 1), :], xbuf.at[in_slot, pl.ds(r, 1), :],
                              sem_in.at[in_slot]).start()

    def start_out(r):
        t = idx_out_ref[0, 0, r]
        pltpu.make_async_copy(ybuf.at[in_slot, pl.ds(r, 1), :], y_hbm.at[pl.ds(t, 1), :],
                              sem_out.at[in_slot]).start()

    def wait_in(slot):
        pltpu.make_async_copy(h_hbm.at[pl.ds(0, bm), :], xbuf.at[slot], sem_in.at[slot]).wait()

    def wait_out(slot):
        pltpu.make_async_copy(ybuf.at[slot], y_hbm.at[pl.ds(0, bm), :], sem_out.at[slot]).wait()

    def loop(fn):
        def body(r, carry):
            fn(r)
            return carry
        lax.fori_loop(0, bm, body, 0, unroll=8)

    def expert(interleave):
        x = xbuf[cur].astype(BF16)
        f_dim = wg_ref.shape[2]
        n_chunks = f_dim // sub
        per = -(-bm // n_chunks)
        part = jnp.zeros((bm, wd_ref.shape[2]), F32)
        for ci in range(n_chunks):
            c0 = ci * sub
            g = jnp.dot(x, wg_ref[0, :, c0:c0 + sub], preferred_element_type=F32)
            u = jnp.dot(x, wu_ref[0, :, c0:c0 + sub], preferred_element_type=F32)
            act = (_silu(g) * u).astype(BF16)
            part = part + jnp.dot(act, wd_ref[0, c0:c0 + sub, :], preferred_element_type=F32)
            if interleave:
                for r in range(ci * per, min((ci + 1) * per, bm)):
                    start_in(r)
                    start_out(r)
        return part

    def store_result(part):
        @pl.when(j >= 3)
        def _():
            wait_out(cur)
        ybuf[cur] = part

    @pl.when((c >= 0) & (c < nb))
    def _():
        wait_in(cur)

    steady = live & gather_ok & scatter_ok

    @pl.when(steady)
    def _():
        store_result(expert(True))

    @pl.when(jnp.logical_not(steady))
    def _():
        @pl.when(gather_ok)
        def _():
            loop(start_in)

        @pl.when(scatter_ok)
        def _():
            loop(start_out)

        @pl.when(live)
        def _():
            store_result(expert(False))

        @pl.when(jnp.logical_not(live) & (c >= 0) & (c < nb))
        def _():
            store_result(jnp.zeros(ybuf.shape[1:], F32))

        @pl.when(c >= nb)
        def _():
            wait_out(cur)
            wait_out(in_slot)


def _experts_stream(h, row_token, out_row, blk_e, nb_used, wg, wu, wd, n_out):
    n, d = h.shape
    bm = MOE_ROWS
    nb = row_token.shape[0] // bm
    single = pl.Buffered(1)

    def expert_of(j, be, nbu):
        return be[jnp.clip(j - 1, 0, nbu[0] - 1)]

    w_in_spec = pl.BlockSpec((1, d, wg.shape[2]), lambda j, be, nbu: (expert_of(j, be, nbu), 0, 0),
                             pipeline_mode=single)
    w_out_spec = pl.BlockSpec((1, wd.shape[1], d), lambda j, be, nbu: (expert_of(j, be, nbu), 0, 0),
                              pipeline_mode=single)
    return pl.pallas_call(
        functools.partial(_expert_stream_kernel, sub=256, nb=nb),
        grid_spec=pltpu.PrefetchScalarGridSpec(
            num_scalar_prefetch=2,
            grid=(nb + 2,),
            in_specs=[pl.BlockSpec((1, 1, bm), lambda j, be, nbu: (jnp.minimum(j, nb - 1), 0, 0),
                                   memory_space=pltpu.SMEM),
                      pl.BlockSpec((1, 1, bm), lambda j, be, nbu: (jnp.clip(j - 2, 0, nb - 1), 0, 0),
                                   memory_space=pltpu.SMEM),
                      pl.BlockSpec(memory_space=pl.ANY),
                      w_in_spec, w_in_spec, w_out_spec],
            out_specs=pl.BlockSpec(memory_space=pl.ANY),
            scratch_shapes=[pltpu.VMEM((2, bm, d), F32), pltpu.VMEM((2, bm, d), F32),
                            pltpu.SemaphoreType.DMA((2,)), pltpu.SemaphoreType.DMA((2,))]),
        out_shape=jax.ShapeDtypeStruct((n_out, d), F32),
        compiler_params=_cparams(("arbitrary",)),
        name="moe_experts",
    )(blk_e, nb_used, row_token.reshape(nb, 1, bm), out_row.reshape(nb, 1, bm), h, wg, wu, wd)


def _combine2_kernel(y0_ref, y1_ref, gates_ref, x_ref, gate_ref, gp_ref, o_ref):
    gt = gates_ref[...]
    f = y0_ref[...] * gt[:, 0:1] + y1_ref[...] * gt[:, 1:2]
    o_ref[...] = x_ref[...] + gate_ref[0] * _rms(f, gp_ref[...])


def _combine2(y, gates, x, gate, g_post, rows_per_mod, tm):
    n, d = x.shape
    nt = n // tm
    return pl.pallas_call(
        _combine2_kernel,
        grid=(nt,),
        in_specs=[pl.BlockSpec((tm, d), lambda i: (i, 0)),
                  pl.BlockSpec((tm, d), lambda i: (nt + i, 0)),
                  pl.BlockSpec((tm, LANES), lambda i: (i, 0)),
                  pl.BlockSpec((tm, d), lambda i: (i, 0)),
                  pl.BlockSpec((1, 1, d), lambda i: ((i * tm) // rows_per_mod, 0, 0)),
                  _const_spec((1, d))],
        out_specs=pl.BlockSpec((tm, d), lambda i: (i, 0)),
        out_shape=jax.ShapeDtypeStruct((n, d), F32),
        compiler_params=_cparams(("arbitrary",)),
        name="moe_combine",
    )(y, y, gates, x, gate, g_post)


def _moe(x, g_pre, shift, scale, gate, g_post, w_router, wg, wu, wd, rows_per_mod):
    n, d = x.shape
    bm = MOE_ROWS
    wr = jnp.zeros((d, LANES), F32).at[:, :N_EXPERTS].set(w_router)
    h, meta, gates, counts = _router(x, g_pre, shift, scale, wr, rows_per_mod, 512)
    cnt = counts[0, :N_EXPERTS].astype(I32)
    padded = (cnt + bm - 1) // bm * bm
    pend = jnp.cumsum(padded)
    pstart = pend - padded
    dest = pstart[meta[:, 0:2]] + meta[:, 2:4]
    cap = 2 * n + N_EXPERTS * bm
    nb = cap // bm
    code = 2 * jnp.arange(n, dtype=I32)[:, None] + jnp.arange(2, dtype=I32)[None, :]
    info = jnp.full((cap,), -1, I32).at[dest.reshape(-1)].set(code.reshape(-1))
    real = info >= 0
    row_token = jnp.where(real, info >> 1, 0)
    spill = 2 * n - 1 + jnp.cumsum(jnp.logical_not(real).astype(I32))
    out_row = jnp.where(real, (info & 1) * n + (info >> 1), spill)
    nb_used = (pend[-1] // bm).astype(I32).reshape(1)
    blk_e = jnp.minimum(jnp.searchsorted(pend, jnp.arange(nb, dtype=I32) * bm, side='right'),
                        N_EXPERTS - 1).astype(I32)
    y = _experts_stream(h, row_token, out_row, blk_e, nb_used, wg, wu, wd, cap)
    return _combine2(y, gates, x, gate, g_post, rows_per_mod, 512)


def _rope_angles(n_tokens, rot_dim):
    rows = n_tokens // GRID_W
    row = jnp.repeat(jnp.arange(rows, dtype=F32), GRID_W)
    col = jnp.tile(jnp.arange(GRID_W, dtype=F32), rows)
    n_freq = rot_dim // 4
    freq = ROPE_BASE ** (-jnp.arange(n_freq, dtype=F32) / n_freq)
    ang = jnp.concatenate([row[:, None] * freq, col[:, None] * freq], axis=-1)
    return jnp.cos(ang), jnp.sin(ang)


def _rot_cols(w):
    half = w.shape[-1] // 2
    return jnp.concatenate([-w[..., half:], w[..., :half]], axis=-1)


def kernel(x, c, ctx, c_ctx, w_mod, b_mod, g_pre_mix, g_post_mix, g_pre_ffn, g_post_ffn,
           w_in_a, mla_g_q, mla_w_uq, mla_g_kv, mla_w_ukv,
           gla_w_gate_f, gla_b_gate_f, gla_w_gate_b, gla_b_gate_b, gla_g_out, w_out_a,
           w_qkv_c, diff_lq1, diff_lk1, diff_lq2, diff_lk2, diff_g_out, w_out_c,
           ffn_w_gate, ffn_w_up, ffn_w_down,
           moe_w_router, moe_w_gate, moe_w_up, moe_w_down):
    bsz, t, d = x.shape
    tx = ctx.shape[1]
    n, nx = bsz * t, bsz * tx
    xl = x.reshape(n, d)
    xc = ctx.reshape(nx, d)

    rows = -(-(bsz + 1) // 8) * 8
    c_all = jnp.zeros((rows, d), F32).at[:bsz].set(c).at[bsz].set(c_ctx)
    mod_all = _modulation(c_all, w_mod, b_mod).reshape(2, rows, 6, d)

    def mods(i):
        lat = [mod_all[i, :bsz, k].reshape(bsz, 1, d) for k in range(6)]
        cx = [mod_all[i, bsz, k].reshape(1, 1, d) for k in range(6)]
        return lat, cx

    row2 = lambda v: v.reshape(1, -1)

    ml, mx = mods(0)
    wi = w_in_a[0]
    cq, ckv, kr, gq, gk, gv, gr, af, ab = jnp.split(
        wi, [256, 384, 448, 704, 960, 1472, 1984, 2000], axis=-1)
    w_in = jnp.concatenate([cq, ckv, kr, _rot_cols(kr), gq, gk, gv, gr, af, ab,
                            jnp.zeros((d, LANES - 2 * GLA_RANK), F32)], axis=-1).astype(BF16)
    uq = mla_w_uq[0].reshape(MLA_Q_RANK, MLA_HEADS, MLA_NOPE + MLA_ROPE)
    wq = jnp.concatenate([uq[..., :MLA_NOPE], uq[..., MLA_NOPE:], _rot_cols(uq[..., MLA_NOPE:])],
                         axis=-1).reshape(MLA_Q_RANK, MLA_HEADS * 2 * LANES).astype(BF16)
    ukv = mla_w_ukv[0].reshape(MLA_KV_RANK, MLA_HEADS, MLA_NOPE + MLA_V)
    wkv = jnp.concatenate([ukv[..., :MLA_NOPE].reshape(MLA_KV_RANK, -1),
                           ukv[..., MLA_NOPE:].reshape(MLA_KV_RANK, -1)], axis=-1).astype(BF16)
    cos_a, sin_a = _rope_angles(t, MLA_ROPE)
    zpad = jnp.zeros((t, LANES - MLA_ROPE), F32)
    c2 = jnp.concatenate([cos_a, cos_a, zpad], axis=-1)
    s2 = jnp.concatenate([sin_a, sin_a, zpad], axis=-1)
    lane = jnp.arange(LANES)
    c2x = jnp.broadcast_to(jnp.where(lane < MLA_ROPE, 1.0, 0.0).astype(F32), (tx, LANES))
    s2x = jnp.zeros((tx, LANES), F32)

    proj_l = _norm_proj(xl, row2(g_pre_mix[0]), ml[0], ml[1], w_in, t, 512, F32, "in_proj_a")
    proj_x = _norm_proj(xc, row2(g_pre_mix[0]), mx[0], mx[1], w_in, nx, 512, F32, "in_proj_a_ctx")
    gq_, gkv_ = row2(mla_g_q[0]), row2(mla_g_kv[0])
    q_l, k_l, v_l = _mla_prep(proj_l, gq_, gkv_, wq, wkv, c2, s2, 512, t // 512)
    q_x, k_x, v_x = _mla_prep(proj_x, gq_, gkv_, wq, wkv, c2x, s2x, tx, 0)
    a_l = _attention(q_l, [k_x, k_l], [v_x, v_l], bsz, MLA_HEADS, 2 * LANES, MLA_V, 512, 256, "mla_attention")
    a_x = _attention(q_x, [k_x], [v_x], bsz, MLA_HEADS, 2 * LANES, MLA_V, tx, tx, "mla_attention_ctx")

    hk = GLA_HEADS * GLA_DK
    hv = GLA_HEADS * GLA_DV
    wf = jnp.zeros((LANES, hk), F32).at[:GLA_RANK].set(gla_w_gate_f[0]).astype(BF16)
    wb = jnp.zeros((LANES, hk), F32).at[GLA_RANK:2 * GLA_RANK].set(gla_w_gate_b[0]).astype(BF16)
    e_mat = ((jnp.arange(hk)[:, None] // GLA_DK) == (jnp.arange(hv)[None, :] // GLA_DV)).astype(BF16)
    s0 = jnp.zeros((bsz, hv, hk), F32)
    gla_args = (wf, row2(gla_b_gate_f[0]), wb, row2(gla_b_gate_b[0]), e_mat)
    ox_f, ox_b, sx_f, sx_b = _gla_scan(proj_x, *gla_args, s0, s0, bsz)
    ol_f, ol_b, _, _ = _gla_scan(proj_l, *gla_args, sx_f, sx_b, bsz)

    w_out = w_out_a[0].astype(BF16)
    g_o = row2(gla_g_out[0])
    xl = _out_proj_a(a_l, ol_f, ol_b, proj_l, g_o, xl, ml[2], row2(g_post_mix[0]), w_out, t, 512)
    xc = _out_proj_a(a_x, ox_f, ox_b, proj_x, g_o, xc, mx[2], row2(g_post_mix[0]), w_out, nx, 512)

    wg, wu, wd = ffn_w_gate[0].astype(BF16), ffn_w_up[0].astype(BF16), ffn_w_down[0].astype(BF16)
    gpf, gqf = row2(g_pre_ffn[0]), row2(g_post_ffn[0])
    xl = _ffn(xl, gpf, ml[3], ml[4], ml[5], gqf, wg, wu, wd, t, 512)
    xc = _ffn(xc, gpf, mx[3], mx[4], mx[5], gqf, wg, wu, wd, nx, 512)

    ml, mx = mods(1)
    lam_init = 0.8 - 0.6 * math.exp(-0.3 * 1)
    w_qkv = w_qkv_c[0].astype(BF16)
    width = DIFF_HEADS * 2 * DIFF_DIM
    cos_c, sin_c = _rope_angles(t, DIFF_DIM)
    z32 = jnp.zeros_like(sin_c)
    cc = jnp.concatenate([cos_c] * 4, axis=-1)
    sa = jnp.concatenate([-sin_c, z32, -sin_c, z32], axis=-1)
    sb = jnp.concatenate([z32, sin_c, z32, sin_c], axis=-1)
    q1, k1, v1 = _diff_prep(xl, row2(g_pre_mix[1]), ml[0], ml[1], w_qkv, cc, sa, sb, t, 512, t // 512)
    kv_x = _norm_proj(xc, row2(g_pre_mix[1]), mx[0], mx[1], w_qkv[:, width:], nx, 512, BF16, "diff_kv_ctx")
    lamvec = jnp.zeros((8, DIFF_DIM), F32).at[0].set(diff_lq1[0]).at[1].set(diff_lk1[0]) \
        .at[2].set(diff_lq2[0]).at[3].set(diff_lk2[0])
    a1 = _diff_attention(q1, kv_x, k1, kv_x, v1, lamvec, row2(diff_g_out[0]), bsz, 512, 256, lam_init)
    xl = _out_proj_c(a1, xl, ml[2], row2(g_post_mix[1]), w_out_c[0].astype(BF16), t, 512)

    xl = _moe(xl, row2(g_pre_ffn[1]), ml[3], ml[4], ml[5], row2(g_post_ffn[1]), moe_w_router[0],
              moe_w_gate[0].astype(BF16), moe_w_up[0].astype(BF16), moe_w_down[0].astype(BF16), t)
    return xl.reshape(bsz, t, d)
```

```python
import functools
import math

import jax
import jax.numpy as jnp
from jax import lax
from jax.experimental import pallas as pl
from jax.experimental.pallas import tpu as pltpu

F32 = jnp.float32
BF16 = jnp.bfloat16
I32 = jnp.int32

EPS = 1e-6
ROPE_BASE = 10000.0
GRID_W = 64

D_MODEL = 1024
MLA_HEADS = 4
MLA_Q_RANK = 256
MLA_KV_RANK = 128
MLA_NOPE = 128
MLA_ROPE = 64
MLA_V = 128
GLA_HEADS = 4
GLA_DK = 64
GLA_DV = 128
GLA_RANK = 16
GLA_TAU = 16.0
DIFF_HEADS = 8
DIFF_DIM = 64
N_EXPERTS = 8
LANES = 128
GLA_CHUNK = 128
GLA_LEVELS = (64, 32, 16, 8, 4, 2, 1)
MOE_ROWS = 512
VMEM_LIMIT = 56 * 1024 * 1024

LOG2E = math.log2(math.e)
_NT = (((1,), (1,)), ((), ()))


def _cparams(sem):
    return pltpu.CompilerParams(dimension_semantics=sem, vmem_limit_bytes=VMEM_LIMIT)


def _rms(xf, g):
    return xf * lax.rsqrt(jnp.mean(xf * xf, axis=-1, keepdims=True) + EPS) * g


def _normmod(x, g, shift, scale):
    return _rms(x.astype(F32), g) * (1.0 + scale) + shift


def _silu(x):
    return x * (1.0 / (1.0 + jnp.exp(-x)))


def _const_spec(shape):
    n = len(shape)
    return pl.BlockSpec(shape, lambda *_: (0,) * n)


def _mod_kernel(c_ref, w_ref, b_ref, o_ref):
    s = _silu(c_ref[...])
    o_ref[0] = jnp.dot(s.astype(BF16), w_ref[0].astype(BF16), preferred_element_type=F32) + b_ref[0]


def _modulation(c_all, w_mod, b_mod):
    depth, d, n6 = w_mod.shape
    rows = c_all.shape[0]
    tn = 1536
    return pl.pallas_call(
        _mod_kernel,
        grid=(depth, n6 // tn),
        in_specs=[pl.BlockSpec((rows, d), lambda i, j: (0, 0)),
                  pl.BlockSpec((1, d, tn), lambda i, j: (i, 0, j)),
                  pl.BlockSpec((1, 1, tn), lambda i, j: (i, 0, j))],
        out_specs=pl.BlockSpec((1, rows, tn), lambda i, j: (i, 0, j)),
        out_shape=jax.ShapeDtypeStruct((depth, rows, n6), F32),
        compiler_params=_cparams(("arbitrary", "arbitrary")),
        name="modulation",
    )(c_all, w_mod, b_mod.reshape(depth, 1, n6))


def _proj_kernel(x_ref, g_ref, sh_ref, sc_ref, w_ref, o_ref):
    h = _normmod(x_ref[...], g_ref[...], sh_ref[0], sc_ref[0])
    o_ref[...] = jnp.dot(h.astype(BF16), w_ref[...], preferred_element_type=F32).astype(o_ref.dtype)


def _norm_proj(x, g, shift, scale, w, rows_per_mod, tm, out_dtype, name):
    n, d = x.shape
    nout = w.shape[1]
    mod_idx = lambda i: ((i * tm) // rows_per_mod, 0, 0)
    return pl.pallas_call(
        _proj_kernel,
        grid=(n // tm,),
        in_specs=[pl.BlockSpec((tm, d), lambda i: (i, 0)),
                  _const_spec((1, d)),
                  pl.BlockSpec((1, 1, d), mod_idx),
                  pl.BlockSpec((1, 1, d), mod_idx),
                  _const_spec((d, nout))],
        out_specs=pl.BlockSpec((tm, nout), lambda i: (i, 0)),
        out_shape=jax.ShapeDtypeStruct((n, nout), out_dtype),
        compiler_params=_cparams(("arbitrary",)),
        name=name,
    )(x, g, shift, scale, w)


def _mla_prep_kernel(cq_ref, ckv_ref, kr_ref, gq_ref, gkv_ref, wq_ref, wkv_ref, c2_ref, s2_ref,
                     q_ref, k_ref, v_ref, *, scale):
    c2 = c2_ref[...]
    s2 = s2_ref[...]

    def rope2(v):
        return v * c2 + pltpu.roll(v, 64, 1) * s2

    q = jnp.dot(_rms(cq_ref[...], gq_ref[...]).astype(BF16), wq_ref[...], preferred_element_type=F32)
    kv = jnp.dot(_rms(ckv_ref[...], gkv_ref[...]).astype(BF16), wkv_ref[...], preferred_element_type=F32)
    krope = rope2(kr_ref[...]).astype(BF16)
    for h in range(MLA_HEADS):
        b = h * 2 * LANES
        q_ref[:, b:b + LANES] = (q[:, b:b + LANES] * scale).astype(BF16)
        q_ref[:, b + LANES:b + 2 * LANES] = (rope2(q[:, b + LANES:b + 2 * LANES]) * scale).astype(BF16)
        k_ref[:, b:b + LANES] = kv[:, h * LANES:(h + 1) * LANES].astype(BF16)
        k_ref[:, b + LANES:b + 2 * LANES] = krope
    v_ref[...] = kv[:, MLA_HEADS * LANES:].astype(BF16)


def _mla_prep(proj, gq, gkv, wq, wkv, c2, s2, tm, rope_blocks):
    n = proj.shape[0]
    hq = MLA_HEADS * 2 * LANES
    rope_idx = (lambda i: (i % rope_blocks, 0)) if rope_blocks else (lambda i: (0, 0))
    return pl.pallas_call(
        functools.partial(_mla_prep_kernel, scale=float((MLA_NOPE + MLA_ROPE) ** -0.5) * LOG2E),
        grid=(n // tm,),
        in_specs=[pl.BlockSpec((tm, 256), lambda i: (i, 0)),
                  pl.BlockSpec((tm, 128), lambda i: (i, 2)),
                  pl.BlockSpec((tm, 128), lambda i: (i, 3)),
                  _const_spec((1, MLA_Q_RANK)), _const_spec((1, MLA_KV_RANK)),
                  _const_spec(wq.shape), _const_spec(wkv.shape),
                  pl.BlockSpec((tm, LANES), rope_idx), pl.BlockSpec((tm, LANES), rope_idx)],
        out_specs=[pl.BlockSpec((tm, hq), lambda i: (i, 0)),
                   pl.BlockSpec((tm, hq), lambda i: (i, 0)),
                   pl.BlockSpec((tm, MLA_HEADS * MLA_V), lambda i: (i, 0))],
        out_shape=[jax.ShapeDtypeStruct((n, hq), BF16), jax.ShapeDtypeStruct((n, hq), BF16),
                   jax.ShapeDtypeStruct((n, MLA_HEADS * MLA_V), BF16)],
        compiler_params=_cparams(("arbitrary",)),
        name="mla_prep",
    )(proj, proj, proj, gq, gkv, wq, wkv, c2, s2)


def _fill_kv(k_refs, v_refs, kt_scr, v_scr):
    r0 = 0
    dv = v_refs[0].shape[1]
    for k_ref, v_ref in zip(k_refs, v_refs):
        r1 = r0 + k_ref.shape[0]
        kt_scr[:, r0:r1] = k_ref[...].astype(F32).T.astype(kt_scr.dtype)
        v_scr[r0:r1, 0:dv] = v_ref[...]
        r0 = r1
    v_scr[:, dv:] = jnp.ones((v_scr.shape[0], v_scr.shape[1] - dv), v_scr.dtype)


def _softmax_pv(q, kt, v1, dv):
    s = jnp.dot(q, kt, preferred_element_type=F32)
    p = jnp.exp2(s - jnp.max(s, axis=-1, keepdims=True)).astype(BF16)
    o = jnp.dot(p, v1, preferred_element_type=F32)
    return o[:, 0:dv] / o[:, dv:dv + 1]


def _attn_kernel(*refs, n_src, sub):
    q_ref = refs[0]
    k_refs = refs[1:1 + n_src]
    v_refs = refs[1 + n_src:1 + 2 * n_src]
    o_ref, kt_scr, v_scr = refs[1 + 2 * n_src:]

    @pl.when(pl.program_id(2) == 0)
    def _():
        _fill_kv(k_refs, v_refs, kt_scr, v_scr)

    kt = kt_scr[...]
    v1 = v_scr[...]
    dv = o_ref.shape[1]
    for r0 in range(0, q_ref.shape[0], sub):
        o_ref[r0:r0 + sub, :] = _softmax_pv(q_ref[r0:r0 + sub, :], kt, v1, dv).astype(o_ref.dtype)


def _attention(q, ks, vs, batch, heads, dq, dv, tq, sub, name):
    nq = q.shape[0] // batch // tq
    n_src = len(ks)
    tk = sum(k.shape[0] for k in ks) // batch
    in_specs = [pl.BlockSpec((tq, dq), lambda b, h, i: (b * nq + i, h))]
    for k in ks:
        in_specs.append(pl.BlockSpec((k.shape[0] // batch, dq), lambda b, h, i: (b, h)))
    for v in vs:
        in_specs.append(pl.BlockSpec((v.shape[0] // batch, dv), lambda b, h, i: (b, h)))
    return pl.pallas_call(
        functools.partial(_attn_kernel, n_src=n_src, sub=sub),
        grid=(batch, heads, nq),
        in_specs=in_specs,
        out_specs=pl.BlockSpec((tq, dv), lambda b, h, i: (b * nq + i, h)),
        out_shape=jax.ShapeDtypeStruct((q.shape[0], heads * dv), BF16),
        scratch_shapes=[pltpu.VMEM((dq, tk), BF16), pltpu.VMEM((tk, 2 * dv), BF16)],
        compiler_params=_cparams(("arbitrary", "arbitrary", "arbitrary")),
        name=name,
    )(q, *ks, *vs)


def _log_sigmoid(z):
    return jnp.minimum(z, 0.0) - jnp.log(1.0 + jnp.exp(-jnp.abs(z)))


def _gla_anchor(cum, level, rev):
    c, hk = cum.shape
    two = 2 * level
    a = level if rev else level - 1
    if two >= 8:
        return jnp.concatenate(
            [jnp.broadcast_to(cum[b * two + a:b * two + a + 1], (two, hk)) for b in range(c // two)], axis=0)
    pos = lax.broadcasted_iota(I32, cum.shape, 0) & (two - 1)
    anc = cum
    for p in range(two):
        if p != a:
            anc = jnp.where(pos == p, pltpu.roll(cum, (p - a) % c, 0), anc)
    return anc


def _gla_chunk(q, k, v, la, s_ref, lvl, rev):
    c = GLA_CHUNK
    hk = GLA_HEADS * GLA_DK
    ri = lax.broadcasted_iota(I32, (c, c), 0)
    ci = lax.broadcasted_iota(I32, (c, c), 1)
    tri = jnp.where((ci >= ri) if rev else (ci <= ri), 1.0, 0.0).astype(BF16)
    la_hi = la.astype(BF16)
    rest = la - la_hi.astype(F32)
    la_mid = rest.astype(BF16)
    la_lo = (rest - la_mid.astype(F32)).astype(BF16)
    cum = (jnp.dot(tri, la_hi, preferred_element_type=F32) + jnp.dot(tri, la_mid, preferred_element_type=F32)
           + jnp.dot(tri, la_lo, preferred_element_type=F32))
    tot = cum[0:1] if rev else cum[c - 1:c]
    s_old = s_ref[...]
    qs = q * (GLA_DK ** -0.5)
    qh = (qs * jnp.exp(cum)).astype(BF16)
    o_inter = lax.dot_general(qh, s_old.astype(BF16), _NT, preferred_element_type=F32)
    kh = (k * jnp.exp(tot - cum)).astype(BF16)
    vb = v.astype(BF16)
    u = jnp.dot(v.T.astype(BF16), kh, preferred_element_type=F32)
    bd = (lax.broadcasted_iota(I32, u.shape, 0) // GLA_DV) == (lax.broadcasted_iota(I32, u.shape, 1) // GLA_DK)
    s_ref[...] = s_old * jnp.exp(tot) + jnp.where(bd, u, 0.0)

    row = lax.broadcasted_iota(I32, (c, hk), 0)
    lane_head = lax.broadcasted_iota(I32, (c, hk), 1) // GLA_DK

    def stack_heads(zb):
        zero = jnp.zeros_like(zb)
        return jnp.concatenate([jnp.where(lane_head == h, zb, zero) for h in range(GLA_HEADS)], axis=0)

    ones = jnp.ones((c, hk), BF16)
    diag = lax.dot_general(stack_heads((qs * k).astype(BF16)), ones, _NT, preferred_element_type=F32)
    att = jnp.where(lvl == 0, diag, 0.0)
    for level in GLA_LEVELS:
        upper = (row & level) != 0
        is_q = jnp.logical_not(upper) if rev else upper
        x = cum - _gla_anchor(cum, level, rev)
        zb = (jnp.where(is_q, qs, k) * jnp.exp(jnp.where(is_q, x, -x))).astype(BF16)
        gram = lax.dot_general(stack_heads(zb), zb, _NT, preferred_element_type=F32)
        att = jnp.where(lvl == level, gram, att)
    attb = att.astype(BF16)
    o_intra = jnp.concatenate(
        [jnp.dot(attb[h * c:(h + 1) * c], vb[:, h * GLA_DV:(h + 1) * GLA_DV], preferred_element_type=F32)
         for h in range(GLA_HEADS)], axis=1)
    return o_inter + o_intra


def _gla_kernel(qf_ref, kf_ref, vf_ref, gf_ref, qb_ref, kb_ref, vb_ref, gb_ref,
                wf_ref, bf_ref, wb_ref, bb_ref, lvlf_ref, lvlb_ref, s0f_ref, s0b_ref,
                of_ref, ob_ref, sf_ref, sb_ref, stf, stb):
    j = pl.program_id(1)

    @pl.when(j == 0)
    def _():
        stf[...] = s0f_ref[0]
        stb[...] = s0b_ref[0]

    def log_decay(g_ref, w_ref, b_ref):
        z = jnp.dot(g_ref[...].astype(BF16), w_ref[...], preferred_element_type=F32) + b_ref[...]
        return _log_sigmoid(z) * (1.0 / GLA_TAU)

    of_ref[...] = _gla_chunk(qf_ref[...], kf_ref[...], vf_ref[...], log_decay(gf_ref, wf_ref, bf_ref),
                             stf, lvlf_ref[...], False)
    ob_ref[...] = _gla_chunk(qb_ref[...], kb_ref[...], vb_ref[...], log_decay(gb_ref, wb_ref, bb_ref),
                             stb, lvlb_ref[...], True)

    @pl.when(j == pl.num_programs(1) - 1)
    def _():
        sf_ref[0] = stf[...]
        sb_ref[0] = stb[...]


def _gla_level_maps():
    c = GLA_CHUNK
    t = jnp.arange(c, dtype=I32)[:, None]
    s = jnp.arange(c, dtype=I32)[None, :]
    diff = t ^ s
    top = jnp.zeros((c, c), I32)
    for level in GLA_LEVELS:
        top = jnp.where((top == 0) & ((diff & level) != 0), level, top)
    fwd = jnp.where(t == s, 0, jnp.where(t > s, top, -1))
    bwd = jnp.where(t == s, 0, jnp.where(t < s, top, -1))
    return jnp.tile(fwd, (GLA_HEADS, 1)), jnp.tile(bwd, (GLA_HEADS, 1))


def _gla_scan(proj, wf, bf, wb, bb, lvl_f, lvl_b, s0f, s0b, batch):
    n = proj.shape[0]
    c = GLA_CHUNK
    nch = n // batch // c
    hk = GLA_HEADS * GLA_DK
    hv = GLA_HEADS * GLA_DV
    fwd = lambda b, j: b * nch + j
    bwd = lambda b, j: b * nch + (nch - 1 - j)

    def specs(row):
        return [pl.BlockSpec((c, hk), lambda b, j: (row(b, j), 2)),
                pl.BlockSpec((c, hk), lambda b, j: (row(b, j), 3)),
                pl.BlockSpec((c, hv), lambda b, j: (row(b, j), 2)),
                pl.BlockSpec((c, LANES), lambda b, j: (row(b, j), 16))]

    st_spec = pl.BlockSpec((1, hv, hk), lambda b, j: (b, 0, 0))
    return pl.pallas_call(
        _gla_kernel,
        grid=(batch, nch),
        in_specs=specs(fwd) + specs(bwd) + [
            _const_spec((LANES, hk)), _const_spec((1, hk)), _const_spec((LANES, hk)), _const_spec((1, hk)),
            _const_spec(lvl_f.shape), _const_spec(lvl_b.shape), st_spec, st_spec],
        out_specs=[pl.BlockSpec((c, hv), lambda b, j: (fwd(b, j), 0)),
                   pl.BlockSpec((c, hv), lambda b, j: (bwd(b, j), 0)),
                   st_spec, st_spec],
        out_shape=[jax.ShapeDtypeStruct((n, hv), F32), jax.ShapeDtypeStruct((n, hv), F32),
                   jax.ShapeDtypeStruct((batch, hv, hk), F32), jax.ShapeDtypeStruct((batch, hv, hk), F32)],
        scratch_shapes=[pltpu.VMEM((hv, hk), F32), pltpu.VMEM((hv, hk), F32)],
        compiler_params=_cparams(("arbitrary", "arbitrary")),
        name="gla_scan",
    )(proj, proj, proj, proj, proj, proj, proj, proj, wf, bf, wb, bb, lvl_f, lvl_b, s0f, s0b)


def _out_a_kernel(a_ref, of_ref, ob_ref, r_ref, go_ref, x_ref, gate_ref, gp_ref, w_ref, o_ref):
    o = of_ref[...] + ob_ref[...]
    r = r_ref[...]
    go = go_ref[...]
    parts = []
    for h in range(GLA_HEADS):
        sl = slice(h * GLA_DV, (h + 1) * GLA_DV)
        parts.append(_rms(o[:, sl], go) * _silu(r[:, sl]))
    g = jnp.concatenate(parts, axis=-1).astype(BF16)
    na = a_ref.shape[1]
    y = (jnp.dot(a_ref[...], w_ref[0:na, :], preferred_element_type=F32)
         + jnp.dot(g, w_ref[na:, :], preferred_element_type=F32))
    o_ref[...] = x_ref[...] + gate_ref[0] * _rms(y, gp_ref[...])


def _out_proj_a(a, o_f, o_b, proj, g_o, x, gate, g_post, w, rows_per_mod, tm):
    n, d = x.shape
    hv = GLA_HEADS * GLA_DV
    return pl.pallas_call(
        _out_a_kernel,
        grid=(n // tm,),
        in_specs=[pl.BlockSpec((tm, a.shape[1]), lambda i: (i, 0)),
                  pl.BlockSpec((tm, hv), lambda i: (i, 0)),
                  pl.BlockSpec((tm, hv), lambda i: (i, 0)),
                  pl.BlockSpec((tm, hv), lambda i: (i, 3)),
                  _const_spec((1, GLA_DV)),
                  pl.BlockSpec((tm, d), lambda i: (i, 0)),
                  pl.BlockSpec((1, 1, d), lambda i: ((i * tm) // rows_per_mod, 0, 0)),
                  _const_spec((1, d)),
                  _const_spec(w.shape)],
        out_specs=pl.BlockSpec((tm, d), lambda i: (i, 0)),
        out_shape=jax.ShapeDtypeStruct((n, d), F32),
        compiler_params=_cparams(("arbitrary",)),
        name="out_proj_a",
    )(a, o_f, o_b, proj, g_o, x, gate, g_post, w)


def _out_c_kernel(a_ref, x_ref, gate_ref, gp_ref, w_ref, o_ref):
    y = jnp.dot(a_ref[...], w_ref[...], preferred_element_type=F32)
    o_ref[...] = x_ref[...] + gate_ref[0] * _rms(y, gp_ref[...])


def _out_proj_c(a, x, gate, g_post, w, rows_per_mod, tm):
    n, d = x.shape
    return pl.pallas_call(
        _out_c_kernel,
        grid=(n // tm,),
        in_specs=[pl.BlockSpec((tm, a.shape[1]), lambda i: (i, 0)),
                  pl.BlockSpec((tm, d), lambda i: (i, 0)),
                  pl.BlockSpec((1, 1, d), lambda i: ((i * tm) // rows_per_mod, 0, 0)),
                  _const_spec((1, d)),
                  _const_spec(w.shape)],
        out_specs=pl.BlockSpec((tm, d), lambda i: (i, 0)),
        out_shape=jax.ShapeDtypeStruct((n, d), F32),
        compiler_params=_cparams(("arbitrary",)),
        name="out_proj_c",
    )(a, x, gate, g_post, w)


def _ffn_kernel(x_ref, gpre_ref, sh_ref, sc_ref, gate_ref, gpost_ref, wg_ref, wu_ref, wd_ref, o_ref, *, fc):
    x = x_ref[...]
    h = _normmod(x, gpre_ref[...], sh_ref[0], sc_ref[0]).astype(BF16)
    f = wg_ref.shape[1]
    acc = jnp.zeros(x.shape, F32)
    for c0 in range(0, f, fc):
        g = jnp.dot(h, wg_ref[:, c0:c0 + fc], preferred_element_type=F32)
        u = jnp.dot(h, wu_ref[:, c0:c0 + fc], preferred_element_type=F32)
        act = (_silu(g) * u).astype(BF16)
        acc = acc + jnp.dot(act, wd_ref[c0:c0 + fc, :], preferred_element_type=F32)
    o_ref[...] = x + gate_ref[0] * _rms(acc, gpost_ref[...])


def _ffn(x, g_pre, shift, scale, gate, g_post, wg, wu, wd, rows_per_mod, tm):
    n, d = x.shape
    mod_idx = lambda i: ((i * tm) // rows_per_mod, 0, 0)
    mod_spec = pl.BlockSpec((1, 1, d), mod_idx)
    single = pl.Buffered(1)
    return pl.pallas_call(
        functools.partial(_ffn_kernel, fc=256),
        grid=(n // tm,),
        in_specs=[pl.BlockSpec((tm, d), lambda i: (i, 0)),
                  _const_spec((1, d)), mod_spec, mod_spec, mod_spec, _const_spec((1, d)),
                  pl.BlockSpec(wg.shape, lambda i: (0, 0), pipeline_mode=single),
                  pl.BlockSpec(wu.shape, lambda i: (0, 0), pipeline_mode=single),
                  pl.BlockSpec(wd.shape, lambda i: (0, 0), pipeline_mode=single)],
        out_specs=pl.BlockSpec((tm, d), lambda i: (i, 0)),
        out_shape=jax.ShapeDtypeStruct((n, d), F32),
        compiler_params=_cparams(("arbitrary",)),
        name="ffn_swiglu",
    )(x, g_pre, shift, scale, gate, g_post, wg, wu, wd)


def _diff_prep_kernel(x_ref, g_ref, sh_ref, sc_ref, w_ref, c_ref, sa_ref, sb_ref, q_ref, k_ref, v_ref):
    h = _normmod(x_ref[...], g_ref[...], sh_ref[0], sc_ref[0]).astype(BF16)
    qkv = jnp.dot(h, w_ref[...], preferred_element_type=F32)
    cc, sa, sb = c_ref[...], sa_ref[...], sb_ref[...]
    width = q_ref.shape[1]

    def rope(v):
        return v * cc + pltpu.roll(v, 96, 1) * sa + pltpu.roll(v, 32, 1) * sb

    for j in range(width // LANES):
        sl = slice(j * LANES, (j + 1) * LANES)
        q_ref[:, sl] = (rope(qkv[:, sl]) * (DIFF_DIM ** -0.5 * LOG2E)).astype(BF16)
        k_ref[:, sl] = rope(qkv[:, width + j * LANES:width + (j + 1) * LANES]).astype(BF16)
    v_ref[...] = qkv[:, 2 * width:].astype(BF16)


def _diff_prep(x, g, shift, scale, w, cc, sa, sb, rows_per_mod, tm, rope_blocks):
    n, d = x.shape
    width = w.shape[1] // 3
    mod_idx = lambda i: ((i * tm) // rows_per_mod, 0, 0)
    rope_spec = pl.BlockSpec((tm, LANES), lambda i: (i % rope_blocks, 0))
    out_spec = pl.BlockSpec((tm, width), lambda i: (i, 0))
    return pl.pallas_call(
        _diff_prep_kernel,
        grid=(n // tm,),
        in_specs=[pl.BlockSpec((tm, d), lambda i: (i, 0)), _const_spec((1, d)),
                  pl.BlockSpec((1, 1, d), mod_idx), pl.BlockSpec((1, 1, d), mod_idx),
                  _const_spec(w.shape), rope_spec, rope_spec, rope_spec],
        out_specs=[out_spec, out_spec, out_spec],
        out_shape=[jax.ShapeDtypeStruct((n, width), BF16)] * 3,
        compiler_params=_cparams(("arbitrary",)),
        name="diff_prep",
    )(x, g, shift, scale, w, cc, sa, sb)


def _diff_attn_kernel(q_ref, kx_ref, kl_ref, vx_ref, vl_ref, lam_ref, go_ref, o_ref, k_scr, v_scr, *,
                      lam_init, sub):
    @pl.when(pl.program_id(2) == 0)
    def _():
        _fill_kv((kx_ref, kl_ref), (vx_ref, vl_ref), k_scr, v_scr)

    lv = lam_ref[...]
    lam = (jnp.exp(jnp.sum(lv[0:1] * lv[1:2], axis=-1, keepdims=True))
           - jnp.exp(jnp.sum(lv[2:3] * lv[3:4], axis=-1, keepdims=True)) + lam_init)
    kt = k_scr[...]
    v1 = v_scr[...]
    go = go_ref[...]
    lane = lax.broadcasted_iota(I32, (sub, LANES), 1)
    for r0 in range(0, q_ref.shape[0], sub):
        q = q_ref[r0:r0 + sub, :]
        zero = jnp.zeros_like(q)
        o = (_softmax_pv(jnp.where(lane < DIFF_DIM, q, zero), kt, v1, LANES)
             - lam * _softmax_pv(jnp.where(lane >= DIFF_DIM, q, zero), kt, v1, LANES))
        o_ref[r0:r0 + sub, :] = (_rms(o, go) * (1.0 - lam_init)).astype(o_ref.dtype)


def _diff_attention(q, kx, kl, vx, vl, lamvec, g_o, batch, tq, sub, lam_init):
    n = q.shape[0]
    nq = n // batch // tq
    tx = kx.shape[0] // batch
    tl = kl.shape[0] // batch
    return pl.pallas_call(
        functools.partial(_diff_attn_kernel, lam_init=lam_init, sub=sub),
        grid=(batch, DIFF_HEADS, nq),
        in_specs=[pl.BlockSpec((tq, LANES), lambda b, h, i: (b * nq + i, h)),
                  pl.BlockSpec((tx, LANES), lambda b, h, i: (b, h)),
                  pl.BlockSpec((tl, LANES), lambda b, h, i: (b, h)),
                  pl.BlockSpec((tx, LANES), lambda b, h, i: (b, DIFF_HEADS + h)),
                  pl.BlockSpec((tl, LANES), lambda b, h, i: (b, h)),
                  _const_spec(lamvec.shape), _const_spec((1, LANES))],
        out_specs=pl.BlockSpec((tq, LANES), lambda b, h, i: (b * nq + i, h)),
        out_shape=jax.ShapeDtypeStruct((n, DIFF_HEADS * LANES), BF16),
        scratch_shapes=[pltpu.VMEM((LANES, tx + tl), BF16), pltpu.VMEM((tx + tl, 2 * LANES), BF16)],
        compiler_params=_cparams(("arbitrary", "arbitrary", "arbitrary")),
        name="diff_attention",
    )(q, kx, kl, vx, vl, lamvec, g_o)


def _router_kernel(x_ref, g_ref, sh_ref, sc_ref, wr_ref, h_ref, meta_ref, gates_ref, cnt_ref, carry_ref):
    i = pl.program_id(0)

    @pl.when(i == 0)
    def _():
        carry_ref[...] = jnp.zeros_like(carry_ref)

    h = _normmod(x_ref[...], g_ref[...], sh_ref[0], sc_ref[0])
    h_ref[...] = h
    tm = h.shape[0]
    logits = jnp.dot(h, wr_ref[...], precision=lax.Precision.HIGHEST, preferred_element_type=F32)
    lane = lax.broadcasted_iota(I32, logits.shape, 1).astype(F32)
    neg = jnp.float32(-jnp.inf)
    logits = jnp.where(lane < N_EXPERTS, logits, neg)
    m0 = jnp.max(logits, axis=-1, keepdims=True)
    i0 = jnp.min(jnp.where(logits == m0, lane, float(LANES)), axis=-1, keepdims=True)
    rest = jnp.where(lane == i0, neg, logits)
    m1 = jnp.max(rest, axis=-1, keepdims=True)
    i1 = jnp.min(jnp.where(rest == m1, lane, float(LANES)), axis=-1, keepdims=True)
    e = jnp.exp(m1 - m0)
    g0 = 1.0 / (1.0 + e)
    g1 = e / (1.0 + e)
    hit = jnp.where(lane == i0, 1.0, jnp.where(lane == i1, 1.0, 0.0)).astype(F32)
    ri = lax.broadcasted_iota(I32, (tm, tm), 0)
    ci = lax.broadcasted_iota(I32, (tm, tm), 1)
    below = jnp.where(ci < ri, 1.0, 0.0).astype(BF16)
    prefix = jnp.dot(below, hit.astype(BF16), preferred_element_type=F32) + carry_ref[...]
    r0 = jnp.sum(jnp.where(lane == i0, prefix, 0.0), axis=-1, keepdims=True)
    r1 = jnp.sum(jnp.where(lane == i1, prefix, 0.0), axis=-1, keepdims=True)
    carry_ref[...] = carry_ref[...] + jnp.sum(hit, axis=0, keepdims=True)
    meta = jnp.where(lane == 0.0, i0, jnp.where(lane == 1.0, i1, jnp.where(lane == 2.0, r0, r1)))
    meta_ref[...] = meta.astype(I32)
    gates_ref[...] = jnp.where(lane == 0.0, g0, g1)
    cnt_ref[...] = carry_ref[...]


def _router(x, g, shift, scale, w_router, rows_per_mod, tm):
    n, d = x.shape
    mod_idx = lambda i: ((i * tm) // rows_per_mod, 0, 0)
    row = lambda i: (i, 0)
    return pl.pallas_call(
        _router_kernel,
        grid=(n // tm,),
        in_specs=[pl.BlockSpec((tm, d), row), _const_spec((1, d)),
                  pl.BlockSpec((1, 1, d), mod_idx), pl.BlockSpec((1, 1, d), mod_idx),
                  _const_spec(w_router.shape)],
        out_specs=[pl.BlockSpec((tm, d), row), pl.BlockSpec((tm, LANES), row), pl.BlockSpec((tm, LANES), row),
                   _const_spec((1, LANES))],
        out_shape=[jax.ShapeDtypeStruct((n, d), F32), jax.ShapeDtypeStruct((n, LANES), I32),
                   jax.ShapeDtypeStruct((n, LANES), F32), jax.ShapeDtypeStruct((1, LANES), F32)],
        scratch_shapes=[pltpu.VMEM((1, LANES), F32)],
        compiler_params=_cparams(("arbitrary",)),
        name="moe_router",
    )(x, g, shift, scale, w_router)


def _expert_stream_kernel(be_ref, nb_ref, idx_in_ref, idx_out_ref, h_hbm, wg_ref, wu_ref, wd_ref, y_hbm,
                          xbuf, ybuf, sem_in, sem_out, *, sub, nb):
    j = pl.program_id(0)
    bm = xbuf.shape[1]
    c = j - 1
    live = (c >= 0) & (c < nb_ref[0])
    gather_ok = j < nb
    scatter_ok = j >= 2
    in_slot = j % 2
    cur = (j + 1) % 2

    def start_in(r):
        t = idx_in_ref[0, 0, r]
        pltpu.make_async_copy(h_hbm.at[pl.ds(t, 1), :], xbuf.at[in_slot, pl.ds(r, 1), :],
                              sem_in.at[in_slot]).start()

    def start_out(r):
        t = idx_out_ref[0, 0, r]
        pltpu.make_async_copy(ybuf.at[in_slot, pl.ds(r, 1), :], y_hbm.at[pl.ds(t, 1), :],
                              sem_out.at[in_slot]).start()

    def wait_in(slot):
        pltpu.make_async_copy(h_hbm.at[pl.ds(0, bm), :], xbuf.at[slot], sem_in.at[slot]).wait()

    def wait_out(slot):
        pltpu.make_async_copy(ybuf.at[slot], y_hbm.at[pl.ds(0, bm), :], sem_out.at[slot]).wait()

    def loop(fn):
        def body(r, carry):
            fn(r)
            return carry
        lax.fori_loop(0, bm, body, 0, unroll=8)

    def expert(interleave):
        x = xbuf[cur].astype(BF16)
        f_dim = wg_ref.shape[2]
        n_chunks = f_dim // sub
        per = -(-bm // n_chunks)
        part = jnp.zeros((bm, wd_ref.shape[2]), F32)
        for ci in range(n_chunks):
            c0 = ci * sub
            g = jnp.dot(x, wg_ref[0, :, c0:c0 + sub], preferred_element_type=F32)
            u = jnp.dot(x, wu_ref[0, :, c0:c0 + sub], preferred_element_type=F32)
            act = (_silu(g) * u).astype(BF16)
            part = part + jnp.dot(act, wd_ref[0, c0:c0 + sub, :], preferred_element_type=F32)
            if interleave:
                for r in range(ci * per, min((ci + 1) * per, bm)):
                    start_in(r)
                    start_out(r)
        return part

    def store_result(part):
        @pl.when(j >= 3)
        def _():
            wait_out(cur)
        ybuf[cur] = part

    @pl.when((c >= 0) & (c < nb))
    def _():
        wait_in(cur)

    steady = live & gather_ok & scatter_ok

    @pl.when(steady)
    def _():
        store_result(expert(True))

    @pl.when(jnp.logical_not(steady))
    def _():
        @pl.when(gather_ok)
        def _():
            loop(start_in)

        @pl.when(scatter_ok)
        def _():
            loop(start_out)

        @pl.when(live)
        def _():
            store_result(expert(False))

        @pl.when(jnp.logical_not(live) & (c >= 0) & (c < nb))
        def _():
            store_result(jnp.zeros(ybuf.shape[1:], F32))

        @pl.when(c >= nb)
        def _():
            wait_out(cur)
            wait_out(in_slot)


def _experts_stream(h, row_token, out_row, blk_e, nb_used, wg, wu, wd, n_out):
    n, d = h.shape
    bm = MOE_ROWS
    nb = row_token.shape[0] // bm
    single = pl.Buffered(1)

    def expert_of(j, be, nbu):
        return be[jnp.clip(j - 1, 0, nbu[0] - 1)]

    w_in_spec = pl.BlockSpec((1, d, wg.shape[2]), lambda j, be, nbu: (expert_of(j, be, nbu), 0, 0),
                             pipeline_mode=single)
    w_out_spec = pl.BlockSpec((1, wd.shape[1], d), lambda j, be, nbu: (expert_of(j, be, nbu), 0, 0),
                              pipeline_mode=single)
    return pl.pallas_call(
        functools.partial(_expert_stream_kernel, sub=256, nb=nb),
        grid_spec=pltpu.PrefetchScalarGridSpec(
            num_scalar_prefetch=2,
            grid=(nb + 2,),
            in_specs=[pl.BlockSpec((1, 1, bm), lambda j, be, nbu: (jnp.minimum(j, nb - 1), 0, 0),
                                   memory_space=pltpu.SMEM),
                      pl.BlockSpec((1, 1, bm), lambda j, be, nbu: (jnp.clip(j - 2, 0, nb - 1), 0, 0),
                                   memory_space=pltpu.SMEM),
                      pl.BlockSpec(memory_space=pl.ANY),
                      w_in_spec, w_in_spec, w_out_spec],
            out_specs=pl.BlockSpec(memory_space=pl.ANY),
            scratch_shapes=[pltpu.VMEM((2, bm, d), F32), pltpu.VMEM((2, bm, d), F32),
                            pltpu.SemaphoreType.DMA((2,)), pltpu.SemaphoreType.DMA((2,))]),
        out_shape=jax.ShapeDtypeStruct((n_out, d), F32),
        compiler_params=_cparams(("arbitrary",)),
        name="moe_experts",
    )(blk_e, nb_used, row_token.reshape(nb, 1, bm), out_row.reshape(nb, 1, bm), h, wg, wu, wd)


def _combine2_kernel(y0_ref, y1_ref, gates_ref, x_ref, gate_ref, gp_ref, o_ref):
    gt = gates_ref[...]
    f = y0_ref[...] * gt[:, 0:1] + y1_ref[...] * gt[:, 1:2]
    o_ref[...] = x_ref[...] + gate_ref[0] * _rms(f, gp_ref[...])


def _combine2(y, gates, x, gate, g_post, rows_per_mod, tm):
    n, d = x.shape
    nt = n // tm
    return pl.pallas_call(
        _combine2_kernel,
        grid=(nt,),
        in_specs=[pl.BlockSpec((tm, d), lambda i: (i, 0)),
                  pl.BlockSpec((tm, d), lambda i: (nt + i, 0)),
                  pl.BlockSpec((tm, LANES), lambda i: (i, 0)),
                  pl.BlockSpec((tm, d), lambda i: (i, 0)),
                  pl.BlockSpec((1, 1, d), lambda i: ((i * tm) // rows_per_mod, 0, 0)),
                  _const_spec((1, d))],
        out_specs=pl.BlockSpec((tm, d), lambda i: (i, 0)),
        out_shape=jax.ShapeDtypeStruct((n, d), F32),
        compiler_params=_cparams(("arbitrary",)),
        name="moe_combine",
    )(y, y, gates, x, gate, g_post)


def _moe(x, g_pre, shift, scale, gate, g_post, w_router, wg, wu, wd, rows_per_mod):
    n, d = x.shape
    bm = MOE_ROWS
    wr = jnp.zeros((d, LANES), F32).at[:, :N_EXPERTS].set(w_router)
    h, meta, gates, counts = _router(x, g_pre, shift, scale, wr, rows_per_mod, 512)
    cnt = counts[0, :N_EXPERTS].astype(I32)
    padded = (cnt + bm - 1) // bm * bm
    pend = jnp.cumsum(padded)
    pstart = pend - padded
    dest = pstart[meta[:, 0:2]] + meta[:, 2:4]
    cap = 2 * n + N_EXPERTS * bm
    nb = cap // bm
    code = 2 * jnp.arange(n, dtype=I32)[:, None] + jnp.arange(2, dtype=I32)[None, :]
    info = jnp.full((cap,), -1, I32).at[dest.reshape(-1)].set(code.reshape(-1))
    real = info >= 0
    row_token = jnp.where(real, info >> 1, 0)
    spill = 2 * n - 1 + jnp.cumsum(jnp.logical_not(real).astype(I32))
    out_row = jnp.where(real, (info & 1) * n + (info >> 1), spill)
    nb_used = (pend[-1] // bm).astype(I32).reshape(1)
    blk_e = jnp.minimum(jnp.searchsorted(pend, jnp.arange(nb, dtype=I32) * bm, side='right'),
                        N_EXPERTS - 1).astype(I32)
    y = _experts_stream(h, row_token, out_row, blk_e, nb_used, wg, wu, wd, cap)
    return _combine2(y, gates, x, gate, g_post, rows_per_mod, 512)


def _rope_angles(n_tokens, rot_dim):
    rows = n_tokens // GRID_W
    row = jnp.repeat(jnp.arange(rows, dtype=F32), GRID_W)
    col = jnp.tile(jnp.arange(GRID_W, dtype=F32), rows)
    n_freq = rot_dim // 4
    freq = ROPE_BASE ** (-jnp.arange(n_freq, dtype=F32) / n_freq)
    ang = jnp.concatenate([row[:, None] * freq, col[:, None] * freq], axis=-1)
    return jnp.cos(ang), jnp.sin(ang)


def _rot_cols(w):
    half = w.shape[-1] // 2
    return jnp.concatenate([-w[..., half:], w[..., :half]], axis=-1)


def kernel(x, c, ctx, c_ctx, w_mod, b_mod, g_pre_mix, g_post_mix, g_pre_ffn, g_post_ffn,
           w_in_a, mla_g_q, mla_w_uq, mla_g_kv, mla_w_ukv,
           gla_w_gate_f, gla_b_gate_f, gla_w_gate_b, gla_b_gate_b, gla_g_out, w_out_a,
           w_qkv_c, diff_lq1, diff_lk1, diff_lq2, diff_lk2, diff_g_out, w_out_c,
           ffn_w_gate, ffn_w_up, ffn_w_down,
           moe_w_router, moe_w_gate, moe_w_up, moe_w_down):
    bsz, t, d = x.shape
    tx = ctx.shape[1]
    n, nx = bsz * t, bsz * tx
    xl = x.reshape(n, d)
    xc = ctx.reshape(nx, d)

    rows = -(-(bsz + 1) // 8) * 8
    c_all = jnp.zeros((rows, d), F32).at[:bsz].set(c).at[bsz].set(c_ctx)
    mod_all = _modulation(c_all, w_mod, b_mod).reshape(2, rows, 6, d)

    def mods(i):
        lat = [mod_all[i, :bsz, k].reshape(bsz, 1, d) for k in range(6)]
        cx = [mod_all[i, bsz, k].reshape(1, 1, d) for k in range(6)]
        return lat, cx

    row2 = lambda v: v.reshape(1, -1)

    ml, mx = mods(0)
    wi = w_in_a[0]
    cq, ckv, kr, gq, gk, gv, gr, af, ab = jnp.split(
        wi, [256, 384, 448, 704, 960, 1472, 1984, 2000], axis=-1)
    w_in = jnp.concatenate([cq, ckv, kr, _rot_cols(kr), gq, gk, gv, gr, af, ab,
                            jnp.zeros((d, LANES - 2 * GLA_RANK), F32)], axis=-1).astype(BF16)
    uq = mla_w_uq[0].reshape(MLA_Q_RANK, MLA_HEADS, MLA_NOPE + MLA_ROPE)
    wq = jnp.concatenate([uq[..., :MLA_NOPE], uq[..., MLA_NOPE:], _rot_cols(uq[..., MLA_NOPE:])],
                         axis=-1).reshape(MLA_Q_RANK, MLA_HEADS * 2 * LANES).astype(BF16)
    ukv = mla_w_ukv[0].reshape(MLA_KV_RANK, MLA_HEADS, MLA_NOPE + MLA_V)
    wkv = jnp.concatenate([ukv[..., :MLA_NOPE].reshape(MLA_KV_RANK, -1),
                           ukv[..., MLA_NOPE:].reshape(MLA_KV_RANK, -1)], axis=-1).astype(BF16)
    cos_a, sin_a = _rope_angles(t, MLA_ROPE)
    zpad = jnp.zeros((t, LANES - MLA_ROPE), F32)
    c2 = jnp.concatenate([cos_a, cos_a, zpad], axis=-1)
    s2 = jnp.concatenate([sin_a, sin_a, zpad], axis=-1)
    lane = jnp.arange(LANES)
    c2x = jnp.broadcast_to(jnp.where(lane < MLA_ROPE, 1.0, 0.0).astype(F32), (tx, LANES))
    s2x = jnp.zeros((tx, LANES), F32)

    proj_l = _norm_proj(xl, row2(g_pre_mix[0]), ml[0], ml[1], w_in, t, 512, F32, "in_proj_a")
    proj_x = _norm_proj(xc, row2(g_pre_mix[0]), mx[0], mx[1], w_in, nx, 512, F32, "in_proj_a_ctx")
    gq_, gkv_ = row2(mla_g_q[0]), row2(mla_g_kv[0])
    q_l, k_l, v_l = _mla_prep(proj_l, gq_, gkv_, wq, wkv, c2, s2, 512, t // 512)
    q_x, k_x, v_x = _mla_prep(proj_x, gq_, gkv_, wq, wkv, c2x, s2x, tx, 0)
    a_l = _attention(q_l, [k_x, k_l], [v_x, v_l], bsz, MLA_HEADS, 2 * LANES, MLA_V, 512, 256, "mla_attention")
    a_x = _attention(q_x, [k_x], [v_x], bsz, MLA_HEADS, 2 * LANES, MLA_V, tx, tx, "mla_attention_ctx")

    hk = GLA_HEADS * GLA_DK
    hv = GLA_HEADS * GLA_DV
    wf = jnp.zeros((LANES, hk), F32).at[:GLA_RANK].set(gla_w_gate_f[0]).astype(BF16)
    wb = jnp.zeros((LANES, hk), F32).at[GLA_RANK:2 * GLA_RANK].set(gla_w_gate_b[0]).astype(BF16)
    lvl_f, lvl_b = _gla_level_maps()
    s0 = jnp.zeros((bsz, hv, hk), F32)
    gla_args = (wf, row2(gla_b_gate_f[0]), wb, row2(gla_b_gate_b[0]), lvl_f, lvl_b)
    ox_f, ox_b, sx_f, sx_b = _gla_scan(proj_x, *gla_args, s0, s0, bsz)
    ol_f, ol_b, _, _ = _gla_scan(proj_l, *gla_args, sx_f, sx_b, bsz)

    w_out = w_out_a[0].astype(BF16)
    g_o = row2(gla_g_out[0])
    xl = _out_proj_a(a_l, ol_f, ol_b, proj_l, g_o, xl, ml[2], row2(g_post_mix[0]), w_out, t, 512)
    xc = _out_proj_a(a_x, ox_f, ox_b, proj_x, g_o, xc, mx[2], row2(g_post_mix[0]), w_out, nx, 512)

    wg, wu, wd = ffn_w_gate[0].astype(BF16), ffn_w_up[0].astype(BF16), ffn_w_down[0].astype(BF16)
    gpf, gqf = row2(g_pre_ffn[0]), row2(g_post_ffn[0])
    xl = _ffn(xl, gpf, ml[3], ml[4], ml[5], gqf, wg, wu, wd, t, 512)
    xc = _ffn(xc, gpf, mx[3], mx[4], mx[5], gqf, wg, wu, wd, nx, 512)

    ml, mx = mods(1)
    lam_init = 0.8 - 0.6 * math.exp(-0.3 * 1)
    w_qkv = w_qkv_c[0].astype(BF16)
    width = DIFF_HEADS * 2 * DIFF_DIM
    cos_c, sin_c = _rope_angles(t, DIFF_DIM)
    z32 = jnp.zeros_like(sin_c)
    cc = jnp.concatenate([cos_c] * 4, axis=-1)
    sa = jnp.concatenate([-sin_c, z32, -sin_c, z32], axis=-1)
    sb = jnp.concatenate([z32, sin_c, z32, sin_c], axis=-1)
    q1, k1, v1 = _diff_prep(xl, row2(g_pre_mix[1]), ml[0], ml[1], w_qkv, cc, sa, sb, t, 512, t // 512)
    kv_x = _norm_proj(xc, row2(g_pre_mix[1]), mx[0], mx[1], w_qkv[:, width:], nx, 512, BF16, "diff_kv_ctx")
    lamvec = jnp.zeros((8, DIFF_DIM), F32).at[0].set(diff_lq1[0]).at[1].set(diff_lk1[0]) \
        .at[2].set(diff_lq2[0]).at[3].set(diff_lk2[0])
    a1 = _diff_attention(q1, kv_x, k1, kv_x, v1, lamvec, row2(diff_g_out[0]), bsz, 512, 256, lam_init)
    xl = _out_proj_c(a1, xl, ml[2], row2(g_post_mix[1]), w_out_c[0].astype(BF16), t, 512)

    xl = _moe(xl, row2(g_pre_ffn[1]), ml[3], ml[4], ml[5], row2(g_post_ffn[1]), moe_w_router[0],
              moe_w_gate[0].astype(BF16), moe_w_up[0].astype(BF16), moe_w_down[0].astype(BF16), t)
    return xl.reshape(bsz, t, d)
```

```python
import functools
import math

import jax
import jax.numpy as jnp
from jax import lax
from jax.experimental import pallas as pl
from jax.experimental.pallas import tpu as pltpu

F32 = jnp.float32
BF16 = jnp.bfloat16
I32 = jnp.int32

EPS = 1e-6
ROPE_BASE = 10000.0
GRID_W = 64

D_MODEL = 1024
MLA_HEADS = 4
MLA_Q_RANK = 256
MLA_KV_RANK = 128
MLA_NOPE = 128
MLA_ROPE = 64
MLA_V = 128
GLA_HEADS = 4
GLA_DK = 64
GLA_DV = 128
GLA_RANK = 16
GLA_TAU = 16.0
DIFF_HEADS = 8
DIFF_DIM = 64
N_EXPERTS = 8
LANES = 128
GLA_CHUNK = 128
GLA_LEVELS = (64, 32, 16, 8, 4, 2, 1)
MOE_ROWS = 512
ATTN_ROWS = 2048
ATTN_SUB = 256
VMEM_LIMIT = 56 * 1024 * 1024

LOG2E = math.log2(math.e)
_NT = (((1,), (1,)), ((), ()))


def _cparams(sem):
    return pltpu.CompilerParams(dimension_semantics=sem, vmem_limit_bytes=VMEM_LIMIT)


def _rms(xf, g):
    return xf * lax.rsqrt(jnp.mean(xf * xf, axis=-1, keepdims=True) + EPS) * g


def _normmod(x, g, shift, scale):
    return _rms(x.astype(F32), g) * (1.0 + scale) + shift


def _silu(x):
    return x * (1.0 / (1.0 + jnp.exp(-x)))


def _const_spec(shape):
    n = len(shape)
    return pl.BlockSpec(shape, lambda *_: (0,) * n)


def _mod_kernel(c_ref, w_ref, b_ref, o_ref):
    s = _silu(c_ref[...])
    o_ref[0] = jnp.dot(s.astype(BF16), w_ref[0].astype(BF16), preferred_element_type=F32) + b_ref[0]


def _modulation(c_all, w_mod, b_mod):
    depth, d, n6 = w_mod.shape
    rows = c_all.shape[0]
    tn = 1536
    return pl.pallas_call(
        _mod_kernel,
        grid=(depth, n6 // tn),
        in_specs=[pl.BlockSpec((rows, d), lambda i, j: (0, 0)),
                  pl.BlockSpec((1, d, tn), lambda i, j: (i, 0, j)),
                  pl.BlockSpec((1, 1, tn), lambda i, j: (i, 0, j))],
        out_specs=pl.BlockSpec((1, rows, tn), lambda i, j: (i, 0, j)),
        out_shape=jax.ShapeDtypeStruct((depth, rows, n6), F32),
        compiler_params=_cparams(("arbitrary", "arbitrary")),
        name="modulation",
    )(c_all, w_mod, b_mod.reshape(depth, 1, n6))


def _proj_kernel(x_ref, g_ref, sh_ref, sc_ref, w_ref, o_ref):
    h = _normmod(x_ref[...], g_ref[...], sh_ref[0], sc_ref[0])
    o_ref[...] = jnp.dot(h.astype(BF16), w_ref[...], preferred_element_type=F32).astype(o_ref.dtype)


def _norm_proj(x, g, shift, scale, w, rows_per_mod, tm, out_dtype, name):
    n, d = x.shape
    nout = w.shape[1]
    mod_idx = lambda i: ((i * tm) // rows_per_mod, 0, 0)
    return pl.pallas_call(
        _proj_kernel,
        grid=(n // tm,),
        in_specs=[pl.BlockSpec((tm, d), lambda i: (i, 0)),
                  _const_spec((1, d)),
                  pl.BlockSpec((1, 1, d), mod_idx),
                  pl.BlockSpec((1, 1, d), mod_idx),
                  _const_spec((d, nout))],
        out_specs=pl.BlockSpec((tm, nout), lambda i: (i, 0)),
        out_shape=jax.ShapeDtypeStruct((n, nout), out_dtype),
        compiler_params=_cparams(("arbitrary",)),
        name=name,
    )(x, g, shift, scale, w)


def _mla_prep_kernel(cq_ref, ckv_ref, kr_ref, gq_ref, gkv_ref, wq_ref, wkv_ref, c2_ref, s2_ref,
                     q_ref, k_ref, v_ref, *, scale):
    c2 = c2_ref[...]
    s2 = s2_ref[...]

    def rope2(v):
        return v * c2 + pltpu.roll(v, 64, 1) * s2

    q = jnp.dot(_rms(cq_ref[...], gq_ref[...]).astype(BF16), wq_ref[...], preferred_element_type=F32)
    kv = jnp.dot(_rms(ckv_ref[...], gkv_ref[...]).astype(BF16), wkv_ref[...], preferred_element_type=F32)
    krope = rope2(kr_ref[...]).astype(BF16)
    for h in range(MLA_HEADS):
        b = h * 2 * LANES
        q_ref[:, b:b + LANES] = (q[:, b:b + LANES] * scale).astype(BF16)
        q_ref[:, b + LANES:b + 2 * LANES] = (rope2(q[:, b + LANES:b + 2 * LANES]) * scale).astype(BF16)
        k_ref[:, b:b + LANES] = kv[:, h * LANES:(h + 1) * LANES].astype(BF16)
        k_ref[:, b + LANES:b + 2 * LANES] = krope
    v_ref[...] = kv[:, MLA_HEADS * LANES:].astype(BF16)


def _mla_prep(proj, gq, gkv, wq, wkv, c2, s2, tm, rope_blocks):
    n = proj.shape[0]
    hq = MLA_HEADS * 2 * LANES
    rope_idx = (lambda i: (i % rope_blocks, 0)) if rope_blocks else (lambda i: (0, 0))
    return pl.pallas_call(
        functools.partial(_mla_prep_kernel, scale=float((MLA_NOPE + MLA_ROPE) ** -0.5) * LOG2E),
        grid=(n // tm,),
        in_specs=[pl.BlockSpec((tm, 256), lambda i: (i, 0)),
                  pl.BlockSpec((tm, 128), lambda i: (i, 2)),
                  pl.BlockSpec((tm, 128), lambda i: (i, 3)),
                  _const_spec((1, MLA_Q_RANK)), _const_spec((1, MLA_KV_RANK)),
                  _const_spec(wq.shape), _const_spec(wkv.shape),
                  pl.BlockSpec((tm, LANES), rope_idx), pl.BlockSpec((tm, LANES), rope_idx)],
        out_specs=[pl.BlockSpec((tm, hq), lambda i: (i, 0)),
                   pl.BlockSpec((tm, hq), lambda i: (i, 0)),
                   pl.BlockSpec((tm, MLA_HEADS * MLA_V), lambda i: (i, 0))],
        out_shape=[jax.ShapeDtypeStruct((n, hq), BF16), jax.ShapeDtypeStruct((n, hq), BF16),
                   jax.ShapeDtypeStruct((n, MLA_HEADS * MLA_V), BF16)],
        compiler_params=_cparams(("arbitrary",)),
        name="mla_prep",
    )(proj, proj, proj, gq, gkv, wq, wkv, c2, s2)


def _fill_kv(k_refs, v_refs, k_scr, v_scr):
    r0 = 0
    dv = v_refs[0].shape[1]
    for k_ref, v_ref in zip(k_refs, v_refs):
        r1 = r0 + k_ref.shape[0]
        k_scr[r0:r1, :] = k_ref[...]
        v_scr[r0:r1, 0:dv] = v_ref[...]
        r0 = r1
    v_scr[:, dv:] = jnp.ones((v_scr.shape[0], v_scr.shape[1] - dv), v_scr.dtype)


def _softmax_pv(q, k, v1, dv):
    s = lax.dot_general(q, k, _NT, preferred_element_type=F32)
    p = jnp.exp2(s - jnp.max(s, axis=-1, keepdims=True)).astype(BF16)
    o = jnp.dot(p, v1, preferred_element_type=F32)
    return o[:, 0:dv] / o[:, dv:dv + 1]


def _attn_kernel(*refs, n_src, sub):
    q_ref = refs[0]
    k_refs = refs[1:1 + n_src]
    v_refs = refs[1 + n_src:1 + 2 * n_src]
    o_ref, k_scr, v_scr = refs[1 + 2 * n_src:]

    @pl.when(pl.program_id(2) == 0)
    def _():
        _fill_kv(k_refs, v_refs, k_scr, v_scr)

    k = k_scr[...]
    v1 = v_scr[...]
    dv = o_ref.shape[1]
    for r0 in range(0, q_ref.shape[0], sub):
        o_ref[r0:r0 + sub, :] = _softmax_pv(q_ref[r0:r0 + sub, :], k, v1, dv).astype(o_ref.dtype)


def _attention(q, ks, vs, batch, heads, dq, dv, tq, sub, name):
    nq = q.shape[0] // batch // tq
    n_src = len(ks)
    tk = sum(k.shape[0] for k in ks) // batch
    in_specs = [pl.BlockSpec((tq, dq), lambda b, h, i: (b * nq + i, h))]
    for k in ks:
        in_specs.append(pl.BlockSpec((k.shape[0] // batch, dq), lambda b, h, i: (b, h)))
    for v in vs:
        in_specs.append(pl.BlockSpec((v.shape[0] // batch, dv), lambda b, h, i: (b, h)))
    return pl.pallas_call(
        functools.partial(_attn_kernel, n_src=n_src, sub=sub),
        grid=(batch, heads, nq),
        in_specs=in_specs,
        out_specs=pl.BlockSpec((tq, dv), lambda b, h, i: (b * nq + i, h)),
        out_shape=jax.ShapeDtypeStruct((q.shape[0], heads * dv), BF16),
        scratch_shapes=[pltpu.VMEM((tk, dq), BF16), pltpu.VMEM((tk, 2 * dv), BF16)],
        compiler_params=_cparams(("arbitrary", "arbitrary", "arbitrary")),
        name=name,
    )(q, *ks, *vs)


def _log_sigmoid(z):
    return jnp.minimum(z, 0.0) - jnp.log(1.0 + jnp.exp(-jnp.abs(z)))


def _gla_anchor(cum, level, rev):
    c, hk = cum.shape
    two = 2 * level
    a = level if rev else level - 1
    if two >= 8:
        return jnp.concatenate(
            [jnp.broadcast_to(cum[b * two + a:b * two + a + 1], (two, hk)) for b in range(c // two)], axis=0)
    pos = lax.broadcasted_iota(I32, cum.shape, 0) & (two - 1)
    anc = cum
    for p in range(two):
        if p != a:
            anc = jnp.where(pos == p, pltpu.roll(cum, (p - a) % c, 0), anc)
    return anc


def _gla_chunk(q, k, v, la, s_ref, lvl, rev):
    c = GLA_CHUNK
    hk = GLA_HEADS * GLA_DK
    ri = lax.broadcasted_iota(I32, (c, c), 0)
    ci = lax.broadcasted_iota(I32, (c, c), 1)
    tri = jnp.where((ci >= ri) if rev else (ci <= ri), 1.0, 0.0).astype(BF16)
    la_hi = la.astype(BF16)
    rest = la - la_hi.astype(F32)
    la_mid = rest.astype(BF16)
    la_lo = (rest - la_mid.astype(F32)).astype(BF16)
    cum = (jnp.dot(tri, la_hi, preferred_element_type=F32) + jnp.dot(tri, la_mid, preferred_element_type=F32)
           + jnp.dot(tri, la_lo, preferred_element_type=F32))
    tot = cum[0:1] if rev else cum[c - 1:c]
    s_old = s_ref[...]
    qs = q * (GLA_DK ** -0.5)
    qh = (qs * jnp.exp(cum)).astype(BF16)
    o_inter = lax.dot_general(qh, s_old.astype(BF16), _NT, preferred_element_type=F32)
    kh = (k * jnp.exp(tot - cum)).astype(BF16)
    vb = v.astype(BF16)
    u = jnp.dot(v.T.astype(BF16), kh, preferred_element_type=F32)
    bd = (lax.broadcasted_iota(I32, u.shape, 0) // GLA_DV) == (lax.broadcasted_iota(I32, u.shape, 1) // GLA_DK)
    s_ref[...] = s_old * jnp.exp(tot) + jnp.where(bd, u, 0.0)

    row = lax.broadcasted_iota(I32, (c, hk), 0)
    lane_head = lax.broadcasted_iota(I32, (c, hk), 1) // GLA_DK

    def stack_heads(zb):
        zero = jnp.zeros_like(zb)
        return jnp.concatenate([jnp.where(lane_head == h, zb, zero) for h in range(GLA_HEADS)], axis=0)

    ones = jnp.ones((c, hk), BF16)
    diag = lax.dot_general(stack_heads((qs * k).astype(BF16)), ones, _NT, preferred_element_type=F32)
    att = jnp.where(lvl == 0, diag, 0.0)
    for level in GLA_LEVELS:
        upper = (row & level) != 0
        is_q = jnp.logical_not(upper) if rev else upper
        x = cum - _gla_anchor(cum, level, rev)
        zb = (jnp.where(is_q, qs, k) * jnp.exp(jnp.where(is_q, x, -x))).astype(BF16)
        gram = lax.dot_general(stack_heads(zb), zb, _NT, preferred_element_type=F32)
        att = jnp.where(lvl == level, gram, att)
    attb = att.astype(BF16)
    o_intra = jnp.concatenate(
        [jnp.dot(attb[h * c:(h + 1) * c], vb[:, h * GLA_DV:(h + 1) * GLA_DV], preferred_element_type=F32)
         for h in range(GLA_HEADS)], axis=1)
    return o_inter + o_intra


def _gla_kernel(qf_ref, kf_ref, vf_ref, gf_ref, qb_ref, kb_ref, vb_ref, gb_ref,
                wf_ref, bf_ref, wb_ref, bb_ref, lvlf_ref, lvlb_ref, s0f_ref, s0b_ref,
                of_ref, ob_ref, sf_ref, sb_ref, stf, stb):
    j = pl.program_id(1)

    @pl.when(j == 0)
    def _():
        stf[...] = s0f_ref[0]
        stb[...] = s0b_ref[0]

    def log_decay(g_ref, w_ref, b_ref):
        z = jnp.dot(g_ref[...].astype(BF16), w_ref[...], preferred_element_type=F32) + b_ref[...]
        return _log_sigmoid(z) * (1.0 / GLA_TAU)

    of_ref[...] = _gla_chunk(qf_ref[...], kf_ref[...], vf_ref[...], log_decay(gf_ref, wf_ref, bf_ref),
                             stf, lvlf_ref[...], False)
    ob_ref[...] = _gla_chunk(qb_ref[...], kb_ref[...], vb_ref[...], log_decay(gb_ref, wb_ref, bb_ref),
                             stb, lvlb_ref[...], True)

    @pl.when(j == pl.num_programs(1) - 1)
    def _():
        sf_ref[0] = stf[...]
        sb_ref[0] = stb[...]


def _gla_level_maps():
    c = GLA_CHUNK
    t = jnp.arange(c, dtype=I32)[:, None]
    s = jnp.arange(c, dtype=I32)[None, :]
    diff = t ^ s
    top = jnp.zeros((c, c), I32)
    for level in GLA_LEVELS:
        top = jnp.where((top == 0) & ((diff & level) != 0), level, top)
    fwd = jnp.where(t == s, 0, jnp.where(t > s, top, -1))
    bwd = jnp.where(t == s, 0, jnp.where(t < s, top, -1))
    return jnp.tile(fwd, (GLA_HEADS, 1)), jnp.tile(bwd, (GLA_HEADS, 1))


def _gla_scan(proj, wf, bf, wb, bb, lvl_f, lvl_b, s0f, s0b, batch):
    n = proj.shape[0]
    c = GLA_CHUNK
    nch = n // batch // c
    hk = GLA_HEADS * GLA_DK
    hv = GLA_HEADS * GLA_DV
    fwd = lambda b, j: b * nch + j
    bwd = lambda b, j: b * nch + (nch - 1 - j)

    def specs(row):
        return [pl.BlockSpec((c, hk), lambda b, j: (row(b, j), 2)),
                pl.BlockSpec((c, hk), lambda b, j: (row(b, j), 3)),
                pl.BlockSpec((c, hv), lambda b, j: (row(b, j), 2)),
                pl.BlockSpec((c, LANES), lambda b, j: (row(b, j), 16))]

    st_spec = pl.BlockSpec((1, hv, hk), lambda b, j: (b, 0, 0))
    return pl.pallas_call(
        _gla_kernel,
        grid=(batch, nch),
        in_specs=specs(fwd) + specs(bwd) + [
            _const_spec((LANES, hk)), _const_spec((1, hk)), _const_spec((LANES, hk)), _const_spec((1, hk)),
            _const_spec(lvl_f.shape), _const_spec(lvl_b.shape), st_spec, st_spec],
        out_specs=[pl.BlockSpec((c, hv), lambda b, j: (fwd(b, j), 0)),
                   pl.BlockSpec((c, hv), lambda b, j: (bwd(b, j), 0)),
                   st_spec, st_spec],
        out_shape=[jax.ShapeDtypeStruct((n, hv), F32), jax.ShapeDtypeStruct((n, hv), F32),
                   jax.ShapeDtypeStruct((batch, hv, hk), F32), jax.ShapeDtypeStruct((batch, hv, hk), F32)],
        scratch_shapes=[pltpu.VMEM((hv, hk), F32), pltpu.VMEM((hv, hk), F32)],
        compiler_params=_cparams(("arbitrary", "arbitrary")),
        name="gla_scan",
    )(proj, proj, proj, proj, proj, proj, proj, proj, wf, bf, wb, bb, lvl_f, lvl_b, s0f, s0b)


def _swiglu_residual(x, h, gate, g_post, wg_ref, wu_ref, wd_ref, fc):
    acc = jnp.zeros(x.shape, F32)
    for c0 in range(0, wg_ref.shape[1], fc):
        g = jnp.dot(h, wg_ref[:, c0:c0 + fc], preferred_element_type=F32)
        u = jnp.dot(h, wu_ref[:, c0:c0 + fc], preferred_element_type=F32)
        act = (_silu(g) * u).astype(BF16)
        acc = acc + jnp.dot(act, wd_ref[c0:c0 + fc, :], preferred_element_type=F32)
    return x + gate * _rms(acc, g_post)


def _mix_ffn_kernel(a_ref, of_ref, ob_ref, r_ref, go_ref, x_ref, mod_ref, gpm_ref, wo_ref,
                    gpre_ref, gpost_ref, wg_ref, wu_ref, wd_ref, o_ref, *, fc):
    o = of_ref[...] + ob_ref[...]
    r = r_ref[...]
    go = go_ref[...]
    parts = []
    for h in range(GLA_HEADS):
        sl = slice(h * GLA_DV, (h + 1) * GLA_DV)
        parts.append(_rms(o[:, sl], go) * _silu(r[:, sl]))
    g = jnp.concatenate(parts, axis=-1).astype(BF16)
    na = a_ref.shape[1]
    y = (jnp.dot(a_ref[...], wo_ref[0:na, :], preferred_element_type=F32)
         + jnp.dot(g, wo_ref[na:, :], preferred_element_type=F32))
    mod = mod_ref[0]
    x = x_ref[...] + mod[0:1] * _rms(y, gpm_ref[...])
    h = _normmod(x, gpre_ref[...], mod[1:2], mod[2:3]).astype(BF16)
    o_ref[...] = _swiglu_residual(x, h, mod[3:4], gpost_ref[...], wg_ref, wu_ref, wd_ref, fc)


def _mix_ffn(a, o_f, o_b, proj, g_o, x, mod, g_post_mix, w_out, g_pre, g_post, wg, wu, wd, rows_per_mod, tm):
    n, d = x.shape
    hv = GLA_HEADS * GLA_DV
    row = lambda i: (i, 0)
    single = pl.Buffered(1)
    resident = lambda w: pl.BlockSpec(w.shape, lambda i: (0, 0), pipeline_mode=single)
    return pl.pallas_call(
        functools.partial(_mix_ffn_kernel, fc=256),
        grid=(n // tm,),
        in_specs=[pl.BlockSpec((tm, a.shape[1]), row),
                  pl.BlockSpec((tm, hv), row),
                  pl.BlockSpec((tm, hv), row),
                  pl.BlockSpec((tm, hv), lambda i: (i, 3)),
                  _const_spec((1, GLA_DV)),
                  pl.BlockSpec((tm, d), row),
                  pl.BlockSpec((1, mod.shape[1], d), lambda i: ((i * tm) // rows_per_mod, 0, 0)),
                  _const_spec((1, d)), resident(w_out),
                  _const_spec((1, d)), _const_spec((1, d)), resident(wg), resident(wu), resident(wd)],
        out_specs=pl.BlockSpec((tm, d), row),
        out_shape=jax.ShapeDtypeStruct((n, d), F32),
        compiler_params=_cparams(("arbitrary",)),
        name="mix_ffn",
    )(a, o_f, o_b, proj, g_o, x, mod, g_post_mix, w_out, g_pre, g_post, wg, wu, wd)


def _diff_prep_kernel(x_ref, g_ref, sh_ref, sc_ref, w_ref, c_ref, sa_ref, sb_ref, q_ref, k_ref, v_ref):
    h = _normmod(x_ref[...], g_ref[...], sh_ref[0], sc_ref[0]).astype(BF16)
    qkv = jnp.dot(h, w_ref[...], preferred_element_type=F32)
    cc, sa, sb = c_ref[...], sa_ref[...], sb_ref[...]
    width = q_ref.shape[1]

    def rope(v):
        return v * cc + pltpu.roll(v, 96, 1) * sa + pltpu.roll(v, 32, 1) * sb

    for j in range(width // LANES):
        sl = slice(j * LANES, (j + 1) * LANES)
        q_ref[:, sl] = (rope(qkv[:, sl]) * (DIFF_DIM ** -0.5 * LOG2E)).astype(BF16)
        k_ref[:, sl] = rope(qkv[:, width + j * LANES:width + (j + 1) * LANES]).astype(BF16)
    v_ref[...] = qkv[:, 2 * width:].astype(BF16)


def _diff_prep(x, g, shift, scale, w, cc, sa, sb, rows_per_mod, tm, rope_blocks):
    n, d = x.shape
    width = w.shape[1] // 3
    mod_idx = lambda i: ((i * tm) // rows_per_mod, 0, 0)
    rope_spec = pl.BlockSpec((tm, LANES), lambda i: (i % rope_blocks, 0))
    out_spec = pl.BlockSpec((tm, width), lambda i: (i, 0))
    return pl.pallas_call(
        _diff_prep_kernel,
        grid=(n // tm,),
        in_specs=[pl.BlockSpec((tm, d), lambda i: (i, 0)), _const_spec((1, d)),
                  pl.BlockSpec((1, 1, d), mod_idx), pl.BlockSpec((1, 1, d), mod_idx),
                  _const_spec(w.shape), rope_spec, rope_spec, rope_spec],
        out_specs=[out_spec, out_spec, out_spec],
        out_shape=[jax.ShapeDtypeStruct((n, width), BF16)] * 3,
        compiler_params=_cparams(("arbitrary",)),
        name="diff_prep",
    )(x, g, shift, scale, w, cc, sa, sb)


def _diff_attn_kernel(q_ref, kx_ref, kl_ref, vx_ref, vl_ref, lam_ref, go_ref, o_ref, k_scr, v_scr, *,
                      lam_init, sub):
    @pl.when(pl.program_id(2) == 0)
    def _():
        _fill_kv((kx_ref, kl_ref), (vx_ref, vl_ref), k_scr, v_scr)

    lv = lam_ref[...]
    lam = (jnp.exp(jnp.sum(lv[0:1] * lv[1:2], axis=-1, keepdims=True))
           - jnp.exp(jnp.sum(lv[2:3] * lv[3:4], axis=-1, keepdims=True)) + lam_init)
    k = k_scr[...]
    v1 = v_scr[...]
    go = go_ref[...]
    lane = lax.broadcasted_iota(I32, (sub, LANES), 1)
    for r0 in range(0, q_ref.shape[0], sub):
        q = q_ref[r0:r0 + sub, :]
        zero = jnp.zeros_like(q)
        o = (_softmax_pv(jnp.where(lane < DIFF_DIM, q, zero), k, v1, LANES)
             - lam * _softmax_pv(jnp.where(lane >= DIFF_DIM, q, zero), k, v1, LANES))
        o_ref[r0:r0 + sub, :] = (_rms(o, go) * (1.0 - lam_init)).astype(o_ref.dtype)


def _diff_attention(q, kx, kl, vx, vl, lamvec, g_o, batch, tq, sub, lam_init):
    n = q.shape[0]
    nq = n // batch // tq
    tx = kx.shape[0] // batch
    tl = kl.shape[0] // batch
    return pl.pallas_call(
        functools.partial(_diff_attn_kernel, lam_init=lam_init, sub=sub),
        grid=(batch, DIFF_HEADS, nq),
        in_specs=[pl.BlockSpec((tq, LANES), lambda b, h, i: (b * nq + i, h)),
                  pl.BlockSpec((tx, LANES), lambda b, h, i: (b, h)),
                  pl.BlockSpec((tl, LANES), lambda b, h, i: (b, h)),
                  pl.BlockSpec((tx, LANES), lambda b, h, i: (b, DIFF_HEADS + h)),
                  pl.BlockSpec((tl, LANES), lambda b, h, i: (b, h)),
                  _const_spec(lamvec.shape), _const_spec((1, LANES))],
        out_specs=pl.BlockSpec((tq, LANES), lambda b, h, i: (b * nq + i, h)),
        out_shape=jax.ShapeDtypeStruct((n, DIFF_HEADS * LANES), BF16),
        scratch_shapes=[pltpu.VMEM((tx + tl, LANES), BF16), pltpu.VMEM((tx + tl, 2 * LANES), BF16)],
        compiler_params=_cparams(("arbitrary", "arbitrary", "arbitrary")),
        name="diff_attention",
    )(q, kx, kl, vx, vl, lamvec, g_o)


def _out_c_kernel(a_ref, x_ref, gate_ref, gp_ref, w_ref, o_ref):
    y = jnp.dot(a_ref[...], w_ref[...], preferred_element_type=F32)
    o_ref[...] = x_ref[...] + gate_ref[0] * _rms(y, gp_ref[...])


def _out_proj_c(a, x, gate, g_post, w, rows_per_mod, tm):
    n, d = x.shape
    return pl.pallas_call(
        _out_c_kernel,
        grid=(n // tm,),
        in_specs=[pl.BlockSpec((tm, a.shape[1]), lambda i: (i, 0)),
                  pl.BlockSpec((tm, d), lambda i: (i, 0)),
                  pl.BlockSpec((1, 1, d), lambda i: ((i * tm) // rows_per_mod, 0, 0)),
                  _const_spec((1, d)),
                  _const_spec(w.shape)],
        out_specs=pl.BlockSpec((tm, d), lambda i: (i, 0)),
        out_shape=jax.ShapeDtypeStruct((n, d), F32),
        compiler_params=_cparams(("arbitrary",)),
        name="out_proj_c",
    )(a, x, gate, g_post, w)


def _router_kernel(x_ref, g_ref, sh_ref, sc_ref, wr_ref, h_ref, meta_ref, gates_ref, cnt_ref, carry_ref):
    i = pl.program_id(0)

    @pl.when(i == 0)
    def _():
        carry_ref[...] = jnp.zeros_like(carry_ref)

    h = _normmod(x_ref[...], g_ref[...], sh_ref[0], sc_ref[0])
    h_ref[...] = h
    tm = h.shape[0]
    logits = jnp.dot(h, wr_ref[...], precision=lax.Precision.HIGHEST, preferred_element_type=F32)
    lane = lax.broadcasted_iota(I32, logits.shape, 1).astype(F32)
    neg = jnp.float32(-jnp.inf)
    logits = jnp.where(lane < N_EXPERTS, logits, neg)
    m0 = jnp.max(logits, axis=-1, keepdims=True)
    i0 = jnp.min(jnp.where(logits == m0, lane, float(LANES)), axis=-1, keepdims=True)
    rest = jnp.where(lane == i0, neg, logits)
    m1 = jnp.max(rest, axis=-1, keepdims=True)
    i1 = jnp.min(jnp.where(rest == m1, lane, float(LANES)), axis=-1, keepdims=True)
    e = jnp.exp(m1 - m0)
    g0 = 1.0 / (1.0 + e)
    g1 = e / (1.0 + e)
    hit = jnp.where(lane == i0, 1.0, jnp.where(lane == i1, 1.0, 0.0)).astype(F32)
    ri = lax.broadcasted_iota(I32, (tm, tm), 0)
    ci = lax.broadcasted_iota(I32, (tm, tm), 1)
    below = jnp.where(ci < ri, 1.0, 0.0).astype(BF16)
    prefix = jnp.dot(below, hit.astype(BF16), preferred_element_type=F32) + carry_ref[...]
    r0 = jnp.sum(jnp.where(lane == i0, prefix, 0.0), axis=-1, keepdims=True)
    r1 = jnp.sum(jnp.where(lane == i1, prefix, 0.0), axis=-1, keepdims=True)
    carry_ref[...] = carry_ref[...] + jnp.sum(hit, axis=0, keepdims=True)
    meta = jnp.where(lane == 0.0, i0, jnp.where(lane == 1.0, i1, jnp.where(lane == 2.0, r0, r1)))
    meta_ref[...] = meta.astype(I32)
    gates_ref[...] = jnp.where(lane == 0.0, g0, g1)
    cnt_ref[...] = carry_ref[...]


def _router(x, g, shift, scale, w_router, rows_per_mod, tm):
    n, d = x.shape
    mod_idx = lambda i: ((i * tm) // rows_per_mod, 0, 0)
    row = lambda i: (i, 0)
    return pl.pallas_call(
        _router_kernel,
        grid=(n // tm,),
        in_specs=[pl.BlockSpec((tm, d), row), _const_spec((1, d)),
                  pl.BlockSpec((1, 1, d), mod_idx), pl.BlockSpec((1, 1, d), mod_idx),
                  _const_spec(w_router.shape)],
        out_specs=[pl.BlockSpec((tm, d), row), pl.BlockSpec((tm, LANES), row), pl.BlockSpec((tm, LANES), row),
                   _const_spec((1, LANES))],
        out_shape=[jax.ShapeDtypeStruct((n, d), F32), jax.ShapeDtypeStruct((n, LANES), I32),
                   jax.ShapeDtypeStruct((n, LANES), F32), jax.ShapeDtypeStruct((1, LANES), F32)],
        scratch_shapes=[pltpu.VMEM((1, LANES), F32)],
        compiler_params=_cparams(("arbitrary",)),
        name="moe_router",
    )(x, g, shift, scale, w_router)


def _expert_stream_kernel(be_ref, nb_ref, idx_in_ref, idx_out_ref, h_hbm, wg_ref, wu_ref, wd_ref, y_hbm,
                          xbuf, ybuf, sem_in, sem_out, *, sub, nb):
    j = pl.program_id(0)
    bm = xbuf.shape[1]
    c = j - 1
    live = (c >= 0) & (c < nb_ref[0])
    gather_ok = j < nb
    scatter_ok = j >= 2
    in_slot = j % 2
    cur = (j + 1) % 2

    def start_in(r):
        t = idx_in_ref[0, 0, r]
        pltpu.make_async_copy(h_hbm.at[pl.ds(t, 1), :], xbuf.at[in_slot, pl.ds(r, 1), :],
                              sem_in.at[in_slot]).start()

    def start_out(r):
        t = idx_out_ref[0, 0, r]
        pltpu.make_async_copy(ybuf.at[in_slot, pl.ds(r, 1), :], y_hbm.at[pl.ds(t, 1), :],
                              sem_out.at[in_slot]).start()

    def wait_in(slot):
        pltpu.make_async_copy(h_hbm.at[pl.ds(0, bm), :], xbuf.at[slot], sem_in.at[slot]).wait()

    def wait_out(slot):
        pltpu.make_async_copy(ybuf.at[slot], y_hbm.at[pl.ds(0, bm), :], sem_out.at[slot]).wait()

    def loop(fn):
        def body(r, carry):
            fn(r)
            return carry
        lax.fori_loop(0, bm, body, 0, unroll=8)

    def expert(interleave):
        x = xbuf[cur].astype(BF16)
        f_dim = wg_ref.shape[2]
        n_chunks = f_dim // sub
        per = -(-bm // n_chunks)
        part = jnp.zeros((bm, wd_ref.shape[2]), F32)
        for ci in range(n_chunks):
            c0 = ci * sub
            g = jnp.dot(x, wg_ref[0, :, c0:c0 + sub], preferred_element_type=F32)
            u = jnp.dot(x, wu_ref[0, :, c0:c0 + sub], preferred_element_type=F32)
            act = (_silu(g) * u).astype(BF16)
            part = part + jnp.dot(act, wd_ref[0, c0:c0 + sub, :], preferred_element_type=F32)
            if interleave:
                for r in range(ci * per, min((ci + 1) * per, bm)):
                    start_in(r)
                    start_out(r)
        return part

    def store_result(part):
        @pl.when(j >= 3)
        def _():
            wait_out(cur)
        ybuf[cur] = part

    @pl.when((c >= 0) & (c < nb))
    def _():
        wait_in(cur)

    steady = live & gather_ok & scatter_ok

    @pl.when(steady)
    def _():
        store_result(expert(True))

    @pl.when(jnp.logical_not(steady))
    def _():
        @pl.when(gather_ok)
        def _():
            loop(start_in)

        @pl.when(scatter_ok)
        def _():
            loop(start_out)

        @pl.when(live)
        def _():
            store_result(expert(False))

        @pl.when(jnp.logical_not(live) & (c >= 0) & (c < nb))
        def _():
            store_result(jnp.zeros(ybuf.shape[1:], F32))

        @pl.when(c >= nb)
        def _():
            wait_out(cur)
            wait_out(in_slot)


def _experts_stream(h, row_token, out_row, blk_e, nb_used, wg, wu, wd, n_out):
    n, d = h.shape
    bm = MOE_ROWS
    nb = row_token.shape[0] // bm
    single = pl.Buffered(1)

    def expert_of(j, be, nbu):
        return be[jnp.clip(j - 1, 0, nbu[0] - 1)]

    w_in_spec = pl.BlockSpec((1, d, wg.shape[2]), lambda j, be, nbu: (expert_of(j, be, nbu), 0, 0),
                             pipeline_mode=single)
    w_out_spec = pl.BlockSpec((1, wd.shape[1], d), lambda j, be, nbu: (expert_of(j, be, nbu), 0, 0),
                              pipeline_mode=single)
    return pl.pallas_call(
        functools.partial(_expert_stream_kernel, sub=256, nb=nb),
        grid_spec=pltpu.PrefetchScalarGridSpec(
            num_scalar_prefetch=2,
            grid=(nb + 2,),
            in_specs=[pl.BlockSpec((1, 1, bm), lambda j, be, nbu: (jnp.minimum(j, nb - 1), 0, 0),
                                   memory_space=pltpu.SMEM),
                      pl.BlockSpec((1, 1, bm), lambda j, be, nbu: (jnp.clip(j - 2, 0, nb - 1), 0, 0),
                                   memory_space=pltpu.SMEM),
                      pl.BlockSpec(memory_space=pl.ANY),
                      w_in_spec, w_in_spec, w_out_spec],
            out_specs=pl.BlockSpec(memory_space=pl.ANY),
            scratch_shapes=[pltpu.VMEM((2, bm, d), F32), pltpu.VMEM((2, bm, d), F32),
                            pltpu.SemaphoreType.DMA((2,)), pltpu.SemaphoreType.DMA((2,))]),
        out_shape=jax.ShapeDtypeStruct((n_out, d), F32),
        compiler_params=_cparams(("arbitrary",)),
        name="moe_experts",
    )(blk_e, nb_used, row_token.reshape(nb, 1, bm), out_row.reshape(nb, 1, bm), h, wg, wu, wd)


def _combine2_kernel(y0_ref, y1_ref, gates_ref, x_ref, gate_ref, gp_ref, o_ref):
    gt = gates_ref[...]
    f = y0_ref[...] * gt[:, 0:1] + y1_ref[...] * gt[:, 1:2]
    o_ref[...] = x_ref[...] + gate_ref[0] * _rms(f, gp_ref[...])


def _combine2(y, gates, x, gate, g_post, rows_per_mod, tm):
    n, d = x.shape
    nt = n // tm
    return pl.pallas_call(
        _combine2_kernel,
        grid=(nt,),
        in_specs=[pl.BlockSpec((tm, d), lambda i: (i, 0)),
                  pl.BlockSpec((tm, d), lambda i: (nt + i, 0)),
                  pl.BlockSpec((tm, LANES), lambda i: (i, 0)),
                  pl.BlockSpec((tm, d), lambda i: (i, 0)),
                  pl.BlockSpec((1, 1, d), lambda i: ((i * tm) // rows_per_mod, 0, 0)),
                  _const_spec((1, d))],
        out_specs=pl.BlockSpec((tm, d), lambda i: (i, 0)),
        out_shape=jax.ShapeDtypeStruct((n, d), F32),
        compiler_params=_cparams(("arbitrary",)),
        name="moe_combine",
    )(y, y, gates, x, gate, g_post)


def _moe(x, g_pre, shift, scale, gate, g_post, w_router, wg, wu, wd, rows_per_mod):
    n, d = x.shape
    bm = MOE_ROWS
    wr = jnp.zeros((d, LANES), F32).at[:, :N_EXPERTS].set(w_router)
    h, meta, gates, counts = _router(x, g_pre, shift, scale, wr, rows_per_mod, 512)
    cnt = counts[0, :N_EXPERTS].astype(I32)
    padded = (cnt + bm - 1) // bm * bm
    pend = jnp.cumsum(padded)
    pstart = pend - padded
    dest = pstart[meta[:, 0:2]] + meta[:, 2:4]
    cap = 2 * n + N_EXPERTS * bm
    nb = cap // bm
    code = 2 * jnp.arange(n, dtype=I32)[:, None] + jnp.arange(2, dtype=I32)[None, :]
    info = jnp.full((cap,), -1, I32).at[dest.reshape(-1)].set(code.reshape(-1))
    real = info >= 0
    row_token = jnp.where(real, info >> 1, 0)
    spill = 2 * n - 1 + jnp.cumsum(jnp.logical_not(real).astype(I32))
    out_row = jnp.where(real, (info & 1) * n + (info >> 1), spill)
    nb_used = (pend[-1] // bm).astype(I32).reshape(1)
    blk_e = jnp.minimum(jnp.searchsorted(pend, jnp.arange(nb, dtype=I32) * bm, side='right'),
                        N_EXPERTS - 1).astype(I32)
    y = _experts_stream(h, row_token, out_row, blk_e, nb_used, wg, wu, wd, cap)
    return _combine2(y, gates, x, gate, g_post, rows_per_mod, 512)


def _rope_angles(n_tokens, rot_dim):
    rows = n_tokens // GRID_W
    row = jnp.repeat(jnp.arange(rows, dtype=F32), GRID_W)
    col = jnp.tile(jnp.arange(GRID_W, dtype=F32), rows)
    n_freq = rot_dim // 4
    freq = ROPE_BASE ** (-jnp.arange(n_freq, dtype=F32) / n_freq)
    ang = jnp.concatenate([row[:, None] * freq, col[:, None] * freq], axis=-1)
    return jnp.cos(ang), jnp.sin(ang)


def _rot_cols(w):
    half = w.shape[-1] // 2
    return jnp.concatenate([-w[..., half:], w[..., :half]], axis=-1)


def kernel(x, c, ctx, c_ctx, w_mod, b_mod, g_pre_mix, g_post_mix, g_pre_ffn, g_post_ffn,
           w_in_a, mla_g_q, mla_w_uq, mla_g_kv, mla_w_ukv,
           gla_w_gate_f, gla_b_gate_f, gla_w_gate_b, gla_b_gate_b, gla_g_out, w_out_a,
           w_qkv_c, diff_lq1, diff_lk1, diff_lq2, diff_lk2, diff_g_out, w_out_c,
           ffn_w_gate, ffn_w_up, ffn_w_down,
           moe_w_router, moe_w_gate, moe_w_up, moe_w_down):
    bsz, t, d = x.shape
    tx = ctx.shape[1]
    n, nx = bsz * t, bsz * tx
    xl = x.reshape(n, d)
    xc = ctx.reshape(nx, d)

    rows = -(-(bsz + 1) // 8) * 8
    c_all = jnp.zeros((rows, d), F32).at[:bsz].set(c).at[bsz].set(c_ctx)
    mod_all = _modulation(c_all, w_mod, b_mod).reshape(2, rows, 6, d)

    def mods(i):
        lat = [mod_all[i, :bsz, k].reshape(bsz, 1, d) for k in range(6)]
        cx = [mod_all[i, bsz, k].reshape(1, 1, d) for k in range(6)]
        return lat, cx

    row2 = lambda v: v.reshape(1, -1)

    ml, mx = mods(0)
    wi = w_in_a[0]
    cq, ckv, kr, gq, gk, gv, gr, af, ab = jnp.split(
        wi, [256, 384, 448, 704, 960, 1472, 1984, 2000], axis=-1)
    w_in = jnp.concatenate([cq, ckv, kr, _rot_cols(kr), gq, gk, gv, gr, af, ab,
                            jnp.zeros((d, LANES - 2 * GLA_RANK), F32)], axis=-1).astype(BF16)
    uq = mla_w_uq[0].reshape(MLA_Q_RANK, MLA_HEADS, MLA_NOPE + MLA_ROPE)
    wq = jnp.concatenate([uq[..., :MLA_NOPE], uq[..., MLA_NOPE:], _rot_cols(uq[..., MLA_NOPE:])],
                         axis=-1).reshape(MLA_Q_RANK, MLA_HEADS * 2 * LANES).astype(BF16)
    ukv = mla_w_ukv[0].reshape(MLA_KV_RANK, MLA_HEADS, MLA_NOPE + MLA_V)
    wkv = jnp.concatenate([ukv[..., :MLA_NOPE].reshape(MLA_KV_RANK, -1),
                           ukv[..., MLA_NOPE:].reshape(MLA_KV_RANK, -1)], axis=-1).astype(BF16)
    cos_a, sin_a = _rope_angles(t, MLA_ROPE)
    zpad = jnp.zeros((t, LANES - MLA_ROPE), F32)
    c2 = jnp.concatenate([cos_a, cos_a, zpad], axis=-1)
    s2 = jnp.concatenate([sin_a, sin_a, zpad], axis=-1)
    lane = jnp.arange(LANES)
    c2x = jnp.broadcast_to(jnp.where(lane < MLA_ROPE, 1.0, 0.0).astype(F32), (tx, LANES))
    s2x = jnp.zeros((tx, LANES), F32)

    proj_l = _norm_proj(xl, row2(g_pre_mix[0]), ml[0], ml[1], w_in, t, 512, F32, "in_proj_a")
    proj_x = _norm_proj(xc, row2(g_pre_mix[0]), mx[0], mx[1], w_in, nx, 512, F32, "in_proj_a_ctx")
    gq_, gkv_ = row2(mla_g_q[0]), row2(mla_g_kv[0])
    q_l, k_l, v_l = _mla_prep(proj_l, gq_, gkv_, wq, wkv, c2, s2, 512, t // 512)
    q_x, k_x, v_x = _mla_prep(proj_x, gq_, gkv_, wq, wkv, c2x, s2x, tx, 0)
    a_l = _attention(q_l, [k_x, k_l], [v_x, v_l], bsz, MLA_HEADS, 2 * LANES, MLA_V, min(t, ATTN_ROWS), ATTN_SUB, "mla_attention")
    a_x = _attention(q_x, [k_x], [v_x], bsz, MLA_HEADS, 2 * LANES, MLA_V, tx, tx, "mla_attention_ctx")

    hk = GLA_HEADS * GLA_DK
    hv = GLA_HEADS * GLA_DV
    wf = jnp.zeros((LANES, hk), F32).at[:GLA_RANK].set(gla_w_gate_f[0]).astype(BF16)
    wb = jnp.zeros((LANES, hk), F32).at[GLA_RANK:2 * GLA_RANK].set(gla_w_gate_b[0]).astype(BF16)
    lvl_f, lvl_b = _gla_level_maps()
    s0 = jnp.zeros((bsz, hv, hk), F32)
    gla_args = (wf, row2(gla_b_gate_f[0]), wb, row2(gla_b_gate_b[0]), lvl_f, lvl_b)
    ox_f, ox_b, sx_f, sx_b = _gla_scan(proj_x, *gla_args, s0, s0, bsz)
    ol_f, ol_b, _, _ = _gla_scan(proj_l, *gla_args, sx_f, sx_b, bsz)

    w_out = w_out_a[0].astype(BF16)
    g_o = row2(gla_g_out[0])
    wg, wu, wd = ffn_w_gate[0].astype(BF16), ffn_w_up[0].astype(BF16), ffn_w_down[0].astype(BF16)
    gpm, gpf, gqf = row2(g_post_mix[0]), row2(g_pre_ffn[0]), row2(g_post_ffn[0])
    tail_mod = lambda m: jnp.concatenate([m[2], m[3], m[4], m[5]], axis=1)
    xl = _mix_ffn(a_l, ol_f, ol_b, proj_l, g_o, xl, tail_mod(ml), gpm, w_out, gpf, gqf, wg, wu, wd, t, 512)
    xc = _mix_ffn(a_x, ox_f, ox_b, proj_x, g_o, xc, tail_mod(mx), gpm, w_out, gpf, gqf, wg, wu, wd, nx, 512)

    ml, mx = mods(1)
    lam_init = 0.8 - 0.6 * math.exp(-0.3 * 1)
    w_qkv = w_qkv_c[0].astype(BF16)
    width = DIFF_HEADS * 2 * DIFF_DIM
    cos_c, sin_c = _rope_angles(t, DIFF_DIM)
    z32 = jnp.zeros_like(sin_c)
    cc = jnp.concatenate([cos_c] * 4, axis=-1)
    sa = jnp.concatenate([-sin_c, z32, -sin_c, z32], axis=-1)
    sb = jnp.concatenate([z32, sin_c, z32, sin_c], axis=-1)
    q1, k1, v1 = _diff_prep(xl, row2(g_pre_mix[1]), ml[0], ml[1], w_qkv, cc, sa, sb, t, 512, t // 512)
    kv_x = _norm_proj(xc, row2(g_pre_mix[1]), mx[0], mx[1], w_qkv[:, width:], nx, 512, BF16, "diff_kv_ctx")
    lamvec = jnp.zeros((8, DIFF_DIM), F32).at[0].set(diff_lq1[0]).at[1].set(diff_lk1[0]) \
        .at[2].set(diff_lq2[0]).at[3].set(diff_lk2[0])
    a1 = _diff_attention(q1, kv_x, k1, kv_x, v1, lamvec, row2(diff_g_out[0]), bsz, min(t, ATTN_ROWS), ATTN_SUB,
                         lam_init)
    xl = _out_proj_c(a1, xl, ml[2], row2(g_post_mix[1]), w_out_c[0].astype(BF16), t, 512)

    xl = _moe(xl, row2(g_pre_ffn[1]), ml[3], ml[4], ml[5], row2(g_post_ffn[1]), moe_w_router[0],
              moe_w_gate[0].astype(BF16), moe_w_up[0].astype(BF16), moe_w_down[0].astype(BF16), t)
    return xl.reshape(bsz, t, d)
```

```python
import functools
import math

import jax
import jax.numpy as jnp
from jax import lax
from jax.experimental import pallas as pl
from jax.experimental.pallas import tpu as pltpu

F32 = jnp.float32
BF16 = jnp.bfloat16
I32 = jnp.int32

EPS = 1e-6
ROPE_BASE = 10000.0
GRID_W = 64

D_MODEL = 1024
MLA_HEADS = 4
MLA_Q_RANK = 256
MLA_KV_RANK = 128
MLA_NOPE = 128
MLA_ROPE = 64
MLA_V = 128
GLA_HEADS = 4
GLA_DK = 64
GLA_DV = 128
GLA_RANK = 16
GLA_TAU = 16.0
DIFF_HEADS = 8
DIFF_DIM = 64
N_EXPERTS = 8
LANES = 128
GLA_CHUNK = 128
GLA_LEVELS = (64, 32, 16, 8, 4, 2, 1)
MOE_ROWS = 512
ATTN_ROWS = 2048
ATTN_SUB = 256
VMEM_LIMIT = 56 * 1024 * 1024

LOG2E = math.log2(math.e)
_NT = (((1,), (1,)), ((), ()))


def _cparams(sem):
    return pltpu.CompilerParams(dimension_semantics=sem, vmem_limit_bytes=VMEM_LIMIT)


def _rms(xf, g):
    return xf * lax.rsqrt(jnp.mean(xf * xf, axis=-1, keepdims=True) + EPS) * g


def _normmod(x, g, shift, scale):
    return _rms(x.astype(F32), g) * (1.0 + scale) + shift


def _silu(x):
    return x * (1.0 / (1.0 + jnp.exp(-x)))


def _const_spec(shape):
    n = len(shape)
    return pl.BlockSpec(shape, lambda *_: (0,) * n)


def _mod_kernel(c_ref, w_ref, b_ref, o_ref):
    s = _silu(c_ref[...])
    o_ref[0] = jnp.dot(s.astype(BF16), w_ref[0].astype(BF16), preferred_element_type=F32) + b_ref[0]


def _modulation(c_all, w_mod, b_mod):
    depth, d, n6 = w_mod.shape
    rows = c_all.shape[0]
    tn = 1536
    return pl.pallas_call(
        _mod_kernel,
        grid=(depth, n6 // tn),
        in_specs=[pl.BlockSpec((rows, d), lambda i, j: (0, 0)),
                  pl.BlockSpec((1, d, tn), lambda i, j: (i, 0, j)),
                  pl.BlockSpec((1, 1, tn), lambda i, j: (i, 0, j))],
        out_specs=pl.BlockSpec((1, rows, tn), lambda i, j: (i, 0, j)),
        out_shape=jax.ShapeDtypeStruct((depth, rows, n6), F32),
        compiler_params=_cparams(("arbitrary", "arbitrary")),
        name="modulation",
    )(c_all, w_mod, b_mod.reshape(depth, 1, n6))


def _proj_kernel(x_ref, g_ref, sh_ref, sc_ref, w_ref, o_ref):
    h = _normmod(x_ref[...], g_ref[...], sh_ref[0], sc_ref[0])
    o_ref[...] = jnp.dot(h.astype(BF16), w_ref[...], preferred_element_type=F32).astype(o_ref.dtype)


def _norm_proj(x, g, shift, scale, w, rows_per_mod, tm, out_dtype, name):
    n, d = x.shape
    nout = w.shape[1]
    mod_idx = lambda i: ((i * tm) // rows_per_mod, 0, 0)
    return pl.pallas_call(
        _proj_kernel,
        grid=(n // tm,),
        in_specs=[pl.BlockSpec((tm, d), lambda i: (i, 0)),
                  _const_spec((1, d)),
                  pl.BlockSpec((1, 1, d), mod_idx),
                  pl.BlockSpec((1, 1, d), mod_idx),
                  _const_spec((d, nout))],
        out_specs=pl.BlockSpec((tm, nout), lambda i: (i, 0)),
        out_shape=jax.ShapeDtypeStruct((n, nout), out_dtype),
        compiler_params=_cparams(("arbitrary",)),
        name=name,
    )(x, g, shift, scale, w)


def _mla_prep_kernel(cq_ref, ckv_ref, kr_ref, gq_ref, gkv_ref, wq_ref, wkv_ref, c2_ref, s2_ref,
                     q_ref, k_ref, v_ref, *, scale):
    c2 = c2_ref[...]
    s2 = s2_ref[...]

    def rope2(v):
        return v * c2 + pltpu.roll(v, 64, 1) * s2

    q = jnp.dot(_rms(cq_ref[...], gq_ref[...]).astype(BF16), wq_ref[...], preferred_element_type=F32)
    kv = jnp.dot(_rms(ckv_ref[...], gkv_ref[...]).astype(BF16), wkv_ref[...], preferred_element_type=F32)
    krope = rope2(kr_ref[...]).astype(BF16)
    for h in range(MLA_HEADS):
        b = h * 2 * LANES
        q_ref[:, b:b + LANES] = (q[:, b:b + LANES] * scale).astype(BF16)
        q_ref[:, b + LANES:b + 2 * LANES] = (rope2(q[:, b + LANES:b + 2 * LANES]) * scale).astype(BF16)
        k_ref[:, b:b + LANES] = kv[:, h * LANES:(h + 1) * LANES].astype(BF16)
        k_ref[:, b + LANES:b + 2 * LANES] = krope
    v_ref[...] = kv[:, MLA_HEADS * LANES:].astype(BF16)


def _mla_prep(proj, gq, gkv, wq, wkv, c2, s2, tm, rope_blocks):
    n = proj.shape[0]
    hq = MLA_HEADS * 2 * LANES
    rope_idx = (lambda i: (i % rope_blocks, 0)) if rope_blocks else (lambda i: (0, 0))
    return pl.pallas_call(
        functools.partial(_mla_prep_kernel, scale=float((MLA_NOPE + MLA_ROPE) ** -0.5) * LOG2E),
        grid=(n // tm,),
        in_specs=[pl.BlockSpec((tm, 256), lambda i: (i, 0)),
                  pl.BlockSpec((tm, 128), lambda i: (i, 2)),
                  pl.BlockSpec((tm, 128), lambda i: (i, 3)),
                  _const_spec((1, MLA_Q_RANK)), _const_spec((1, MLA_KV_RANK)),
                  _const_spec(wq.shape), _const_spec(wkv.shape),
                  pl.BlockSpec((tm, LANES), rope_idx), pl.BlockSpec((tm, LANES), rope_idx)],
        out_specs=[pl.BlockSpec((tm, hq), lambda i: (i, 0)),
                   pl.BlockSpec((tm, hq), lambda i: (i, 0)),
                   pl.BlockSpec((tm, MLA_HEADS * MLA_V), lambda i: (i, 0))],
        out_shape=[jax.ShapeDtypeStruct((n, hq), BF16), jax.ShapeDtypeStruct((n, hq), BF16),
                   jax.ShapeDtypeStruct((n, MLA_HEADS * MLA_V), BF16)],
        compiler_params=_cparams(("arbitrary",)),
        name="mla_prep",
    )(proj, proj, proj, gq, gkv, wq, wkv, c2, s2)


def _fill_kv(k_refs, v_refs, k_scr, v_scr):
    r0 = 0
    dv = v_refs[0].shape[1]
    for k_ref, v_ref in zip(k_refs, v_refs):
        r1 = r0 + k_ref.shape[0]
        k_scr[r0:r1, :] = k_ref[...]
        v_scr[r0:r1, 0:dv] = v_ref[...]
        r0 = r1
    v_scr[:, dv:] = jnp.ones((v_scr.shape[0], v_scr.shape[1] - dv), v_scr.dtype)


def _softmax_pv(q, k, v1, dv):
    s = lax.dot_general(q, k, _NT, preferred_element_type=F32)
    p = jnp.exp2(s - jnp.max(s, axis=-1, keepdims=True)).astype(BF16)
    o = jnp.dot(p, v1, preferred_element_type=F32)
    return o[:, 0:dv] / o[:, dv:dv + 1]


def _attn_kernel(*refs, n_src, sub):
    q_ref = refs[0]
    k_refs = refs[1:1 + n_src]
    v_refs = refs[1 + n_src:1 + 2 * n_src]
    o_ref, k_scr, v_scr = refs[1 + 2 * n_src:]

    @pl.when(pl.program_id(2) == 0)
    def _():
        _fill_kv(k_refs, v_refs, k_scr, v_scr)

    k = k_scr[...]
    v1 = v_scr[...]
    dv = o_ref.shape[1]
    for r0 in range(0, q_ref.shape[0], sub):
        o_ref[r0:r0 + sub, :] = _softmax_pv(q_ref[r0:r0 + sub, :], k, v1, dv).astype(o_ref.dtype)


def _attention(q, ks, vs, batch, heads, dq, dv, tq, sub, name):
    nq = q.shape[0] // batch // tq
    n_src = len(ks)
    tk = sum(k.shape[0] for k in ks) // batch
    in_specs = [pl.BlockSpec((tq, dq), lambda b, h, i: (b * nq + i, h))]
    for k in ks:
        in_specs.append(pl.BlockSpec((k.shape[0] // batch, dq), lambda b, h, i: (b, h)))
    for v in vs:
        in_specs.append(pl.BlockSpec((v.shape[0] // batch, dv), lambda b, h, i: (b, h)))
    return pl.pallas_call(
        functools.partial(_attn_kernel, n_src=n_src, sub=sub),
        grid=(batch, heads, nq),
        in_specs=in_specs,
        out_specs=pl.BlockSpec((tq, dv), lambda b, h, i: (b * nq + i, h)),
        out_shape=jax.ShapeDtypeStruct((q.shape[0], heads * dv), BF16),
        scratch_shapes=[pltpu.VMEM((tk, dq), BF16), pltpu.VMEM((tk, 2 * dv), BF16)],
        compiler_params=_cparams(("arbitrary", "arbitrary", "arbitrary")),
        name=name,
    )(q, *ks, *vs)


def _log_sigmoid(z):
    return jnp.minimum(z, 0.0) - jnp.log(1.0 + jnp.exp(-jnp.abs(z)))


def _gla_anchor(cum, level, rev):
    c, hk = cum.shape
    two = 2 * level
    a = level if rev else level - 1
    if two >= 8:
        return jnp.concatenate(
            [jnp.broadcast_to(cum[b * two + a:b * two + a + 1], (two, hk)) for b in range(c // two)], axis=0)
    pos = lax.broadcasted_iota(I32, cum.shape, 0) & (two - 1)
    anc = cum
    for p in range(two):
        if p != a:
            anc = jnp.where(pos == p, pltpu.roll(cum, (p - a) % c, 0), anc)
    return anc


def _gla_chunks(streams):
    c = GLA_CHUNK
    hk = GLA_HEADS * GLA_DK
    ri = lax.broadcasted_iota(I32, (c, c), 0)
    ci = lax.broadcasted_iota(I32, (c, c), 1)
    row = lax.broadcasted_iota(I32, (c, hk), 0)
    pair_lane = lax.broadcasted_iota(I32, (c, LANES), 1)
    keep_first = jnp.where(pair_lane < GLA_DK, 1.0, 0.0).astype(BF16)
    keep_second = jnp.where(pair_lane < GLA_DK, 0.0, 1.0).astype(BF16)

    def head_grams(zb, keys):
        outs = []
        for p in range(hk // LANES):
            kp = keys[:, p * LANES:(p + 1) * LANES]
            rhs = jnp.concatenate([kp * keep_first, kp * keep_second], axis=0)
            outs.append(lax.dot_general(zb[:, p * LANES:(p + 1) * LANES], rhs, _NT, preferred_element_type=F32))
        return jnp.concatenate(outs, axis=1)

    def cumulative(la, rev):
        tri = jnp.where((ci >= ri) if rev else (ci <= ri), 1.0, 0.0).astype(BF16)
        la_hi = la.astype(BF16)
        rest = la - la_hi.astype(F32)
        la_mid = rest.astype(BF16)
        la_lo = (rest - la_mid.astype(F32)).astype(BF16)
        return (jnp.dot(tri, la_hi, preferred_element_type=F32) + jnp.dot(tri, la_mid, preferred_element_type=F32)
                + jnp.dot(tri, la_lo, preferred_element_type=F32))

    cums = [cumulative(la, rev) for (_, _, _, la, _, _, rev) in streams]
    qss = [q * (GLA_DK ** -0.5) for (q, *_) in streams]
    vbs = [v.astype(BF16) for (_, _, v, *_) in streams]

    o_inter = []
    for (q, k, v, la, s_ref, lvl, rev), cum, qs in zip(streams, cums, qss):
        tot = cum[0:1] if rev else cum[c - 1:c]
        s_old = s_ref[...]
        qh = (qs * jnp.exp(cum)).astype(BF16)
        o_inter.append(lax.dot_general(qh, s_old.astype(BF16), _NT, preferred_element_type=F32))
        kh = (k * jnp.exp(tot - cum)).astype(BF16)
        u = jnp.dot(v.T.astype(BF16), kh, preferred_element_type=F32)
        bd = ((lax.broadcasted_iota(I32, u.shape, 0) // GLA_DV)
              == (lax.broadcasted_iota(I32, u.shape, 1) // GLA_DK))
        s_ref[...] = s_old * jnp.exp(tot) + jnp.where(bd, u, 0.0)

    ones = jnp.ones((c, hk), BF16)
    atts = [jnp.where(lvl == 0, head_grams((qs * k).astype(BF16), ones), 0.0)
            for (q, k, v, la, s_ref, lvl, rev), qs in zip(streams, qss)]
    for level in GLA_LEVELS:
        upper = (row & level) != 0
        for i, ((q, k, v, la, s_ref, lvl, rev), cum, qs) in enumerate(zip(streams, cums, qss)):
            is_q = jnp.logical_not(upper) if rev else upper
            x = cum - _gla_anchor(cum, level, rev)
            zb = (jnp.where(is_q, qs, k) * jnp.exp(jnp.where(is_q, x, -x))).astype(BF16)
            atts[i] = jnp.where(lvl == level, head_grams(zb, zb), atts[i])
    outs = []
    for att, vb, oi in zip(atts, vbs, o_inter):
        attb = att.astype(BF16)
        outs.append(oi + jnp.concatenate(
            [jnp.dot(attb[:, h * c:(h + 1) * c], vb[:, h * GLA_DV:(h + 1) * GLA_DV], preferred_element_type=F32)
             for h in range(GLA_HEADS)], axis=1))
    return outs


def _gla_kernel(qf_ref, kf_ref, vf_ref, gf_ref, qb_ref, kb_ref, vb_ref, gb_ref,
                wf_ref, bf_ref, wb_ref, bb_ref, lvlf_ref, lvlb_ref, s0f_ref, s0b_ref,
                of_ref, ob_ref, sf_ref, sb_ref, stf, stb):
    j = pl.program_id(1)

    @pl.when(j == 0)
    def _():
        stf[...] = s0f_ref[0]
        stb[...] = s0b_ref[0]

    def log_decay(g_ref, w_ref, b_ref):
        z = jnp.dot(g_ref[...].astype(BF16), w_ref[...], preferred_element_type=F32) + b_ref[...]
        return _log_sigmoid(z) * (1.0 / GLA_TAU)

    o_f, o_b = _gla_chunks([
        (qf_ref[...], kf_ref[...], vf_ref[...], log_decay(gf_ref, wf_ref, bf_ref), stf, lvlf_ref[...], False),
        (qb_ref[...], kb_ref[...], vb_ref[...], log_decay(gb_ref, wb_ref, bb_ref), stb, lvlb_ref[...], True)])
    of_ref[...] = o_f
    ob_ref[...] = o_b

    @pl.when(j == pl.num_programs(1) - 1)
    def _():
        sf_ref[0] = stf[...]
        sb_ref[0] = stb[...]


def _gla_level_maps():
    c = GLA_CHUNK
    t = jnp.arange(c, dtype=I32)[:, None]
    s = jnp.arange(c, dtype=I32)[None, :]
    diff = t ^ s
    top = jnp.zeros((c, c), I32)
    for level in GLA_LEVELS:
        top = jnp.where((top == 0) & ((diff & level) != 0), level, top)
    fwd = jnp.where(t == s, 0, jnp.where(t > s, top, -1))
    bwd = jnp.where(t == s, 0, jnp.where(t < s, top, -1))
    return jnp.tile(fwd, (1, GLA_HEADS)), jnp.tile(bwd, (1, GLA_HEADS))


def _gla_scan(proj, wf, bf, wb, bb, lvl_f, lvl_b, s0f, s0b, batch):
    n = proj.shape[0]
    c = GLA_CHUNK
    nch = n // batch // c
    hk = GLA_HEADS * GLA_DK
    hv = GLA_HEADS * GLA_DV
    fwd = lambda b, j: b * nch + j
    bwd = lambda b, j: b * nch + (nch - 1 - j)

    def specs(row):
        return [pl.BlockSpec((c, hk), lambda b, j: (row(b, j), 2)),
                pl.BlockSpec((c, hk), lambda b, j: (row(b, j), 3)),
                pl.BlockSpec((c, hv), lambda b, j: (row(b, j), 2)),
                pl.BlockSpec((c, LANES), lambda b, j: (row(b, j), 16))]

    st_spec = pl.BlockSpec((1, hv, hk), lambda b, j: (b, 0, 0))
    return pl.pallas_call(
        _gla_kernel,
        grid=(batch, nch),
        in_specs=specs(fwd) + specs(bwd) + [
            _const_spec((LANES, hk)), _const_spec((1, hk)), _const_spec((LANES, hk)), _const_spec((1, hk)),
            _const_spec(lvl_f.shape), _const_spec(lvl_b.shape), st_spec, st_spec],
        out_specs=[pl.BlockSpec((c, hv), lambda b, j: (fwd(b, j), 0)),
                   pl.BlockSpec((c, hv), lambda b, j: (bwd(b, j), 0)),
                   st_spec, st_spec],
        out_shape=[jax.ShapeDtypeStruct((n, hv), F32), jax.ShapeDtypeStruct((n, hv), F32),
                   jax.ShapeDtypeStruct((batch, hv, hk), F32), jax.ShapeDtypeStruct((batch, hv, hk), F32)],
        scratch_shapes=[pltpu.VMEM((hv, hk), F32), pltpu.VMEM((hv, hk), F32)],
        compiler_params=_cparams(("arbitrary", "arbitrary")),
        name="gla_scan",
    )(proj, proj, proj, proj, proj, proj, proj, proj, wf, bf, wb, bb, lvl_f, lvl_b, s0f, s0b)


def _swiglu_residual(x, h, gate, g_post, wg_ref, wu_ref, wd_ref, fc):
    acc = jnp.zeros(x.shape, F32)
    for c0 in range(0, wg_ref.shape[1], fc):
        g = jnp.dot(h, wg_ref[:, c0:c0 + fc], preferred_element_type=F32)
        u = jnp.dot(h, wu_ref[:, c0:c0 + fc], preferred_element_type=F32)
        act = (_silu(g) * u).astype(BF16)
        acc = acc + jnp.dot(act, wd_ref[c0:c0 + fc, :], preferred_element_type=F32)
    return x + gate * _rms(acc, g_post)


def _mix_ffn_kernel(a_ref, of_ref, ob_ref, r_ref, go_ref, x_ref, mod_ref, gpm_ref, wo_ref,
                    gpre_ref, gpost_ref, wg_ref, wu_ref, wd_ref, o_ref, *, fc):
    o = of_ref[...] + ob_ref[...]
    r = r_ref[...]
    go = go_ref[...]
    parts = []
    for h in range(GLA_HEADS):
        sl = slice(h * GLA_DV, (h + 1) * GLA_DV)
        parts.append(_rms(o[:, sl], go) * _silu(r[:, sl]))
    g = jnp.concatenate(parts, axis=-1).astype(BF16)
    na = a_ref.shape[1]
    y = (jnp.dot(a_ref[...], wo_ref[0:na, :], preferred_element_type=F32)
         + jnp.dot(g, wo_ref[na:, :], preferred_element_type=F32))
    mod = mod_ref[0]
    x = x_ref[...] + mod[0:1] * _rms(y, gpm_ref[...])
    h = _normmod(x, gpre_ref[...], mod[1:2], mod[2:3]).astype(BF16)
    o_ref[...] = _swiglu_residual(x, h, mod[3:4], gpost_ref[...], wg_ref, wu_ref, wd_ref, fc)


def _mix_ffn(a, o_f, o_b, proj, g_o, x, mod, g_post_mix, w_out, g_pre, g_post, wg, wu, wd, rows_per_mod, tm):
    n, d = x.shape
    hv = GLA_HEADS * GLA_DV
    row = lambda i: (i, 0)
    single = pl.Buffered(1)
    resident = lambda w: pl.BlockSpec(w.shape, lambda i: (0, 0), pipeline_mode=single)
    return pl.pallas_call(
        functools.partial(_mix_ffn_kernel, fc=256),
        grid=(n // tm,),
        in_specs=[pl.BlockSpec((tm, a.shape[1]), row),
                  pl.BlockSpec((tm, hv), row),
                  pl.BlockSpec((tm, hv), row),
                  pl.BlockSpec((tm, hv), lambda i: (i, 3)),
                  _const_spec((1, GLA_DV)),
                  pl.BlockSpec((tm, d), row),
                  pl.BlockSpec((1, mod.shape[1], d), lambda i: ((i * tm) // rows_per_mod, 0, 0)),
                  _const_spec((1, d)), resident(w_out),
                  _const_spec((1, d)), _const_spec((1, d)), resident(wg), resident(wu), resident(wd)],
        out_specs=pl.BlockSpec((tm, d), row),
        out_shape=jax.ShapeDtypeStruct((n, d), F32),
        compiler_params=_cparams(("arbitrary",)),
        name="mix_ffn",
    )(a, o_f, o_b, proj, g_o, x, mod, g_post_mix, w_out, g_pre, g_post, wg, wu, wd)


def _diff_prep_kernel(x_ref, g_ref, sh_ref, sc_ref, w_ref, c_ref, sa_ref, sb_ref, q_ref, k_ref, v_ref):
    h = _normmod(x_ref[...], g_ref[...], sh_ref[0], sc_ref[0]).astype(BF16)
    qkv = jnp.dot(h, w_ref[...], preferred_element_type=F32)
    cc, sa, sb = c_ref[...], sa_ref[...], sb_ref[...]
    width = q_ref.shape[1]

    def rope(v):
        return v * cc + pltpu.roll(v, 96, 1) * sa + pltpu.roll(v, 32, 1) * sb

    for j in range(width // LANES):
        sl = slice(j * LANES, (j + 1) * LANES)
        q_ref[:, sl] = (rope(qkv[:, sl]) * (DIFF_DIM ** -0.5 * LOG2E)).astype(BF16)
        k_ref[:, sl] = rope(qkv[:, width + j * LANES:width + (j + 1) * LANES]).astype(BF16)
    v_ref[...] = qkv[:, 2 * width:].astype(BF16)


def _diff_prep(x, g, shift, scale, w, cc, sa, sb, rows_per_mod, tm, rope_blocks):
    n, d = x.shape
    width = w.shape[1] // 3
    mod_idx = lambda i: ((i * tm) // rows_per_mod, 0, 0)
    rope_spec = pl.BlockSpec((tm, LANES), lambda i: (i % rope_blocks, 0))
    out_spec = pl.BlockSpec((tm, width), lambda i: (i, 0))
    return pl.pallas_call(
        _diff_prep_kernel,
        grid=(n // tm,),
        in_specs=[pl.BlockSpec((tm, d), lambda i: (i, 0)), _const_spec((1, d)),
                  pl.BlockSpec((1, 1, d), mod_idx), pl.BlockSpec((1, 1, d), mod_idx),
                  _const_spec(w.shape), rope_spec, rope_spec, rope_spec],
        out_specs=[out_spec, out_spec, out_spec],
        out_shape=[jax.ShapeDtypeStruct((n, width), BF16)] * 3,
        compiler_params=_cparams(("arbitrary",)),
        name="diff_prep",
    )(x, g, shift, scale, w, cc, sa, sb)


def _diff_attn_kernel(q_ref, kx_ref, kl_ref, vx_ref, vl_ref, lam_ref, go_ref, o_ref, k_scr, v_scr, *,
                      lam_init, sub):
    @pl.when(pl.program_id(2) == 0)
    def _():
        _fill_kv((kx_ref, kl_ref), (vx_ref, vl_ref), k_scr, v_scr)

    lv = lam_ref[...]
    lam = (jnp.exp(jnp.sum(lv[0:1] * lv[1:2], axis=-1, keepdims=True))
           - jnp.exp(jnp.sum(lv[2:3] * lv[3:4], axis=-1, keepdims=True)) + lam_init)
    k = k_scr[...]
    v1 = v_scr[...]
    go = go_ref[...]
    lane = lax.broadcasted_iota(I32, (sub, LANES), 1)
    for r0 in range(0, q_ref.shape[0], sub):
        q = q_ref[r0:r0 + sub, :]
        zero = jnp.zeros_like(q)
        o = (_softmax_pv(jnp.where(lane < DIFF_DIM, q, zero), k, v1, LANES)
             - lam * _softmax_pv(jnp.where(lane >= DIFF_DIM, q, zero), k, v1, LANES))
        o_ref[r0:r0 + sub, :] = (_rms(o, go) * (1.0 - lam_init)).astype(o_ref.dtype)


def _diff_attention(q, kx, kl, vx, vl, lamvec, g_o, batch, tq, sub, lam_init):
    n = q.shape[0]
    nq = n // batch // tq
    tx = kx.shape[0] // batch
    tl = kl.shape[0] // batch
    return pl.pallas_call(
        functools.partial(_diff_attn_kernel, lam_init=lam_init, sub=sub),
        grid=(batch, DIFF_HEADS, nq),
        in_specs=[pl.BlockSpec((tq, LANES), lambda b, h, i: (b * nq + i, h)),
                  pl.BlockSpec((tx, LANES), lambda b, h, i: (b, h)),
                  pl.BlockSpec((tl, LANES), lambda b, h, i: (b, h)),
                  pl.BlockSpec((tx, LANES), lambda b, h, i: (b, DIFF_HEADS + h)),
                  pl.BlockSpec((tl, LANES), lambda b, h, i: (b, h)),
                  _const_spec(lamvec.shape), _const_spec((1, LANES))],
        out_specs=pl.BlockSpec((tq, LANES), lambda b, h, i: (b * nq + i, h)),
        out_shape=jax.ShapeDtypeStruct((n, DIFF_HEADS * LANES), BF16),
        scratch_shapes=[pltpu.VMEM((tx + tl, LANES), BF16), pltpu.VMEM((tx + tl, 2 * LANES), BF16)],
        compiler_params=_cparams(("arbitrary", "arbitrary", "arbitrary")),
        name="diff_attention",
    )(q, kx, kl, vx, vl, lamvec, g_o)


def _out_c_kernel(a_ref, x_ref, gate_ref, gp_ref, w_ref, o_ref):
    y = jnp.dot(a_ref[...], w_ref[...], preferred_element_type=F32)
    o_ref[...] = x_ref[...] + gate_ref[0] * _rms(y, gp_ref[...])


def _out_proj_c(a, x, gate, g_post, w, rows_per_mod, tm):
    n, d = x.shape
    return pl.pallas_call(
        _out_c_kernel,
        grid=(n // tm,),
        in_specs=[pl.BlockSpec((tm, a.shape[1]), lambda i: (i, 0)),
                  pl.BlockSpec((tm, d), lambda i: (i, 0)),
                  pl.BlockSpec((1, 1, d), lambda i: ((i * tm) // rows_per_mod, 0, 0)),
                  _const_spec((1, d)),
                  _const_spec(w.shape)],
        out_specs=pl.BlockSpec((tm, d), lambda i: (i, 0)),
        out_shape=jax.ShapeDtypeStruct((n, d), F32),
        compiler_params=_cparams(("arbitrary",)),
        name="out_proj_c",
    )(a, x, gate, g_post, w)


def _router_kernel(x_ref, g_ref, sh_ref, sc_ref, wr_ref, h_ref, meta_ref, gates_ref, cnt_ref, carry_ref):
    i = pl.program_id(0)

    @pl.when(i == 0)
    def _():
        carry_ref[...] = jnp.zeros_like(carry_ref)

    h = _normmod(x_ref[...], g_ref[...], sh_ref[0], sc_ref[0])
    h_ref[...] = h
    tm = h.shape[0]
    logits = jnp.dot(h, wr_ref[...], precision=lax.Precision.HIGHEST, preferred_element_type=F32)
    lane = lax.broadcasted_iota(I32, logits.shape, 1).astype(F32)
    neg = jnp.float32(-jnp.inf)
    logits = jnp.where(lane < N_EXPERTS, logits, neg)
    m0 = jnp.max(logits, axis=-1, keepdims=True)
    i0 = jnp.min(jnp.where(logits == m0, lane, float(LANES)), axis=-1, keepdims=True)
    rest = jnp.where(lane == i0, neg, logits)
    m1 = jnp.max(rest, axis=-1, keepdims=True)
    i1 = jnp.min(jnp.where(rest == m1, lane, float(LANES)), axis=-1, keepdims=True)
    e = jnp.exp(m1 - m0)
    g0 = 1.0 / (1.0 + e)
    g1 = e / (1.0 + e)
    hit = jnp.where(lane == i0, 1.0, jnp.where(lane == i1, 1.0, 0.0)).astype(F32)
    ri = lax.broadcasted_iota(I32, (tm, tm), 0)
    ci = lax.broadcasted_iota(I32, (tm, tm), 1)
    below = jnp.where(ci < ri, 1.0, 0.0).astype(BF16)
    prefix = jnp.dot(below, hit.astype(BF16), preferred_element_type=F32) + carry_ref[...]
    r0 = jnp.sum(jnp.where(lane == i0, prefix, 0.0), axis=-1, keepdims=True)
    r1 = jnp.sum(jnp.where(lane == i1, prefix, 0.0), axis=-1, keepdims=True)
    carry_ref[...] = carry_ref[...] + jnp.sum(hit, axis=0, keepdims=True)
    meta = jnp.where(lane == 0.0, i0, jnp.where(lane == 1.0, i1, jnp.where(lane == 2.0, r0, r1)))
    meta_ref[...] = meta.astype(I32)
    gates_ref[...] = jnp.where(lane == 0.0, g0, g1)
    cnt_ref[...] = carry_ref[...]


def _router(x, g, shift, scale, w_router, rows_per_mod, tm):
    n, d = x.shape
    mod_idx = lambda i: ((i * tm) // rows_per_mod, 0, 0)
    row = lambda i: (i, 0)
    return pl.pallas_call(
        _router_kernel,
        grid=(n // tm,),
        in_specs=[pl.BlockSpec((tm, d), row), _const_spec((1, d)),
                  pl.BlockSpec((1, 1, d), mod_idx), pl.BlockSpec((1, 1, d), mod_idx),
                  _const_spec(w_router.shape)],
        out_specs=[pl.BlockSpec((tm, d), row), pl.BlockSpec((tm, LANES), row), pl.BlockSpec((tm, LANES), row),
                   _const_spec((1, LANES))],
        out_shape=[jax.ShapeDtypeStruct((n, d), F32), jax.ShapeDtypeStruct((n, LANES), I32),
                   jax.ShapeDtypeStruct((n, LANES), F32), jax.ShapeDtypeStruct((1, LANES), F32)],
        scratch_shapes=[pltpu.VMEM((1, LANES), F32)],
        compiler_params=_cparams(("arbitrary",)),
        name="moe_router",
    )(x, g, shift, scale, w_router)


def _expert_stream_kernel(be_ref, nb_ref, idx_in_ref, idx_out_ref, h_hbm, wg_ref, wu_ref, wd_ref, y_hbm,
                          xbuf, ybuf, sem_in, sem_out, *, sub, nb):
    j = pl.program_id(0)
    bm = xbuf.shape[1]
    nbu = nb_ref[0]
    c = j - 1
    live = (c >= 0) & (c < nbu)
    gather_ok = j < nbu
    scatter_ok = (j >= 2) & (j - 2 < nbu)
    in_slot = j % 2
    cur = (j + 1) % 2

    def start_in(r):
        t = idx_in_ref[0, 0, r]
        pltpu.make_async_copy(h_hbm.at[pl.ds(t, 1), :], xbuf.at[in_slot, pl.ds(r, 1), :],
                              sem_in.at[in_slot]).start()

    def start_out(r):
        t = idx_out_ref[0, 0, r]
        pltpu.make_async_copy(ybuf.at[in_slot, pl.ds(r, 1), :], y_hbm.at[pl.ds(t, 1), :],
                              sem_out.at[in_slot]).start()

    def wait_in(slot):
        pltpu.make_async_copy(h_hbm.at[pl.ds(0, bm), :], xbuf.at[slot], sem_in.at[slot]).wait()

    def wait_out(slot):
        pltpu.make_async_copy(ybuf.at[slot], y_hbm.at[pl.ds(0, bm), :], sem_out.at[slot]).wait()

    def loop(fn):
        def body(r, carry):
            fn(r)
            return carry
        lax.fori_loop(0, bm, body, 0, unroll=8)

    def expert(interleave):
        x = xbuf[cur].astype(BF16)
        f_dim = wg_ref.shape[2]
        n_chunks = f_dim // sub
        per = -(-bm // max(1, (2 * n_chunks) // 3))
        part = jnp.zeros((bm, wd_ref.shape[2]), F32)
        for ci in range(n_chunks):
            c0 = ci * sub
            g = jnp.dot(x, wg_ref[0, :, c0:c0 + sub], preferred_element_type=F32)
            u = jnp.dot(x, wu_ref[0, :, c0:c0 + sub], preferred_element_type=F32)
            act = (_silu(g) * u).astype(BF16)
            part = part + jnp.dot(act, wd_ref[0, c0:c0 + sub, :], preferred_element_type=F32)
            if interleave:
                for r in range(ci * per, min((ci + 1) * per, bm)):
                    start_in(r)
                    start_out(r)
        return part

    def store_result(part):
        @pl.when(j >= 3)
        def _():
            wait_out(cur)
        ybuf[cur] = part

    @pl.when(live)
    def _():
        wait_in(cur)

    steady = live & gather_ok & scatter_ok

    @pl.when(steady)
    def _():
        store_result(expert(True))

    @pl.when(jnp.logical_not(steady))
    def _():
        @pl.when(gather_ok)
        def _():
            loop(start_in)

        @pl.when(scatter_ok)
        def _():
            loop(start_out)

        @pl.when(live)
        def _():
            store_result(expert(False))

        @pl.when(jnp.logical_not(live) & (j >= 3) & (j - 3 < nbu))
        def _():
            wait_out(cur)

        @pl.when((c >= nbu) & (c < nb))
        def _():
            ybuf[cur] = jnp.zeros(ybuf.shape[1:], F32)
            fill = pltpu.make_async_copy(ybuf.at[cur], y_hbm.at[pl.ds(pl.multiple_of(c * bm, bm), bm), :],
                                         sem_out.at[cur])
            fill.start()
            fill.wait()

    @pl.when((j == nb + 1) & (nb - 1 < nbu))
    def _():
        wait_out(in_slot)


def _experts_stream(h, row_token, out_row, blk_e, nb_used, wg, wu, wd, n_out):
    n, d = h.shape
    bm = MOE_ROWS
    nb = row_token.shape[0] // bm
    single = pl.Buffered(1)

    def expert_of(j, be, nbu):
        return be[jnp.clip(j - 1, 0, nbu[0] - 1)]

    w_in_spec = pl.BlockSpec((1, d, wg.shape[2]), lambda j, be, nbu: (expert_of(j, be, nbu), 0, 0),
                             pipeline_mode=single)
    w_out_spec = pl.BlockSpec((1, wd.shape[1], d), lambda j, be, nbu: (expert_of(j, be, nbu), 0, 0),
                              pipeline_mode=single)
    return pl.pallas_call(
        functools.partial(_expert_stream_kernel, sub=256, nb=nb),
        grid_spec=pltpu.PrefetchScalarGridSpec(
            num_scalar_prefetch=2,
            grid=(nb + 2,),
            in_specs=[pl.BlockSpec((1, 1, bm), lambda j, be, nbu: (jnp.minimum(j, nb - 1), 0, 0),
                                   memory_space=pltpu.SMEM),
                      pl.BlockSpec((1, 1, bm), lambda j, be, nbu: (jnp.clip(j - 2, 0, nb - 1), 0, 0),
                                   memory_space=pltpu.SMEM),
                      pl.BlockSpec(memory_space=pl.ANY),
                      w_in_spec, w_in_spec, w_out_spec],
            out_specs=pl.BlockSpec(memory_space=pl.ANY),
            scratch_shapes=[pltpu.VMEM((2, bm, d), F32), pltpu.VMEM((2, bm, d), F32),
                            pltpu.SemaphoreType.DMA((2,)), pltpu.SemaphoreType.DMA((2,))]),
        out_shape=jax.ShapeDtypeStruct((n_out, d), F32),
        compiler_params=_cparams(("arbitrary",)),
        name="moe_experts",
    )(blk_e, nb_used, row_token.reshape(nb, 1, bm), out_row.reshape(nb, 1, bm), h, wg, wu, wd)


def _combine2_kernel(y0_ref, y1_ref, gates_ref, x_ref, gate_ref, gp_ref, o_ref):
    gt = gates_ref[...]
    f = y0_ref[...] * gt[:, 0:1] + y1_ref[...] * gt[:, 1:2]
    o_ref[...] = x_ref[...] + gate_ref[0] * _rms(f, gp_ref[...])


def _combine2(y, gates, x, gate, g_post, rows_per_mod, tm):
    n, d = x.shape
    nt = n // tm
    return pl.pallas_call(
        _combine2_kernel,
        grid=(nt,),
        in_specs=[pl.BlockSpec((tm, d), lambda i: (i, 0)),
                  pl.BlockSpec((tm, d), lambda i: (nt + i, 0)),
                  pl.BlockSpec((tm, LANES), lambda i: (i, 0)),
                  pl.BlockSpec((tm, d), lambda i: (i, 0)),
                  pl.BlockSpec((1, 1, d), lambda i: ((i * tm) // rows_per_mod, 0, 0)),
                  _const_spec((1, d))],
        out_specs=pl.BlockSpec((tm, d), lambda i: (i, 0)),
        out_shape=jax.ShapeDtypeStruct((n, d), F32),
        compiler_params=_cparams(("arbitrary",)),
        name="moe_combine",
    )(y, y, gates, x, gate, g_post)


def _moe(x, g_pre, shift, scale, gate, g_post, w_router, wg, wu, wd, rows_per_mod):
    n, d = x.shape
    bm = MOE_ROWS
    wr = jnp.zeros((d, LANES), F32).at[:, :N_EXPERTS].set(w_router)
    h, meta, gates, counts = _router(x, g_pre, shift, scale, wr, rows_per_mod, 512)
    cnt = counts[0, :N_EXPERTS].astype(I32)
    padded = (cnt + bm - 1) // bm * bm
    pend = jnp.cumsum(padded)
    pstart = pend - padded
    dest = pstart[meta[:, 0:2]] + meta[:, 2:4]
    cap = 2 * n + N_EXPERTS * bm
    nb = cap // bm
    code = 2 * jnp.arange(n, dtype=I32)[:, None] + jnp.arange(2, dtype=I32)[None, :]
    info = jnp.full((cap,), -1, I32).at[dest.reshape(-1)].set(code.reshape(-1))
    real = info >= 0
    row_token = jnp.where(real, info >> 1, 0)
    spill = 2 * n - 1 + jnp.cumsum(jnp.logical_not(real).astype(I32))
    out_row = jnp.where(real, (info & 1) * n + (info >> 1), spill)
    nb_used = (pend[-1] // bm).astype(I32).reshape(1)
    blk_start = jnp.arange(nb, dtype=I32)[:, None] * bm
    blk_e = jnp.minimum(jnp.sum((pend[None, :] <= blk_start).astype(I32), axis=1), N_EXPERTS - 1)
    y = _experts_stream(h, row_token, out_row, blk_e, nb_used, wg, wu, wd, cap)
    return _combine2(y, gates, x, gate, g_post, rows_per_mod, 512)


def _rope_angles(n_tokens, rot_dim):
    rows = n_tokens // GRID_W
    row = jnp.repeat(jnp.arange(rows, dtype=F32), GRID_W)
    col = jnp.tile(jnp.arange(GRID_W, dtype=F32), rows)
    n_freq = rot_dim // 4
    freq = ROPE_BASE ** (-jnp.arange(n_freq, dtype=F32) / n_freq)
    ang = jnp.concatenate([row[:, None] * freq, col[:, None] * freq], axis=-1)
    return jnp.cos(ang), jnp.sin(ang)


def _rot_cols(w):
    half = w.shape[-1] // 2
    return jnp.concatenate([-w[..., half:], w[..., :half]], axis=-1)


def kernel(x, c, ctx, c_ctx, w_mod, b_mod, g_pre_mix, g_post_mix, g_pre_ffn, g_post_ffn,
           w_in_a, mla_g_q, mla_w_uq, mla_g_kv, mla_w_ukv,
           gla_w_gate_f, gla_b_gate_f, gla_w_gate_b, gla_b_gate_b, gla_g_out, w_out_a,
           w_qkv_c, diff_lq1, diff_lk1, diff_lq2, diff_lk2, diff_g_out, w_out_c,
           ffn_w_gate, ffn_w_up, ffn_w_down,
           moe_w_router, moe_w_gate, moe_w_up, moe_w_down):
    bsz, t, d = x.shape
    tx = ctx.shape[1]
    n, nx = bsz * t, bsz * tx
    xl = x.reshape(n, d)
    xc = ctx.reshape(nx, d)

    rows = -(-(bsz + 1) // 8) * 8
    c_all = jnp.zeros((rows, d), F32).at[:bsz].set(c).at[bsz].set(c_ctx)
    mod_all = _modulation(c_all, w_mod, b_mod).reshape(2, rows, 6, d)

    def mods(i):
        lat = [mod_all[i, :bsz, k].reshape(bsz, 1, d) for k in range(6)]
        cx = [mod_all[i, bsz, k].reshape(1, 1, d) for k in range(6)]
        return lat, cx

    row2 = lambda v: v.reshape(1, -1)

    ml, mx = mods(0)
    wi = w_in_a[0]
    cq, ckv, kr, gq, gk, gv, gr, af, ab = jnp.split(
        wi, [256, 384, 448, 704, 960, 1472, 1984, 2000], axis=-1)
    w_in = jnp.concatenate([cq, ckv, kr, _rot_cols(kr), gq, gk, gv, gr, af, ab,
                            jnp.zeros((d, LANES - 2 * GLA_RANK), F32)], axis=-1).astype(BF16)
    uq = mla_w_uq[0].reshape(MLA_Q_RANK, MLA_HEADS, MLA_NOPE + MLA_ROPE)
    wq = jnp.concatenate([uq[..., :MLA_NOPE], uq[..., MLA_NOPE:], _rot_cols(uq[..., MLA_NOPE:])],
                         axis=-1).reshape(MLA_Q_RANK, MLA_HEADS * 2 * LANES).astype(BF16)
    ukv = mla_w_ukv[0].reshape(MLA_KV_RANK, MLA_HEADS, MLA_NOPE + MLA_V)
    wkv = jnp.concatenate([ukv[..., :MLA_NOPE].reshape(MLA_KV_RANK, -1),
                           ukv[..., MLA_NOPE:].reshape(MLA_KV_RANK, -1)], axis=-1).astype(BF16)
    cos_a, sin_a = _rope_angles(t, MLA_ROPE)
    zpad = jnp.zeros((t, LANES - MLA_ROPE), F32)
    c2 = jnp.concatenate([cos_a, cos_a, zpad], axis=-1)
    s2 = jnp.concatenate([sin_a, sin_a, zpad], axis=-1)
    lane = jnp.arange(LANES)
    c2x = jnp.broadcast_to(jnp.where(lane < MLA_ROPE, 1.0, 0.0).astype(F32), (tx, LANES))
    s2x = jnp.zeros((tx, LANES), F32)

    proj_l = _norm_proj(xl, row2(g_pre_mix[0]), ml[0], ml[1], w_in, t, 512, F32, "in_proj_a")
    proj_x = _norm_proj(xc, row2(g_pre_mix[0]), mx[0], mx[1], w_in, nx, 512, F32, "in_proj_a_ctx")
    gq_, gkv_ = row2(mla_g_q[0]), row2(mla_g_kv[0])
    q_l, k_l, v_l = _mla_prep(proj_l, gq_, gkv_, wq, wkv, c2, s2, 512, t // 512)
    q_x, k_x, v_x = _mla_prep(proj_x, gq_, gkv_, wq, wkv, c2x, s2x, tx, 0)
    a_l = _attention(q_l, [k_x, k_l], [v_x, v_l], bsz, MLA_HEADS, 2 * LANES, MLA_V, min(t, ATTN_ROWS), ATTN_SUB, "mla_attention")
    a_x = _attention(q_x, [k_x], [v_x], bsz, MLA_HEADS, 2 * LANES, MLA_V, tx, tx, "mla_attention_ctx")

    hk = GLA_HEADS * GLA_DK
    hv = GLA_HEADS * GLA_DV
    wf = jnp.zeros((LANES, hk), F32).at[:GLA_RANK].set(gla_w_gate_f[0]).astype(BF16)
    wb = jnp.zeros((LANES, hk), F32).at[GLA_RANK:2 * GLA_RANK].set(gla_w_gate_b[0]).astype(BF16)
    lvl_f, lvl_b = _gla_level_maps()
    s0 = jnp.zeros((bsz, hv, hk), F32)
    gla_args = (wf, row2(gla_b_gate_f[0]), wb, row2(gla_b_gate_b[0]), lvl_f, lvl_b)
    ox_f, ox_b, sx_f, sx_b = _gla_scan(proj_x, *gla_args, s0, s0, bsz)
    ol_f, ol_b, _, _ = _gla_scan(proj_l, *gla_args, sx_f, sx_b, bsz)

    w_out = w_out_a[0].astype(BF16)
    g_o = row2(gla_g_out[0])
    wg, wu, wd = ffn_w_gate[0].astype(BF16), ffn_w_up[0].astype(BF16), ffn_w_down[0].astype(BF16)
    gpm, gpf, gqf = row2(g_post_mix[0]), row2(g_pre_ffn[0]), row2(g_post_ffn[0])
    tail_mod = lambda m: jnp.concatenate([m[2], m[3], m[4], m[5]], axis=1)
    xl = _mix_ffn(a_l, ol_f, ol_b, proj_l, g_o, xl, tail_mod(ml), gpm, w_out, gpf, gqf, wg, wu, wd, t, 512)
    xc = _mix_ffn(a_x, ox_f, ox_b, proj_x, g_o, xc, tail_mod(mx), gpm, w_out, gpf, gqf, wg, wu, wd, nx, 512)

    ml, mx = mods(1)
    lam_init = 0.8 - 0.6 * math.exp(-0.3 * 1)
    w_qkv = w_qkv_c[0].astype(BF16)
    width = DIFF_HEADS * 2 * DIFF_DIM
    cos_c, sin_c = _rope_angles(t, DIFF_DIM)
    z32 = jnp.zeros_like(sin_c)
    cc = jnp.concatenate([cos_c] * 4, axis=-1)
    sa = jnp.concatenate([-sin_c, z32, -sin_c, z32], axis=-1)
    sb = jnp.concatenate([z32, sin_c, z32, sin_c], axis=-1)
    q1, k1, v1 = _diff_prep(xl, row2(g_pre_mix[1]), ml[0], ml[1], w_qkv, cc, sa, sb, t, 512, t // 512)
    kv_x = _norm_proj(xc, row2(g_pre_mix[1]), mx[0], mx[1], w_qkv[:, width:], nx, 512, BF16, "diff_kv_ctx")
    lamvec = jnp.zeros((8, DIFF_DIM), F32).at[0].set(diff_lq1[0]).at[1].set(diff_lk1[0]) \
        .at[2].set(diff_lq2[0]).at[3].set(diff_lk2[0])
    a1 = _diff_attention(q1, kv_x, k1, kv_x, v1, lamvec, row2(diff_g_out[0]), bsz, min(t, ATTN_ROWS), ATTN_SUB,
                         lam_init)
    xl = _out_proj_c(a1, xl, ml[2], row2(g_post_mix[1]), w_out_c[0].astype(BF16), t, 512)

    xl = _moe(xl, row2(g_pre_ffn[1]), ml[3], ml[4], ml[5], row2(g_post_ffn[1]), moe_w_router[0],
              moe_w_gate[0].astype(BF16), moe_w_up[0].astype(BF16), moe_w_down[0].astype(BF16), t)
    return xl.reshape(bsz, t, d)
```

```python
import functools
import math

import jax
import jax.numpy as jnp
from jax import lax
from jax.experimental import pallas as pl
from jax.experimental.pallas import tpu as pltpu

F32 = jnp.float32
BF16 = jnp.bfloat16
I32 = jnp.int32

EPS = 1e-6
ROPE_BASE = 10000.0
GRID_W = 64

D_MODEL = 1024
MLA_HEADS = 4
MLA_Q_RANK = 256
MLA_KV_RANK = 128
MLA_NOPE = 128
MLA_ROPE = 64
MLA_V = 128
GLA_HEADS = 4
GLA_DK = 64
GLA_DV = 128
GLA_RANK = 16
GLA_TAU = 16.0
DIFF_HEADS = 8
DIFF_DIM = 64
N_EXPERTS = 8
LANES = 128
GLA_CHUNK = 128
GLA_LEVELS = (64, 32, 16, 8, 4, 2, 1)
MOE_ROWS = 512
ATTN_ROWS = 2048
ATTN_SUB = 256
VMEM_LIMIT = 56 * 1024 * 1024

LOG2E = math.log2(math.e)
_NT = (((1,), (1,)), ((), ()))


def _cparams(sem):
    return pltpu.CompilerParams(dimension_semantics=sem, vmem_limit_bytes=VMEM_LIMIT)


def _rms(xf, g):
    return xf * lax.rsqrt(jnp.mean(xf * xf, axis=-1, keepdims=True) + EPS) * g


def _normmod(x, g, shift, scale):
    return _rms(x.astype(F32), g) * (1.0 + scale) + shift


def _silu(x):
    return x * (1.0 / (1.0 + jnp.exp(-x)))


def _const_spec(shape):
    n = len(shape)
    return pl.BlockSpec(shape, lambda *_: (0,) * n)


def _mod_kernel(c_ref, w_ref, b_ref, o_ref):
    s = _silu(c_ref[...])
    o_ref[0] = jnp.dot(s.astype(BF16), w_ref[0].astype(BF16), preferred_element_type=F32) + b_ref[0]


def _modulation(c_all, w_mod, b_mod):
    depth, d, n6 = w_mod.shape
    rows = c_all.shape[0]
    tn = 1536
    return pl.pallas_call(
        _mod_kernel,
        grid=(depth, n6 // tn),
        in_specs=[pl.BlockSpec((rows, d), lambda i, j: (0, 0)),
                  pl.BlockSpec((1, d, tn), lambda i, j: (i, 0, j)),
                  pl.BlockSpec((1, 1, tn), lambda i, j: (i, 0, j))],
        out_specs=pl.BlockSpec((1, rows, tn), lambda i, j: (i, 0, j)),
        out_shape=jax.ShapeDtypeStruct((depth, rows, n6), F32),
        compiler_params=_cparams(("arbitrary", "arbitrary")),
        name="modulation",
    )(c_all, w_mod, b_mod.reshape(depth, 1, n6))


def _proj_kernel(x_ref, g_ref, sh_ref, sc_ref, w_ref, o_ref):
    h = _normmod(x_ref[...], g_ref[...], sh_ref[0], sc_ref[0])
    o_ref[...] = jnp.dot(h.astype(BF16), w_ref[...], preferred_element_type=F32).astype(o_ref.dtype)


def _norm_proj(x, g, shift, scale, w, rows_per_mod, tm, out_dtype, name):
    n, d = x.shape
    nout = w.shape[1]
    mod_idx = lambda i: ((i * tm) // rows_per_mod, 0, 0)
    return pl.pallas_call(
        _proj_kernel,
        grid=(n // tm,),
        in_specs=[pl.BlockSpec((tm, d), lambda i: (i, 0)),
                  _const_spec((1, d)),
                  pl.BlockSpec((1, 1, d), mod_idx),
                  pl.BlockSpec((1, 1, d), mod_idx),
                  _const_spec((d, nout))],
        out_specs=pl.BlockSpec((tm, nout), lambda i: (i, 0)),
        out_shape=jax.ShapeDtypeStruct((n, nout), out_dtype),
        compiler_params=_cparams(("arbitrary",)),
        name=name,
    )(x, g, shift, scale, w)


def _mla_prep_kernel(cq_ref, ckv_ref, kr_ref, gq_ref, gkv_ref, wq_ref, wkv_ref, c2_ref, s2_ref,
                     q_ref, k_ref, v_ref, *, scale):
    c2 = c2_ref[...]
    s2 = s2_ref[...]

    def rope2(v):
        return v * c2 + pltpu.roll(v, 64, 1) * s2

    q = jnp.dot(_rms(cq_ref[...], gq_ref[...]).astype(BF16), wq_ref[...], preferred_element_type=F32)
    kv = jnp.dot(_rms(ckv_ref[...], gkv_ref[...]).astype(BF16), wkv_ref[...], preferred_element_type=F32)
    krope = rope2(kr_ref[...]).astype(BF16)
    for h in range(MLA_HEADS):
        b = h * 2 * LANES
        q_ref[:, b:b + LANES] = (q[:, b:b + LANES] * scale).astype(BF16)
        q_ref[:, b + LANES:b + 2 * LANES] = (rope2(q[:, b + LANES:b + 2 * LANES]) * scale).astype(BF16)
        k_ref[:, b:b + LANES] = kv[:, h * LANES:(h + 1) * LANES].astype(BF16)
        k_ref[:, b + LANES:b + 2 * LANES] = krope
    v_ref[...] = kv[:, MLA_HEADS * LANES:].astype(BF16)


def _mla_prep(proj, gq, gkv, wq, wkv, c2, s2, tm, rope_blocks):
    n = proj.shape[0]
    hq = MLA_HEADS * 2 * LANES
    rope_idx = (lambda i: (i % rope_blocks, 0)) if rope_blocks else (lambda i: (0, 0))
    return pl.pallas_call(
        functools.partial(_mla_prep_kernel, scale=float((MLA_NOPE + MLA_ROPE) ** -0.5) * LOG2E),
        grid=(n // tm,),
        in_specs=[pl.BlockSpec((tm, 256), lambda i: (i, 0)),
                  pl.BlockSpec((tm, 128), lambda i: (i, 2)),
                  pl.BlockSpec((tm, 128), lambda i: (i, 3)),
                  _const_spec((1, MLA_Q_RANK)), _const_spec((1, MLA_KV_RANK)),
                  _const_spec(wq.shape), _const_spec(wkv.shape),
                  pl.BlockSpec((tm, LANES), rope_idx), pl.BlockSpec((tm, LANES), rope_idx)],
        out_specs=[pl.BlockSpec((tm, hq), lambda i: (i, 0)),
                   pl.BlockSpec((tm, hq), lambda i: (i, 0)),
                   pl.BlockSpec((tm, MLA_HEADS * MLA_V), lambda i: (i, 0))],
        out_shape=[jax.ShapeDtypeStruct((n, hq), BF16), jax.ShapeDtypeStruct((n, hq), BF16),
                   jax.ShapeDtypeStruct((n, MLA_HEADS * MLA_V), BF16)],
        compiler_params=_cparams(("arbitrary",)),
        name="mla_prep",
    )(proj, proj, proj, gq, gkv, wq, wkv, c2, s2)


def _fill_kv(k_refs, v_refs, k_scr, v_scr):
    r0 = 0
    dv = v_refs[0].shape[1]
    for k_ref, v_ref in zip(k_refs, v_refs):
        r1 = r0 + k_ref.shape[0]
        k_scr[r0:r1, :] = k_ref[...]
        v_scr[r0:r1, 0:dv] = v_ref[...]
        r0 = r1
    v_scr[:, dv:] = jnp.ones((v_scr.shape[0], v_scr.shape[1] - dv), v_scr.dtype)


def _softmax_pv(q, k, v1, dv):
    s = lax.dot_general(q, k, _NT, preferred_element_type=F32)
    p = jnp.exp2(s - jnp.max(s, axis=-1, keepdims=True)).astype(BF16)
    o = jnp.dot(p, v1, preferred_element_type=F32)
    return o[:, 0:dv] / o[:, dv:dv + 1]


def _attn_kernel(*refs, n_src, sub):
    q_ref = refs[0]
    k_refs = refs[1:1 + n_src]
    v_refs = refs[1 + n_src:1 + 2 * n_src]
    o_ref, k_scr, v_scr = refs[1 + 2 * n_src:]

    @pl.when(pl.program_id(2) == 0)
    def _():
        _fill_kv(k_refs, v_refs, k_scr, v_scr)

    k = k_scr[...]
    v1 = v_scr[...]
    dv = o_ref.shape[1]
    for r0 in range(0, q_ref.shape[0], sub):
        o_ref[r0:r0 + sub, :] = _softmax_pv(q_ref[r0:r0 + sub, :], k, v1, dv).astype(o_ref.dtype)


def _attention(q, ks, vs, batch, heads, dq, dv, tq, sub, name):
    nq = q.shape[0] // batch // tq
    n_src = len(ks)
    tk = sum(k.shape[0] for k in ks) // batch
    in_specs = [pl.BlockSpec((tq, dq), lambda b, h, i: (b * nq + i, h))]
    for k in ks:
        in_specs.append(pl.BlockSpec((k.shape[0] // batch, dq), lambda b, h, i: (b, h)))
    for v in vs:
        in_specs.append(pl.BlockSpec((v.shape[0] // batch, dv), lambda b, h, i: (b, h)))
    return pl.pallas_call(
        functools.partial(_attn_kernel, n_src=n_src, sub=sub),
        grid=(batch, heads, nq),
        in_specs=in_specs,
        out_specs=pl.BlockSpec((tq, dv), lambda b, h, i: (b * nq + i, h)),
        out_shape=jax.ShapeDtypeStruct((q.shape[0], heads * dv), BF16),
        scratch_shapes=[pltpu.VMEM((tk, dq), BF16), pltpu.VMEM((tk, 2 * dv), BF16)],
        compiler_params=_cparams(("arbitrary", "arbitrary", "arbitrary")),
        name=name,
    )(q, *ks, *vs)


def _log_sigmoid(z):
    return jnp.minimum(z, 0.0) - jnp.log(1.0 + jnp.exp(-jnp.abs(z)))


def _gla_anchor(cum, level, rev):
    c, hk = cum.shape
    two = 2 * level
    a = level if rev else level - 1
    if two >= 8:
        return jnp.concatenate(
            [jnp.broadcast_to(cum[b * two + a:b * two + a + 1], (two, hk)) for b in range(c // two)], axis=0)
    pos = lax.broadcasted_iota(I32, cum.shape, 0) & (two - 1)
    anc = cum
    for p in range(two):
        if p != a:
            anc = jnp.where(pos == p, pltpu.roll(cum, (p - a) % c, 0), anc)
    return anc


def _gla_chunks(streams):
    c = GLA_CHUNK
    hk = GLA_HEADS * GLA_DK
    ri = lax.broadcasted_iota(I32, (c, c), 0)
    ci = lax.broadcasted_iota(I32, (c, c), 1)
    row = lax.broadcasted_iota(I32, (c, hk), 0)
    pair_lane = lax.broadcasted_iota(I32, (c, LANES), 1)
    keep_first = jnp.where(pair_lane < GLA_DK, 1.0, 0.0).astype(BF16)
    keep_second = jnp.where(pair_lane < GLA_DK, 0.0, 1.0).astype(BF16)

    def head_grams(zb, keys):
        outs = []
        for p in range(hk // LANES):
            kp = keys[:, p * LANES:(p + 1) * LANES]
            rhs = jnp.concatenate([kp * keep_first, kp * keep_second], axis=0)
            outs.append(lax.dot_general(zb[:, p * LANES:(p + 1) * LANES], rhs, _NT, preferred_element_type=F32))
        return jnp.concatenate(outs, axis=1)

    def cumulative(la, rev):
        tri = jnp.where((ci >= ri) if rev else (ci <= ri), 1.0, 0.0).astype(BF16)
        la_hi = la.astype(BF16)
        rest = la - la_hi.astype(F32)
        la_mid = rest.astype(BF16)
        la_lo = (rest - la_mid.astype(F32)).astype(BF16)
        return (jnp.dot(tri, la_hi, preferred_element_type=F32) + jnp.dot(tri, la_mid, preferred_element_type=F32)
                + jnp.dot(tri, la_lo, preferred_element_type=F32))

    cums = [cumulative(la, rev) for (_, _, _, la, _, _, rev) in streams]
    qss = [q * (GLA_DK ** -0.5) for (q, *_) in streams]
    vbs = [v.astype(BF16) for (_, _, v, *_) in streams]

    o_inter = []
    for (q, k, v, la, s_ref, lvl, rev), cum, qs in zip(streams, cums, qss):
        tot = cum[0:1] if rev else cum[c - 1:c]
        s_old = s_ref[...]
        qh = (qs * jnp.exp(cum)).astype(BF16)
        o_inter.append(lax.dot_general(qh, s_old.astype(BF16), _NT, preferred_element_type=F32))
        kh = (k * jnp.exp(tot - cum)).astype(BF16)
        u = jnp.dot(v.T.astype(BF16), kh, preferred_element_type=F32)
        bd = ((lax.broadcasted_iota(I32, u.shape, 0) // GLA_DV)
              == (lax.broadcasted_iota(I32, u.shape, 1) // GLA_DK))
        s_ref[...] = s_old * jnp.exp(tot) + jnp.where(bd, u, 0.0)

    ones = jnp.ones((c, hk), BF16)
    atts = [jnp.where(lvl == 0, head_grams((qs * k).astype(BF16), ones), 0.0)
            for (q, k, v, la, s_ref, lvl, rev), qs in zip(streams, qss)]
    for level in GLA_LEVELS:
        upper = (row & level) != 0
        for i, ((q, k, v, la, s_ref, lvl, rev), cum, qs) in enumerate(zip(streams, cums, qss)):
            is_q = jnp.logical_not(upper) if rev else upper
            x = cum - _gla_anchor(cum, level, rev)
            zb = (jnp.where(is_q, qs, k) * jnp.exp(jnp.where(is_q, x, -x))).astype(BF16)
            atts[i] = jnp.where(lvl == level, head_grams(zb, zb), atts[i])
    outs = []
    for att, vb, oi in zip(atts, vbs, o_inter):
        attb = att.astype(BF16)
        outs.append(oi + jnp.concatenate(
            [jnp.dot(attb[:, h * c:(h + 1) * c], vb[:, h * GLA_DV:(h + 1) * GLA_DV], preferred_element_type=F32)
             for h in range(GLA_HEADS)], axis=1))
    return outs


def _gla_kernel(qf_ref, kf_ref, vf_ref, gf_ref, qb_ref, kb_ref, vb_ref, gb_ref,
                wf_ref, bf_ref, wb_ref, bb_ref, lvlf_ref, lvlb_ref, s0f_ref, s0b_ref,
                of_ref, ob_ref, sf_ref, sb_ref, stf, stb):
    j = pl.program_id(1)

    @pl.when(j == 0)
    def _():
        stf[...] = s0f_ref[0]
        stb[...] = s0b_ref[0]

    def log_decay(g_ref, w_ref, b_ref):
        z = jnp.dot(g_ref[...].astype(BF16), w_ref[...], preferred_element_type=F32) + b_ref[...]
        return _log_sigmoid(z) * (1.0 / GLA_TAU)

    o_f, o_b = _gla_chunks([
        (qf_ref[...], kf_ref[...], vf_ref[...], log_decay(gf_ref, wf_ref, bf_ref), stf, lvlf_ref[...], False),
        (qb_ref[...], kb_ref[...], vb_ref[...], log_decay(gb_ref, wb_ref, bb_ref), stb, lvlb_ref[...], True)])
    of_ref[...] = o_f
    ob_ref[...] = o_b

    @pl.when(j == pl.num_programs(1) - 1)
    def _():
        sf_ref[0] = stf[...]
        sb_ref[0] = stb[...]


def _gla_level_maps():
    c = GLA_CHUNK
    t = jnp.arange(c, dtype=I32)[:, None]
    s = jnp.arange(c, dtype=I32)[None, :]
    diff = t ^ s
    top = jnp.zeros((c, c), I32)
    for level in GLA_LEVELS:
        top = jnp.where((top == 0) & ((diff & level) != 0), level, top)
    fwd = jnp.where(t == s, 0, jnp.where(t > s, top, -1))
    bwd = jnp.where(t == s, 0, jnp.where(t < s, top, -1))
    return jnp.tile(fwd, (1, GLA_HEADS)), jnp.tile(bwd, (1, GLA_HEADS))


def _gla_scan(proj, wf, bf, wb, bb, lvl_f, lvl_b, s0f, s0b, batch):
    n = proj.shape[0]
    c = GLA_CHUNK
    nch = n // batch // c
    hk = GLA_HEADS * GLA_DK
    hv = GLA_HEADS * GLA_DV
    fwd = lambda b, j: b * nch + j
    bwd = lambda b, j: b * nch + (nch - 1 - j)

    def specs(row):
        return [pl.BlockSpec((c, hk), lambda b, j: (row(b, j), 2)),
                pl.BlockSpec((c, hk), lambda b, j: (row(b, j), 3)),
                pl.BlockSpec((c, hv), lambda b, j: (row(b, j), 2)),
                pl.BlockSpec((c, LANES), lambda b, j: (row(b, j), 16))]

    st_spec = pl.BlockSpec((1, hv, hk), lambda b, j: (b, 0, 0))
    return pl.pallas_call(
        _gla_kernel,
        grid=(batch, nch),
        in_specs=specs(fwd) + specs(bwd) + [
            _const_spec((LANES, hk)), _const_spec((1, hk)), _const_spec((LANES, hk)), _const_spec((1, hk)),
            _const_spec(lvl_f.shape), _const_spec(lvl_b.shape), st_spec, st_spec],
        out_specs=[pl.BlockSpec((c, hv), lambda b, j: (fwd(b, j), 0)),
                   pl.BlockSpec((c, hv), lambda b, j: (bwd(b, j), 0)),
                   st_spec, st_spec],
        out_shape=[jax.ShapeDtypeStruct((n, hv), F32), jax.ShapeDtypeStruct((n, hv), F32),
                   jax.ShapeDtypeStruct((batch, hv, hk), F32), jax.ShapeDtypeStruct((batch, hv, hk), F32)],
        scratch_shapes=[pltpu.VMEM((hv, hk), F32), pltpu.VMEM((hv, hk), F32)],
        compiler_params=_cparams(("arbitrary", "arbitrary")),
        name="gla_scan",
    )(proj, proj, proj, proj, proj, proj, proj, proj, wf, bf, wb, bb, lvl_f, lvl_b, s0f, s0b)


def _mla_gla_kernel(q_ref, kx_ref, kl_ref, vx_ref, vl_ref,
                    qf_ref, kf_ref, vf_ref, gf_ref, qb_ref, kb_ref, vb_ref, gb_ref,
                    wf_ref, bf_ref, wb_ref, bb_ref, lvlf_ref, lvlb_ref, s0f_ref, s0b_ref,
                    a_ref, of_ref, ob_ref, k_scr, v_scr, stf, stb, *, sub):
    @pl.when(pl.program_id(1) == 0)
    def _():
        stf[...] = s0f_ref[0]
        stb[...] = s0b_ref[0]

    _fill_kv((kx_ref, kl_ref), (vx_ref, vl_ref), k_scr, v_scr)
    k = k_scr[...]
    v1 = v_scr[...]
    dv = a_ref.shape[1]
    lvl_f = lvlf_ref[...]
    lvl_b = lvlb_ref[...]
    c = GLA_CHUNK
    groups = qf_ref.shape[0] // c
    chains = q_ref.shape[0] // sub // groups

    def log_decay(g, w_ref, b_ref):
        z = jnp.dot(g.astype(BF16), w_ref[...], preferred_element_type=F32) + b_ref[...]
        return _log_sigmoid(z) * (1.0 / GLA_TAU)

    for g in range(groups):
        for r0 in range(g * chains * sub, (g + 1) * chains * sub, sub):
            a_ref[r0:r0 + sub, :] = _softmax_pv(q_ref[r0:r0 + sub, :], k, v1, dv).astype(a_ref.dtype)
        rf = slice(g * c, (g + 1) * c)
        rb = slice((groups - 1 - g) * c, (groups - g) * c)
        o_f, o_b = _gla_chunks([
            (qf_ref[rf, :], kf_ref[rf, :], vf_ref[rf, :], log_decay(gf_ref[rf, :], wf_ref, bf_ref),
             stf, lvl_f, False),
            (qb_ref[rb, :], kb_ref[rb, :], vb_ref[rb, :], log_decay(gb_ref[rb, :], wb_ref, bb_ref),
             stb, lvl_b, True)])
        of_ref[rf, :] = o_f
        ob_ref[rb, :] = o_b


def _mla_gla(q, kx, kl, vx, vl, proj, wf, bf, wb, bb, lvl_f, lvl_b, s0f, s0b, batch, sub):
    n = q.shape[0]
    heads = MLA_HEADS
    t = n // batch
    tx = kx.shape[0] // batch
    dq, dv = 2 * LANES, MLA_V
    hk = GLA_HEADS * GLA_DK
    hv = GLA_HEADS * GLA_DV
    rows = t // heads
    fwd = lambda b, h: b * heads + h
    bwd = lambda b, h: b * heads + (heads - 1 - h)

    def scan_specs(row):
        return [pl.BlockSpec((rows, hk), lambda b, h: (row(b, h), 2)),
                pl.BlockSpec((rows, hk), lambda b, h: (row(b, h), 3)),
                pl.BlockSpec((rows, hv), lambda b, h: (row(b, h), 2)),
                pl.BlockSpec((rows, LANES), lambda b, h: (row(b, h), 16))]

    st_spec = pl.BlockSpec((1, hv, hk), lambda b, h: (b, 0, 0))
    return pl.pallas_call(
        functools.partial(_mla_gla_kernel, sub=sub),
        grid=(batch, heads),
        in_specs=[pl.BlockSpec((t, dq), lambda b, h: (b, h)),
                  pl.BlockSpec((tx, dq), lambda b, h: (b, h)),
                  pl.BlockSpec((t, dq), lambda b, h: (b, h)),
                  pl.BlockSpec((tx, dv), lambda b, h: (b, h)),
                  pl.BlockSpec((t, dv), lambda b, h: (b, h))]
        + scan_specs(fwd) + scan_specs(bwd) + [
            _const_spec((LANES, hk)), _const_spec((1, hk)), _const_spec((LANES, hk)), _const_spec((1, hk)),
            _const_spec(lvl_f.shape), _const_spec(lvl_b.shape), st_spec, st_spec],
        out_specs=[pl.BlockSpec((t, dv), lambda b, h: (b, h)),
                   pl.BlockSpec((rows, hv), lambda b, h: (fwd(b, h), 0)),
                   pl.BlockSpec((rows, hv), lambda b, h: (bwd(b, h), 0))],
        out_shape=[jax.ShapeDtypeStruct((n, heads * dv), BF16),
                   jax.ShapeDtypeStruct((n, hv), F32), jax.ShapeDtypeStruct((n, hv), F32)],
        scratch_shapes=[pltpu.VMEM((tx + t, dq), BF16), pltpu.VMEM((tx + t, 2 * dv), BF16),
                        pltpu.VMEM((hv, hk), F32), pltpu.VMEM((hv, hk), F32)],
        compiler_params=_cparams(("arbitrary", "arbitrary")),
        name="mla_gla",
    )(q, kx, kl, vx, vl, proj, proj, proj, proj, proj, proj, proj, proj, wf, bf, wb, bb, lvl_f, lvl_b, s0f, s0b)


def _swiglu_residual(x, h, gate, g_post, wg_ref, wu_ref, wd_ref, fc):
    acc = jnp.zeros(x.shape, F32)
    for c0 in range(0, wg_ref.shape[1], fc):
        g = jnp.dot(h, wg_ref[:, c0:c0 + fc], preferred_element_type=F32)
        u = jnp.dot(h, wu_ref[:, c0:c0 + fc], preferred_element_type=F32)
        act = (_silu(g) * u).astype(BF16)
        acc = acc + jnp.dot(act, wd_ref[c0:c0 + fc, :], preferred_element_type=F32)
    return x + gate * _rms(acc, g_post)


def _mix_ffn_kernel(a_ref, of_ref, ob_ref, r_ref, go_ref, x_ref, mod_ref, gpm_ref, wo_ref,
                    gpre_ref, gpost_ref, wg_ref, wu_ref, wd_ref, o_ref, *, fc):
    o = of_ref[...] + ob_ref[...]
    r = r_ref[...]
    go = go_ref[...]
    parts = []
    for h in range(GLA_HEADS):
        sl = slice(h * GLA_DV, (h + 1) * GLA_DV)
        parts.append(_rms(o[:, sl], go) * _silu(r[:, sl]))
    g = jnp.concatenate(parts, axis=-1).astype(BF16)
    na = a_ref.shape[1]
    y = (jnp.dot(a_ref[...], wo_ref[0:na, :], preferred_element_type=F32)
         + jnp.dot(g, wo_ref[na:, :], preferred_element_type=F32))
    mod = mod_ref[0]
    x = x_ref[...] + mod[0:1] * _rms(y, gpm_ref[...])
    h = _normmod(x, gpre_ref[...], mod[1:2], mod[2:3]).astype(BF16)
    o_ref[...] = _swiglu_residual(x, h, mod[3:4], gpost_ref[...], wg_ref, wu_ref, wd_ref, fc)


def _mix_ffn(a, o_f, o_b, proj, g_o, x, mod, g_post_mix, w_out, g_pre, g_post, wg, wu, wd, rows_per_mod, tm):
    n, d = x.shape
    hv = GLA_HEADS * GLA_DV
    row = lambda i: (i, 0)
    single = pl.Buffered(1)
    resident = lambda w: pl.BlockSpec(w.shape, lambda i: (0, 0), pipeline_mode=single)
    return pl.pallas_call(
        functools.partial(_mix_ffn_kernel, fc=256),
        grid=(n // tm,),
        in_specs=[pl.BlockSpec((tm, a.shape[1]), row),
                  pl.BlockSpec((tm, hv), row),
                  pl.BlockSpec((tm, hv), row),
                  pl.BlockSpec((tm, hv), lambda i: (i, 3)),
                  _const_spec((1, GLA_DV)),
                  pl.BlockSpec((tm, d), row),
                  pl.BlockSpec((1, mod.shape[1], d), lambda i: ((i * tm) // rows_per_mod, 0, 0)),
                  _const_spec((1, d)), resident(w_out),
                  _const_spec((1, d)), _const_spec((1, d)), resident(wg), resident(wu), resident(wd)],
        out_specs=pl.BlockSpec((tm, d), row),
        out_shape=jax.ShapeDtypeStruct((n, d), F32),
        compiler_params=_cparams(("arbitrary",)),
        name="mix_ffn",
    )(a, o_f, o_b, proj, g_o, x, mod, g_post_mix, w_out, g_pre, g_post, wg, wu, wd)


def _diff_prep_kernel(x_ref, g_ref, sh_ref, sc_ref, w_ref, c_ref, sa_ref, sb_ref, q_ref, k_ref, v_ref):
    h = _normmod(x_ref[...], g_ref[...], sh_ref[0], sc_ref[0]).astype(BF16)
    qkv = jnp.dot(h, w_ref[...], preferred_element_type=F32)
    cc, sa, sb = c_ref[...], sa_ref[...], sb_ref[...]
    width = q_ref.shape[1]

    def rope(v):
        return v * cc + pltpu.roll(v, 96, 1) * sa + pltpu.roll(v, 32, 1) * sb

    for j in range(width // LANES):
        sl = slice(j * LANES, (j + 1) * LANES)
        q_ref[:, sl] = (rope(qkv[:, sl]) * (DIFF_DIM ** -0.5 * LOG2E)).astype(BF16)
        k_ref[:, sl] = rope(qkv[:, width + j * LANES:width + (j + 1) * LANES]).astype(BF16)
    v_ref[...] = qkv[:, 2 * width:].astype(BF16)


def _diff_prep(x, g, shift, scale, w, cc, sa, sb, rows_per_mod, tm, rope_blocks):
    n, d = x.shape
    width = w.shape[1] // 3
    mod_idx = lambda i: ((i * tm) // rows_per_mod, 0, 0)
    rope_spec = pl.BlockSpec((tm, LANES), lambda i: (i % rope_blocks, 0))
    out_spec = pl.BlockSpec((tm, width), lambda i: (i, 0))
    return pl.pallas_call(
        _diff_prep_kernel,
        grid=(n // tm,),
        in_specs=[pl.BlockSpec((tm, d), lambda i: (i, 0)), _const_spec((1, d)),
                  pl.BlockSpec((1, 1, d), mod_idx), pl.BlockSpec((1, 1, d), mod_idx),
                  _const_spec(w.shape), rope_spec, rope_spec, rope_spec],
        out_specs=[out_spec, out_spec, out_spec],
        out_shape=[jax.ShapeDtypeStruct((n, width), BF16)] * 3,
        compiler_params=_cparams(("arbitrary",)),
        name="diff_prep",
    )(x, g, shift, scale, w, cc, sa, sb)


def _diff_attn_kernel(q_ref, kx_ref, kl_ref, vx_ref, vl_ref, lam_ref, go_ref, o_ref, k_scr, v_scr, *,
                      lam_init, sub):
    @pl.when(pl.program_id(2) == 0)
    def _():
        _fill_kv((kx_ref, kl_ref), (vx_ref, vl_ref), k_scr, v_scr)

    lv = lam_ref[...]
    lam = (jnp.exp(jnp.sum(lv[0:1] * lv[1:2], axis=-1, keepdims=True))
           - jnp.exp(jnp.sum(lv[2:3] * lv[3:4], axis=-1, keepdims=True)) + lam_init)
    k = k_scr[...]
    v1 = v_scr[...]
    go = go_ref[...]
    lane = lax.broadcasted_iota(I32, (sub, LANES), 1)
    for r0 in range(0, q_ref.shape[0], sub):
        q = q_ref[r0:r0 + sub, :]
        zero = jnp.zeros_like(q)
        o = (_softmax_pv(jnp.where(lane < DIFF_DIM, q, zero), k, v1, LANES)
             - lam * _softmax_pv(jnp.where(lane >= DIFF_DIM, q, zero), k, v1, LANES))
        o_ref[r0:r0 + sub, :] = (_rms(o, go) * (1.0 - lam_init)).astype(o_ref.dtype)


def _diff_attention(q, kx, kl, vx, vl, lamvec, g_o, batch, tq, sub, lam_init):
    n = q.shape[0]
    nq = n // batch // tq
    tx = kx.shape[0] // batch
    tl = kl.shape[0] // batch
    return pl.pallas_call(
        functools.partial(_diff_attn_kernel, lam_init=lam_init, sub=sub),
        grid=(batch, DIFF_HEADS, nq),
        in_specs=[pl.BlockSpec((tq, LANES), lambda b, h, i: (b * nq + i, h)),
                  pl.BlockSpec((tx, LANES), lambda b, h, i: (b, h)),
                  pl.BlockSpec((tl, LANES), lambda b, h, i: (b, h)),
                  pl.BlockSpec((tx, LANES), lambda b, h, i: (b, DIFF_HEADS + h)),
                  pl.BlockSpec((tl, LANES), lambda b, h, i: (b, h)),
                  _const_spec(lamvec.shape), _const_spec((1, LANES))],
        out_specs=pl.BlockSpec((tq, LANES), lambda b, h, i: (b * nq + i, h)),
        out_shape=jax.ShapeDtypeStruct((n, DIFF_HEADS * LANES), BF16),
        scratch_shapes=[pltpu.VMEM((tx + tl, LANES), BF16), pltpu.VMEM((tx + tl, 2 * LANES), BF16)],
        compiler_params=_cparams(("arbitrary", "arbitrary", "arbitrary")),
        name="diff_attention",
    )(q, kx, kl, vx, vl, lamvec, g_o)


def _out_c_kernel(a_ref, x_ref, gate_ref, gp_ref, w_ref, o_ref):
    y = jnp.dot(a_ref[...], w_ref[...], preferred_element_type=F32)
    o_ref[...] = x_ref[...] + gate_ref[0] * _rms(y, gp_ref[...])


def _out_proj_c(a, x, gate, g_post, w, rows_per_mod, tm):
    n, d = x.shape
    return pl.pallas_call(
        _out_c_kernel,
        grid=(n // tm,),
        in_specs=[pl.BlockSpec((tm, a.shape[1]), lambda i: (i, 0)),
                  pl.BlockSpec((tm, d), lambda i: (i, 0)),
                  pl.BlockSpec((1, 1, d), lambda i: ((i * tm) // rows_per_mod, 0, 0)),
                  _const_spec((1, d)),
                  _const_spec(w.shape)],
        out_specs=pl.BlockSpec((tm, d), lambda i: (i, 0)),
        out_shape=jax.ShapeDtypeStruct((n, d), F32),
        compiler_params=_cparams(("arbitrary",)),
        name="out_proj_c",
    )(a, x, gate, g_post, w)


def _router_kernel(x_ref, g_ref, sh_ref, sc_ref, wr_ref, h_ref, meta_ref, gates_ref, cnt_ref, carry_ref):
    i = pl.program_id(0)

    @pl.when(i == 0)
    def _():
        carry_ref[...] = jnp.zeros_like(carry_ref)

    h = _normmod(x_ref[...], g_ref[...], sh_ref[0], sc_ref[0])
    h_ref[...] = h
    tm = h.shape[0]
    logits = jnp.dot(h, wr_ref[...], precision=lax.Precision.HIGHEST, preferred_element_type=F32)
    lane = lax.broadcasted_iota(I32, logits.shape, 1).astype(F32)
    neg = jnp.float32(-jnp.inf)
    logits = jnp.where(lane < N_EXPERTS, logits, neg)
    m0 = jnp.max(logits, axis=-1, keepdims=True)
    i0 = jnp.min(jnp.where(logits == m0, lane, float(LANES)), axis=-1, keepdims=True)
    rest = jnp.where(lane == i0, neg, logits)
    m1 = jnp.max(rest, axis=-1, keepdims=True)
    i1 = jnp.min(jnp.where(rest == m1, lane, float(LANES)), axis=-1, keepdims=True)
    e = jnp.exp(m1 - m0)
    g0 = 1.0 / (1.0 + e)
    g1 = e / (1.0 + e)
    hit = jnp.where(lane == i0, 1.0, jnp.where(lane == i1, 1.0, 0.0)).astype(F32)
    ri = lax.broadcasted_iota(I32, (tm, tm), 0)
    ci = lax.broadcasted_iota(I32, (tm, tm), 1)
    below = jnp.where(ci < ri, 1.0, 0.0).astype(BF16)
    prefix = jnp.dot(below, hit.astype(BF16), preferred_element_type=F32) + carry_ref[...]
    r0 = jnp.sum(jnp.where(lane == i0, prefix, 0.0), axis=-1, keepdims=True)
    r1 = jnp.sum(jnp.where(lane == i1, prefix, 0.0), axis=-1, keepdims=True)
    carry_ref[...] = carry_ref[...] + jnp.sum(hit, axis=0, keepdims=True)
    meta = jnp.where(lane == 0.0, i0, jnp.where(lane == 1.0, i1, jnp.where(lane == 2.0, r0, r1)))
    meta_ref[...] = meta.astype(I32)
    gates_ref[...] = jnp.where(lane == 0.0, g0, g1)
    cnt_ref[...] = carry_ref[...]


def _router(x, g, shift, scale, w_router, rows_per_mod, tm):
    n, d = x.shape
    mod_idx = lambda i: ((i * tm) // rows_per_mod, 0, 0)
    row = lambda i: (i, 0)
    return pl.pallas_call(
        _router_kernel,
        grid=(n // tm,),
        in_specs=[pl.BlockSpec((tm, d), row), _const_spec((1, d)),
                  pl.BlockSpec((1, 1, d), mod_idx), pl.BlockSpec((1, 1, d), mod_idx),
                  _const_spec(w_router.shape)],
        out_specs=[pl.BlockSpec((tm, d), row), pl.BlockSpec((tm, LANES), row), pl.BlockSpec((tm, LANES), row),
                   _const_spec((1, LANES))],
        out_shape=[jax.ShapeDtypeStruct((n, d), F32), jax.ShapeDtypeStruct((n, LANES), I32),
                   jax.ShapeDtypeStruct((n, LANES), F32), jax.ShapeDtypeStruct((1, LANES), F32)],
        scratch_shapes=[pltpu.VMEM((1, LANES), F32)],
        compiler_params=_cparams(("arbitrary",)),
        name="moe_router",
    )(x, g, shift, scale, w_router)


def _expert_stream_kernel(be_ref, nb_ref, idx_in_ref, idx_out_ref, h_hbm, wg_ref, wu_ref, wd_ref, y_hbm,
                          xbuf, ybuf, sem_in, sem_out, *, sub, nb):
    j = pl.program_id(0)
    bm = xbuf.shape[1]
    nbu = nb_ref[0]
    c = j - 1
    live = (c >= 0) & (c < nbu)
    gather_ok = j < nbu
    scatter_ok = (j >= 2) & (j - 2 < nbu)
    in_slot = j % 2
    cur = (j + 1) % 2

    def start_in(r):
        t = idx_in_ref[0, 0, r]
        pltpu.make_async_copy(h_hbm.at[pl.ds(t, 1), :], xbuf.at[in_slot, pl.ds(r, 1), :],
                              sem_in.at[in_slot]).start()

    def start_out(r):
        t = idx_out_ref[0, 0, r]
        pltpu.make_async_copy(ybuf.at[in_slot, pl.ds(r, 1), :], y_hbm.at[pl.ds(t, 1), :],
                              sem_out.at[in_slot]).start()

    def wait_in(slot):
        pltpu.make_async_copy(h_hbm.at[pl.ds(0, bm), :], xbuf.at[slot], sem_in.at[slot]).wait()

    def wait_out(slot):
        pltpu.make_async_copy(ybuf.at[slot], y_hbm.at[pl.ds(0, bm), :], sem_out.at[slot]).wait()

    def loop(fn):
        def body(r, carry):
            fn(r)
            return carry
        lax.fori_loop(0, bm, body, 0, unroll=8)

    def expert(interleave):
        x = xbuf[cur].astype(BF16)
        f_dim = wg_ref.shape[2]
        n_chunks = f_dim // sub
        per = -(-bm // max(1, (2 * n_chunks) // 3))
        part = jnp.zeros((bm, wd_ref.shape[2]), F32)
        for ci in range(n_chunks):
            c0 = ci * sub
            g = jnp.dot(x, wg_ref[0, :, c0:c0 + sub], preferred_element_type=F32)
            u = jnp.dot(x, wu_ref[0, :, c0:c0 + sub], preferred_element_type=F32)
            act = (_silu(g) * u).astype(BF16)
            part = part + jnp.dot(act, wd_ref[0, c0:c0 + sub, :], preferred_element_type=F32)
            if interleave:
                for r in range(ci * per, min((ci + 1) * per, bm)):
                    start_in(r)
                    start_out(r)
        return part

    def store_result(part):
        @pl.when(j >= 3)
        def _():
            wait_out(cur)
        ybuf[cur] = part

    @pl.when(live)
    def _():
        wait_in(cur)

    steady = live & gather_ok & scatter_ok

    @pl.when(steady)
    def _():
        store_result(expert(True))

    @pl.when(jnp.logical_not(steady))
    def _():
        @pl.when(gather_ok)
        def _():
            loop(start_in)

        @pl.when(scatter_ok)
        def _():
            loop(start_out)

        @pl.when(live)
        def _():
            store_result(expert(False))

        @pl.when(jnp.logical_not(live) & (j >= 3) & (j - 3 < nbu))
        def _():
            wait_out(cur)

        @pl.when((c >= nbu) & (c < nb))
        def _():
            ybuf[cur] = jnp.zeros(ybuf.shape[1:], F32)
            fill = pltpu.make_async_copy(ybuf.at[cur], y_hbm.at[pl.ds(pl.multiple_of(c * bm, bm), bm), :],
                                         sem_out.at[cur])
            fill.start()
            fill.wait()

    @pl.when((j == nb + 1) & (nb - 1 < nbu))
    def _():
        wait_out(in_slot)


def _experts_stream(h, row_token, out_row, blk_e, nb_used, wg, wu, wd, n_out):
    n, d = h.shape
    bm = MOE_ROWS
    nb = row_token.shape[0] // bm
    single = pl.Buffered(1)

    def expert_of(j, be, nbu):
        return be[jnp.clip(j - 1, 0, nbu[0] - 1)]

    w_in_spec = pl.BlockSpec((1, d, wg.shape[2]), lambda j, be, nbu: (expert_of(j, be, nbu), 0, 0),
                             pipeline_mode=single)
    w_out_spec = pl.BlockSpec((1, wd.shape[1], d), lambda j, be, nbu: (expert_of(j, be, nbu), 0, 0),
                              pipeline_mode=single)
    return pl.pallas_call(
        functools.partial(_expert_stream_kernel, sub=256, nb=nb),
        grid_spec=pltpu.PrefetchScalarGridSpec(
            num_scalar_prefetch=2,
            grid=(nb + 2,),
            in_specs=[pl.BlockSpec((1, 1, bm), lambda j, be, nbu: (jnp.minimum(j, nb - 1), 0, 0),
                                   memory_space=pltpu.SMEM),
                      pl.BlockSpec((1, 1, bm), lambda j, be, nbu: (jnp.clip(j - 2, 0, nb - 1), 0, 0),
                                   memory_space=pltpu.SMEM),
                      pl.BlockSpec(memory_space=pl.ANY),
                      w_in_spec, w_in_spec, w_out_spec],
            out_specs=pl.BlockSpec(memory_space=pl.ANY),
            scratch_shapes=[pltpu.VMEM((2, bm, d), F32), pltpu.VMEM((2, bm, d), F32),
                            pltpu.SemaphoreType.DMA((2,)), pltpu.SemaphoreType.DMA((2,))]),
        out_shape=jax.ShapeDtypeStruct((n_out, d), F32),
        compiler_params=_cparams(("arbitrary",)),
        name="moe_experts",
    )(blk_e, nb_used, row_token.reshape(nb, 1, bm), out_row.reshape(nb, 1, bm), h, wg, wu, wd)


def _combine2_kernel(y0_ref, y1_ref, gates_ref, x_ref, gate_ref, gp_ref, o_ref):
    gt = gates_ref[...]
    f = y0_ref[...] * gt[:, 0:1] + y1_ref[...] * gt[:, 1:2]
    o_ref[...] = x_ref[...] + gate_ref[0] * _rms(f, gp_ref[...])


def _combine2(y, gates, x, gate, g_post, rows_per_mod, tm):
    n, d = x.shape
    nt = n // tm
    return pl.pallas_call(
        _combine2_kernel,
        grid=(nt,),
        in_specs=[pl.BlockSpec((tm, d), lambda i: (i, 0)),
                  pl.BlockSpec((tm, d), lambda i: (nt + i, 0)),
                  pl.BlockSpec((tm, LANES), lambda i: (i, 0)),
                  pl.BlockSpec((tm, d), lambda i: (i, 0)),
                  pl.BlockSpec((1, 1, d), lambda i: ((i * tm) // rows_per_mod, 0, 0)),
                  _const_spec((1, d))],
        out_specs=pl.BlockSpec((tm, d), lambda i: (i, 0)),
        out_shape=jax.ShapeDtypeStruct((n, d), F32),
        compiler_params=_cparams(("arbitrary",)),
        name="moe_combine",
    )(y, y, gates, x, gate, g_post)


def _moe(x, g_pre, shift, scale, gate, g_post, w_router, wg, wu, wd, rows_per_mod):
    n, d = x.shape
    bm = MOE_ROWS
    wr = jnp.zeros((d, LANES), F32).at[:, :N_EXPERTS].set(w_router)
    h, meta, gates, counts = _router(x, g_pre, shift, scale, wr, rows_per_mod, 512)
    cnt = counts[0, :N_EXPERTS].astype(I32)
    padded = (cnt + bm - 1) // bm * bm
    pend = jnp.cumsum(padded)
    pstart = pend - padded
    dest = pstart[meta[:, 0:2]] + meta[:, 2:4]
    cap = 2 * n + N_EXPERTS * bm
    nb = cap // bm
    code = 2 * jnp.arange(n, dtype=I32)[:, None] + jnp.arange(2, dtype=I32)[None, :]
    info = jnp.full((cap,), -1, I32).at[dest.reshape(-1)].set(code.reshape(-1))
    real = info >= 0
    row_token = jnp.where(real, info >> 1, 0)
    spill = 2 * n - 1 + jnp.cumsum(jnp.logical_not(real).astype(I32))
    out_row = jnp.where(real, (info & 1) * n + (info >> 1), spill)
    nb_used = (pend[-1] // bm).astype(I32).reshape(1)
    blk_start = jnp.arange(nb, dtype=I32)[:, None] * bm
    blk_e = jnp.minimum(jnp.sum((pend[None, :] <= blk_start).astype(I32), axis=1), N_EXPERTS - 1)
    y = _experts_stream(h, row_token, out_row, blk_e, nb_used, wg, wu, wd, cap)
    return _combine2(y, gates, x, gate, g_post, rows_per_mod, 512)


def _rope_angles(n_tokens, rot_dim):
    rows = n_tokens // GRID_W
    row = jnp.repeat(jnp.arange(rows, dtype=F32), GRID_W)
    col = jnp.tile(jnp.arange(GRID_W, dtype=F32), rows)
    n_freq = rot_dim // 4
    freq = ROPE_BASE ** (-jnp.arange(n_freq, dtype=F32) / n_freq)
    ang = jnp.concatenate([row[:, None] * freq, col[:, None] * freq], axis=-1)
    return jnp.cos(ang), jnp.sin(ang)


def _rot_cols(w):
    half = w.shape[-1] // 2
    return jnp.concatenate([-w[..., half:], w[..., :half]], axis=-1)


def kernel(x, c, ctx, c_ctx, w_mod, b_mod, g_pre_mix, g_post_mix, g_pre_ffn, g_post_ffn,
           w_in_a, mla_g_q, mla_w_uq, mla_g_kv, mla_w_ukv,
           gla_w_gate_f, gla_b_gate_f, gla_w_gate_b, gla_b_gate_b, gla_g_out, w_out_a,
           w_qkv_c, diff_lq1, diff_lk1, diff_lq2, diff_lk2, diff_g_out, w_out_c,
           ffn_w_gate, ffn_w_up, ffn_w_down,
           moe_w_router, moe_w_gate, moe_w_up, moe_w_down):
    bsz, t, d = x.shape
    tx = ctx.shape[1]
    n, nx = bsz * t, bsz * tx
    xl = x.reshape(n, d)
    xc = ctx.reshape(nx, d)

    rows = -(-(bsz + 1) // 8) * 8
    c_all = jnp.zeros((rows, d), F32).at[:bsz].set(c).at[bsz].set(c_ctx)
    mod_all = _modulation(c_all, w_mod, b_mod).reshape(2, rows, 6, d)

    def mods(i):
        lat = [mod_all[i, :bsz, k].reshape(bsz, 1, d) for k in range(6)]
        cx = [mod_all[i, bsz, k].reshape(1, 1, d) for k in range(6)]
        return lat, cx

    row2 = lambda v: v.reshape(1, -1)

    ml, mx = mods(0)
    wi = w_in_a[0]
    cq, ckv, kr, gq, gk, gv, gr, af, ab = jnp.split(
        wi, [256, 384, 448, 704, 960, 1472, 1984, 2000], axis=-1)
    w_in = jnp.concatenate([cq, ckv, kr, _rot_cols(kr), gq, gk, gv, gr, af, ab,
                            jnp.zeros((d, LANES - 2 * GLA_RANK), F32)], axis=-1).astype(BF16)
    uq = mla_w_uq[0].reshape(MLA_Q_RANK, MLA_HEADS, MLA_NOPE + MLA_ROPE)
    wq = jnp.concatenate([uq[..., :MLA_NOPE], uq[..., MLA_NOPE:], _rot_cols(uq[..., MLA_NOPE:])],
                         axis=-1).reshape(MLA_Q_RANK, MLA_HEADS * 2 * LANES).astype(BF16)
    ukv = mla_w_ukv[0].reshape(MLA_KV_RANK, MLA_HEADS, MLA_NOPE + MLA_V)
    wkv = jnp.concatenate([ukv[..., :MLA_NOPE].reshape(MLA_KV_RANK, -1),
                           ukv[..., MLA_NOPE:].reshape(MLA_KV_RANK, -1)], axis=-1).astype(BF16)
    cos_a, sin_a = _rope_angles(t, MLA_ROPE)
    zpad = jnp.zeros((t, LANES - MLA_ROPE), F32)
    c2 = jnp.concatenate([cos_a, cos_a, zpad], axis=-1)
    s2 = jnp.concatenate([sin_a, sin_a, zpad], axis=-1)
    lane = jnp.arange(LANES)
    c2x = jnp.broadcast_to(jnp.where(lane < MLA_ROPE, 1.0, 0.0).astype(F32), (tx, LANES))
    s2x = jnp.zeros((tx, LANES), F32)

    proj_l = _norm_proj(xl, row2(g_pre_mix[0]), ml[0], ml[1], w_in, t, 512, F32, "in_proj_a")
    proj_x = _norm_proj(xc, row2(g_pre_mix[0]), mx[0], mx[1], w_in, nx, 512, F32, "in_proj_a_ctx")
    gq_, gkv_ = row2(mla_g_q[0]), row2(mla_g_kv[0])
    q_l, k_l, v_l = _mla_prep(proj_l, gq_, gkv_, wq, wkv, c2, s2, 512, t // 512)
    q_x, k_x, v_x = _mla_prep(proj_x, gq_, gkv_, wq, wkv, c2x, s2x, tx, 0)
    a_x = _attention(q_x, [k_x], [v_x], bsz, MLA_HEADS, 2 * LANES, MLA_V, tx, tx, "mla_attention_ctx")

    hk = GLA_HEADS * GLA_DK
    hv = GLA_HEADS * GLA_DV
    wf = jnp.zeros((LANES, hk), F32).at[:GLA_RANK].set(gla_w_gate_f[0]).astype(BF16)
    wb = jnp.zeros((LANES, hk), F32).at[GLA_RANK:2 * GLA_RANK].set(gla_w_gate_b[0]).astype(BF16)
    lvl_f, lvl_b = _gla_level_maps()
    s0 = jnp.zeros((bsz, hv, hk), F32)
    gla_args = (wf, row2(gla_b_gate_f[0]), wb, row2(gla_b_gate_b[0]), lvl_f, lvl_b)
    ox_f, ox_b, sx_f, sx_b = _gla_scan(proj_x, *gla_args, s0, s0, bsz)
    a_l, ol_f, ol_b = _mla_gla(q_l, k_x, k_l, v_x, v_l, proj_l, *gla_args, sx_f, sx_b, bsz, ATTN_SUB)

    w_out = w_out_a[0].astype(BF16)
    g_o = row2(gla_g_out[0])
    wg, wu, wd = ffn_w_gate[0].astype(BF16), ffn_w_up[0].astype(BF16), ffn_w_down[0].astype(BF16)
    gpm, gpf, gqf = row2(g_post_mix[0]), row2(g_pre_ffn[0]), row2(g_post_ffn[0])
    tail_mod = lambda m: jnp.concatenate([m[2], m[3], m[4], m[5]], axis=1)
    xl = _mix_ffn(a_l, ol_f, ol_b, proj_l, g_o, xl, tail_mod(ml), gpm, w_out, gpf, gqf, wg, wu, wd, t, 512)
    xc = _mix_ffn(a_x, ox_f, ox_b, proj_x, g_o, xc, tail_mod(mx), gpm, w_out, gpf, gqf, wg, wu, wd, nx, 512)

    ml, mx = mods(1)
    lam_init = 0.8 - 0.6 * math.exp(-0.3 * 1)
    w_qkv = w_qkv_c[0].astype(BF16)
    width = DIFF_HEADS * 2 * DIFF_DIM
    cos_c, sin_c = _rope_angles(t, DIFF_DIM)
    z32 = jnp.zeros_like(sin_c)
    cc = jnp.concatenate([cos_c] * 4, axis=-1)
    sa = jnp.concatenate([-sin_c, z32, -sin_c, z32], axis=-1)
    sb = jnp.concatenate([z32, sin_c, z32, sin_c], axis=-1)
    q1, k1, v1 = _diff_prep(xl, row2(g_pre_mix[1]), ml[0], ml[1], w_qkv, cc, sa, sb, t, 512, t // 512)
    kv_x = _norm_proj(xc, row2(g_pre_mix[1]), mx[0], mx[1], w_qkv[:, width:], nx, 512, BF16, "diff_kv_ctx")
    lamvec = jnp.zeros((8, DIFF_DIM), F32).at[0].set(diff_lq1[0]).at[1].set(diff_lk1[0]) \
        .at[2].set(diff_lq2[0]).at[3].set(diff_lk2[0])
    a1 = _diff_attention(q1, kv_x, k1, kv_x, v1, lamvec, row2(diff_g_out[0]), bsz, min(t, ATTN_ROWS), ATTN_SUB,
                         lam_init)
    xl = _out_proj_c(a1, xl, ml[2], row2(g_post_mix[1]), w_out_c[0].astype(BF16), t, 512)

    xl = _moe(xl, row2(g_pre_ffn[1]), ml[3], ml[4], ml[5], row2(g_post_ffn[1]), moe_w_router[0],
              moe_w_gate[0].astype(BF16), moe_w_up[0].astype(BF16), moe_w_down[0].astype(BF16), t)
    return xl.reshape(bsz, t, d)
```

```python
import functools
import math

import jax
import jax.numpy as jnp
from jax import lax
from jax.experimental import pallas as pl
from jax.experimental.pallas import tpu as pltpu

F32 = jnp.float32
BF16 = jnp.bfloat16
I32 = jnp.int32

EPS = 1e-6
ROPE_BASE = 10000.0
GRID_W = 64

D_MODEL = 1024
MLA_HEADS = 4
MLA_Q_RANK = 256
MLA_KV_RANK = 128
MLA_NOPE = 128
MLA_ROPE = 64
MLA_V = 128
GLA_HEADS = 4
GLA_DK = 64
GLA_DV = 128
GLA_RANK = 16
GLA_TAU = 16.0
DIFF_HEADS = 8
DIFF_DIM = 64
N_EXPERTS = 8
LANES = 128
GLA_CHUNK = 128
GLA_LEVELS = (64, 32, 16, 8, 4, 2, 1)
MOE_ROWS = 512
ATTN_ROWS = 2048
ATTN_SUB = 256
VMEM_LIMIT = 56 * 1024 * 1024

LOG2E = math.log2(math.e)
_NT = (((1,), (1,)), ((), ()))


def _cparams(sem):
    return pltpu.CompilerParams(dimension_semantics=sem, vmem_limit_bytes=VMEM_LIMIT)


def _rms(xf, g):
    return xf * lax.rsqrt(jnp.mean(xf * xf, axis=-1, keepdims=True) + EPS) * g


def _normmod(x, g, shift, scale):
    return _rms(x.astype(F32), g) * (1.0 + scale) + shift


def _silu(x):
    return x * (1.0 / (1.0 + jnp.exp(-x)))


def _const_spec(shape):
    n = len(shape)
    return pl.BlockSpec(shape, lambda *_: (0,) * n)


def _mod_kernel(c_ref, w_ref, b_ref, o_ref):
    s = _silu(c_ref[...])
    o_ref[0] = jnp.dot(s.astype(BF16), w_ref[0].astype(BF16), preferred_element_type=F32) + b_ref[0]


def _modulation(c_all, w_mod, b_mod):
    depth, d, n6 = w_mod.shape
    rows = c_all.shape[0]
    tn = 1536
    return pl.pallas_call(
        _mod_kernel,
        grid=(depth, n6 // tn),
        in_specs=[pl.BlockSpec((rows, d), lambda i, j: (0, 0)),
                  pl.BlockSpec((1, d, tn), lambda i, j: (i, 0, j)),
                  pl.BlockSpec((1, 1, tn), lambda i, j: (i, 0, j))],
        out_specs=pl.BlockSpec((1, rows, tn), lambda i, j: (i, 0, j)),
        out_shape=jax.ShapeDtypeStruct((depth, rows, n6), F32),
        compiler_params=_cparams(("arbitrary", "arbitrary")),
        name="modulation",
    )(c_all, w_mod, b_mod.reshape(depth, 1, n6))


def _proj_kernel(x_ref, g_ref, sh_ref, sc_ref, w_ref, o_ref):
    h = _normmod(x_ref[...], g_ref[...], sh_ref[0], sc_ref[0])
    o_ref[...] = jnp.dot(h.astype(BF16), w_ref[...], preferred_element_type=F32).astype(o_ref.dtype)


def _norm_proj(x, g, shift, scale, w, rows_per_mod, tm, out_dtype, name):
    n, d = x.shape
    nout = w.shape[1]
    mod_idx = lambda i: ((i * tm) // rows_per_mod, 0, 0)
    return pl.pallas_call(
        _proj_kernel,
        grid=(n // tm,),
        in_specs=[pl.BlockSpec((tm, d), lambda i: (i, 0)),
                  _const_spec((1, d)),
                  pl.BlockSpec((1, 1, d), mod_idx),
                  pl.BlockSpec((1, 1, d), mod_idx),
                  _const_spec((d, nout))],
        out_specs=pl.BlockSpec((tm, nout), lambda i: (i, 0)),
        out_shape=jax.ShapeDtypeStruct((n, nout), out_dtype),
        compiler_params=_cparams(("arbitrary",)),
        name=name,
    )(x, g, shift, scale, w)


def _mla_prep_kernel(cq_ref, ckv_ref, kr_ref, gq_ref, gkv_ref, wq_ref, wkv_ref, c2_ref, s2_ref,
                     q_ref, k_ref, v_ref, *, scale):
    c2 = c2_ref[...]
    s2 = s2_ref[...]

    def rope2(v):
        return v * c2 + pltpu.roll(v, 64, 1) * s2

    q = jnp.dot(_rms(cq_ref[...], gq_ref[...]).astype(BF16), wq_ref[...], preferred_element_type=F32)
    kv = jnp.dot(_rms(ckv_ref[...], gkv_ref[...]).astype(BF16), wkv_ref[...], preferred_element_type=F32)
    krope = rope2(kr_ref[...]).astype(BF16)
    for h in range(MLA_HEADS):
        b = h * 2 * LANES
        q_ref[:, b:b + LANES] = (q[:, b:b + LANES] * scale).astype(BF16)
        q_ref[:, b + LANES:b + 2 * LANES] = (rope2(q[:, b + LANES:b + 2 * LANES]) * scale).astype(BF16)
        k_ref[:, b:b + LANES] = kv[:, h * LANES:(h + 1) * LANES].astype(BF16)
        k_ref[:, b + LANES:b + 2 * LANES] = krope
    v_ref[...] = kv[:, MLA_HEADS * LANES:].astype(BF16)


def _mla_prep(proj, gq, gkv, wq, wkv, c2, s2, tm, rope_blocks):
    n = proj.shape[0]
    hq = MLA_HEADS * 2 * LANES
    rope_idx = (lambda i: (i % rope_blocks, 0)) if rope_blocks else (lambda i: (0, 0))
    return pl.pallas_call(
        functools.partial(_mla_prep_kernel, scale=float((MLA_NOPE + MLA_ROPE) ** -0.5) * LOG2E),
        grid=(n // tm,),
        in_specs=[pl.BlockSpec((tm, 256), lambda i: (i, 0)),
                  pl.BlockSpec((tm, 128), lambda i: (i, 2)),
                  pl.BlockSpec((tm, 128), lambda i: (i, 3)),
                  _const_spec((1, MLA_Q_RANK)), _const_spec((1, MLA_KV_RANK)),
                  _const_spec(wq.shape), _const_spec(wkv.shape),
                  pl.BlockSpec((tm, LANES), rope_idx), pl.BlockSpec((tm, LANES), rope_idx)],
        out_specs=[pl.BlockSpec((tm, hq), lambda i: (i, 0)),
                   pl.BlockSpec((tm, hq), lambda i: (i, 0)),
                   pl.BlockSpec((tm, MLA_HEADS * MLA_V), lambda i: (i, 0))],
        out_shape=[jax.ShapeDtypeStruct((n, hq), BF16), jax.ShapeDtypeStruct((n, hq), BF16),
                   jax.ShapeDtypeStruct((n, MLA_HEADS * MLA_V), BF16)],
        compiler_params=_cparams(("arbitrary",)),
        name="mla_prep",
    )(proj, proj, proj, gq, gkv, wq, wkv, c2, s2)


def _fill_kv(k_refs, v_refs, k_scr, v_scr):
    r0 = 0
    dv = v_refs[0].shape[1]
    for k_ref, v_ref in zip(k_refs, v_refs):
        r1 = r0 + k_ref.shape[0]
        k_scr[r0:r1, :] = k_ref[...]
        v_scr[r0:r1, 0:dv] = v_ref[...]
        r0 = r1
    v_scr[:, dv:] = jnp.ones((v_scr.shape[0], v_scr.shape[1] - dv), v_scr.dtype)


def _softmax_pv(q, k, v1, dv):
    s = lax.dot_general(q, k, _NT, preferred_element_type=F32)
    p = jnp.exp2(s - jnp.max(s, axis=-1, keepdims=True)).astype(BF16)
    o = jnp.dot(p, v1, preferred_element_type=F32)
    return o[:, 0:dv] / o[:, dv:dv + 1]


def _attn_kernel(*refs, n_src, sub):
    q_ref = refs[0]
    k_refs = refs[1:1 + n_src]
    v_refs = refs[1 + n_src:1 + 2 * n_src]
    o_ref, k_scr, v_scr = refs[1 + 2 * n_src:]

    @pl.when(pl.program_id(2) == 0)
    def _():
        _fill_kv(k_refs, v_refs, k_scr, v_scr)

    k = k_scr[...]
    v1 = v_scr[...]
    dv = o_ref.shape[1]
    for r0 in range(0, q_ref.shape[0], sub):
        o_ref[r0:r0 + sub, :] = _softmax_pv(q_ref[r0:r0 + sub, :], k, v1, dv).astype(o_ref.dtype)


def _attention(q, ks, vs, batch, heads, dq, dv, tq, sub, name):
    nq = q.shape[0] // batch // tq
    n_src = len(ks)
    tk = sum(k.shape[0] for k in ks) // batch
    in_specs = [pl.BlockSpec((tq, dq), lambda b, h, i: (b * nq + i, h))]
    for k in ks:
        in_specs.append(pl.BlockSpec((k.shape[0] // batch, dq), lambda b, h, i: (b, h)))
    for v in vs:
        in_specs.append(pl.BlockSpec((v.shape[0] // batch, dv), lambda b, h, i: (b, h)))
    return pl.pallas_call(
        functools.partial(_attn_kernel, n_src=n_src, sub=sub),
        grid=(batch, heads, nq),
        in_specs=in_specs,
        out_specs=pl.BlockSpec((tq, dv), lambda b, h, i: (b * nq + i, h)),
        out_shape=jax.ShapeDtypeStruct((q.shape[0], heads * dv), BF16),
        scratch_shapes=[pltpu.VMEM((tk, dq), BF16), pltpu.VMEM((tk, 2 * dv), BF16)],
        compiler_params=_cparams(("arbitrary", "arbitrary", "arbitrary")),
        name=name,
    )(q, *ks, *vs)


def _log_sigmoid(z):
    return jnp.minimum(z, 0.0) - jnp.log(1.0 + jnp.exp(-jnp.abs(z)))


def _gla_anchor(cum, level, rev):
    c, hk = cum.shape
    two = 2 * level
    a = level if rev else level - 1
    if two >= 8:
        return jnp.concatenate(
            [jnp.broadcast_to(cum[b * two + a:b * two + a + 1], (two, hk)) for b in range(c // two)], axis=0)
    pos = lax.broadcasted_iota(I32, cum.shape, 0) & (two - 1)
    anc = cum
    for p in range(two):
        if p != a:
            anc = jnp.where(pos == p, pltpu.roll(cum, (p - a) % c, 0), anc)
    return anc


def _gla_chunks(streams):
    c = GLA_CHUNK
    hk = GLA_HEADS * GLA_DK
    ri = lax.broadcasted_iota(I32, (c, c), 0)
    ci = lax.broadcasted_iota(I32, (c, c), 1)
    row = lax.broadcasted_iota(I32, (c, hk), 0)
    pair_lane = lax.broadcasted_iota(I32, (c, LANES), 1)
    keep_first = jnp.where(pair_lane < GLA_DK, 1.0, 0.0).astype(BF16)
    keep_second = jnp.where(pair_lane < GLA_DK, 0.0, 1.0).astype(BF16)

    def head_grams(zb, keys):
        outs = []
        for p in range(hk // LANES):
            kp = keys[:, p * LANES:(p + 1) * LANES]
            rhs = jnp.concatenate([kp * keep_first, kp * keep_second], axis=0)
            outs.append(lax.dot_general(zb[:, p * LANES:(p + 1) * LANES], rhs, _NT, preferred_element_type=F32))
        return jnp.concatenate(outs, axis=1)

    def cumulative(la, rev):
        tri = jnp.where((ci >= ri) if rev else (ci <= ri), 1.0, 0.0).astype(BF16)
        la_hi = la.astype(BF16)
        rest = la - la_hi.astype(F32)
        la_mid = rest.astype(BF16)
        la_lo = (rest - la_mid.astype(F32)).astype(BF16)
        return (jnp.dot(tri, la_hi, preferred_element_type=F32) + jnp.dot(tri, la_mid, preferred_element_type=F32)
                + jnp.dot(tri, la_lo, preferred_element_type=F32))

    cums = [cumulative(la, rev) for (_, _, _, la, _, _, rev) in streams]
    qss = [q * (GLA_DK ** -0.5) for (q, *_) in streams]
    vbs = [v.astype(BF16) for (_, _, v, *_) in streams]

    o_inter = []
    for (q, k, v, la, s_ref, lvl, rev), cum, qs in zip(streams, cums, qss):
        tot = cum[0:1] if rev else cum[c - 1:c]
        s_old = s_ref[...]
        qh = (qs * jnp.exp(cum)).astype(BF16)
        o_inter.append(lax.dot_general(qh, s_old.astype(BF16), _NT, preferred_element_type=F32))
        kh = (k * jnp.exp(tot - cum)).astype(BF16)
        u = jnp.dot(v.T.astype(BF16), kh, preferred_element_type=F32)
        bd = ((lax.broadcasted_iota(I32, u.shape, 0) // GLA_DV)
              == (lax.broadcasted_iota(I32, u.shape, 1) // GLA_DK))
        s_ref[...] = s_old * jnp.exp(tot) + jnp.where(bd, u, 0.0)

    ones = jnp.ones((c, hk), BF16)
    atts = [jnp.where(lvl == 0, head_grams((qs * k).astype(BF16), ones), 0.0)
            for (q, k, v, la, s_ref, lvl, rev), qs in zip(streams, qss)]
    for level in GLA_LEVELS:
        upper = (row & level) != 0
        for i, ((q, k, v, la, s_ref, lvl, rev), cum, qs) in enumerate(zip(streams, cums, qss)):
            is_q = jnp.logical_not(upper) if rev else upper
            x = cum - _gla_anchor(cum, level, rev)
            zb = (jnp.where(is_q, qs, k) * jnp.exp(jnp.where(is_q, x, -x))).astype(BF16)
            atts[i] = jnp.where(lvl == level, head_grams(zb, zb), atts[i])
    outs = []
    for att, vb, oi in zip(atts, vbs, o_inter):
        attb = att.astype(BF16)
        outs.append(oi + jnp.concatenate(
            [jnp.dot(attb[:, h * c:(h + 1) * c], vb[:, h * GLA_DV:(h + 1) * GLA_DV], preferred_element_type=F32)
             for h in range(GLA_HEADS)], axis=1))
    return outs


def _gla_kernel(qf_ref, kf_ref, vf_ref, gf_ref, qb_ref, kb_ref, vb_ref, gb_ref,
                wf_ref, bf_ref, wb_ref, bb_ref, lvlf_ref, lvlb_ref, s0f_ref, s0b_ref,
                of_ref, ob_ref, sf_ref, sb_ref, stf, stb):
    j = pl.program_id(1)

    @pl.when(j == 0)
    def _():
        stf[...] = s0f_ref[0]
        stb[...] = s0b_ref[0]

    def log_decay(g_ref, w_ref, b_ref):
        z = jnp.dot(g_ref[...].astype(BF16), w_ref[...], preferred_element_type=F32) + b_ref[...]
        return _log_sigmoid(z) * (1.0 / GLA_TAU)

    o_f, o_b = _gla_chunks([
        (qf_ref[...], kf_ref[...], vf_ref[...], log_decay(gf_ref, wf_ref, bf_ref), stf, lvlf_ref[...], False),
        (qb_ref[...], kb_ref[...], vb_ref[...], log_decay(gb_ref, wb_ref, bb_ref), stb, lvlb_ref[...], True)])
    of_ref[...] = o_f
    ob_ref[...] = o_b

    @pl.when(j == pl.num_programs(1) - 1)
    def _():
        sf_ref[0] = stf[...]
        sb_ref[0] = stb[...]


def _gla_level_maps():
    c = GLA_CHUNK
    t = jnp.arange(c, dtype=I32)[:, None]
    s = jnp.arange(c, dtype=I32)[None, :]
    diff = t ^ s
    top = jnp.zeros((c, c), I32)
    for level in GLA_LEVELS:
        top = jnp.where((top == 0) & ((diff & level) != 0), level, top)
    fwd = jnp.where(t == s, 0, jnp.where(t > s, top, -1))
    bwd = jnp.where(t == s, 0, jnp.where(t < s, top, -1))
    return jnp.tile(fwd, (1, GLA_HEADS)), jnp.tile(bwd, (1, GLA_HEADS))


def _gla_scan(proj, wf, bf, wb, bb, lvl_f, lvl_b, s0f, s0b, batch):
    n = proj.shape[0]
    c = GLA_CHUNK
    nch = n // batch // c
    hk = GLA_HEADS * GLA_DK
    hv = GLA_HEADS * GLA_DV
    fwd = lambda b, j: b * nch + j
    bwd = lambda b, j: b * nch + (nch - 1 - j)

    def specs(row):
        return [pl.BlockSpec((c, hk), lambda b, j: (row(b, j), 2)),
                pl.BlockSpec((c, hk), lambda b, j: (row(b, j), 3)),
                pl.BlockSpec((c, hv), lambda b, j: (row(b, j), 2)),
                pl.BlockSpec((c, LANES), lambda b, j: (row(b, j), 16))]

    st_spec = pl.BlockSpec((1, hv, hk), lambda b, j: (b, 0, 0))
    return pl.pallas_call(
        _gla_kernel,
        grid=(batch, nch),
        in_specs=specs(fwd) + specs(bwd) + [
            _const_spec((LANES, hk)), _const_spec((1, hk)), _const_spec((LANES, hk)), _const_spec((1, hk)),
            _const_spec(lvl_f.shape), _const_spec(lvl_b.shape), st_spec, st_spec],
        out_specs=[pl.BlockSpec((c, hv), lambda b, j: (fwd(b, j), 0)),
                   pl.BlockSpec((c, hv), lambda b, j: (bwd(b, j), 0)),
                   st_spec, st_spec],
        out_shape=[jax.ShapeDtypeStruct((n, hv), F32), jax.ShapeDtypeStruct((n, hv), F32),
                   jax.ShapeDtypeStruct((batch, hv, hk), F32), jax.ShapeDtypeStruct((batch, hv, hk), F32)],
        scratch_shapes=[pltpu.VMEM((hv, hk), F32), pltpu.VMEM((hv, hk), F32)],
        compiler_params=_cparams(("arbitrary", "arbitrary")),
        name="gla_scan",
    )(proj, proj, proj, proj, proj, proj, proj, proj, wf, bf, wb, bb, lvl_f, lvl_b, s0f, s0b)


def _mla_gla_kernel(q_ref, kx_ref, kl_ref, vx_ref, vl_ref,
                    qf_ref, kf_ref, vf_ref, gf_ref, qb_ref, kb_ref, vb_ref, gb_ref,
                    wf_ref, bf_ref, wb_ref, bb_ref, lvlf_ref, lvlb_ref, s0f_ref, s0b_ref,
                    a_ref, of_ref, ob_ref, k_scr, v_scr, stf, stb, *, sub):
    @pl.when(pl.program_id(1) == 0)
    def _():
        stf[...] = s0f_ref[0]
        stb[...] = s0b_ref[0]

    _fill_kv((kx_ref, kl_ref), (vx_ref, vl_ref), k_scr, v_scr)
    k = k_scr[...]
    v1 = v_scr[...]
    dv = a_ref.shape[1]
    lvl_f = lvlf_ref[...]
    lvl_b = lvlb_ref[...]
    c = GLA_CHUNK
    groups = qf_ref.shape[0] // c
    chains = q_ref.shape[0] // sub // groups

    def log_decay(g, w_ref, b_ref):
        z = jnp.dot(g.astype(BF16), w_ref[...], preferred_element_type=F32) + b_ref[...]
        return _log_sigmoid(z) * (1.0 / GLA_TAU)

    for g in range(groups):
        for r0 in range(g * chains * sub, (g + 1) * chains * sub, sub):
            a_ref[r0:r0 + sub, :] = _softmax_pv(q_ref[r0:r0 + sub, :], k, v1, dv).astype(a_ref.dtype)
        rf = slice(g * c, (g + 1) * c)
        rb = slice((groups - 1 - g) * c, (groups - g) * c)
        o_f, o_b = _gla_chunks([
            (qf_ref[rf, :], kf_ref[rf, :], vf_ref[rf, :], log_decay(gf_ref[rf, :], wf_ref, bf_ref),
             stf, lvl_f, False),
            (qb_ref[rb, :], kb_ref[rb, :], vb_ref[rb, :], log_decay(gb_ref[rb, :], wb_ref, bb_ref),
             stb, lvl_b, True)])
        of_ref[rf, :] = o_f
        ob_ref[rb, :] = o_b


def _mla_gla(q, kx, kl, vx, vl, proj, wf, bf, wb, bb, lvl_f, lvl_b, s0f, s0b, batch, sub):
    n = q.shape[0]
    heads = MLA_HEADS
    t = n // batch
    tx = kx.shape[0] // batch
    dq, dv = 2 * LANES, MLA_V
    hk = GLA_HEADS * GLA_DK
    hv = GLA_HEADS * GLA_DV
    rows = t // heads
    fwd = lambda b, h: b * heads + h
    bwd = lambda b, h: b * heads + (heads - 1 - h)

    def scan_specs(row):
        return [pl.BlockSpec((rows, hk), lambda b, h: (row(b, h), 2)),
                pl.BlockSpec((rows, hk), lambda b, h: (row(b, h), 3)),
                pl.BlockSpec((rows, hv), lambda b, h: (row(b, h), 2)),
                pl.BlockSpec((rows, LANES), lambda b, h: (row(b, h), 16))]

    st_spec = pl.BlockSpec((1, hv, hk), lambda b, h: (b, 0, 0))
    return pl.pallas_call(
        functools.partial(_mla_gla_kernel, sub=sub),
        grid=(batch, heads),
        in_specs=[pl.BlockSpec((t, dq), lambda b, h: (b, h)),
                  pl.BlockSpec((tx, dq), lambda b, h: (b, h)),
                  pl.BlockSpec((t, dq), lambda b, h: (b, h)),
                  pl.BlockSpec((tx, dv), lambda b, h: (b, h)),
                  pl.BlockSpec((t, dv), lambda b, h: (b, h))]
        + scan_specs(fwd) + scan_specs(bwd) + [
            _const_spec((LANES, hk)), _const_spec((1, hk)), _const_spec((LANES, hk)), _const_spec((1, hk)),
            _const_spec(lvl_f.shape), _const_spec(lvl_b.shape), st_spec, st_spec],
        out_specs=[pl.BlockSpec((t, dv), lambda b, h: (b, h)),
                   pl.BlockSpec((rows, hv), lambda b, h: (fwd(b, h), 0)),
                   pl.BlockSpec((rows, hv), lambda b, h: (bwd(b, h), 0))],
        out_shape=[jax.ShapeDtypeStruct((n, heads * dv), BF16),
                   jax.ShapeDtypeStruct((n, hv), F32), jax.ShapeDtypeStruct((n, hv), F32)],
        scratch_shapes=[pltpu.VMEM((tx + t, dq), BF16), pltpu.VMEM((tx + t, 2 * dv), BF16),
                        pltpu.VMEM((hv, hk), F32), pltpu.VMEM((hv, hk), F32)],
        compiler_params=_cparams(("arbitrary", "arbitrary")),
        name="mla_gla",
    )(q, kx, kl, vx, vl, proj, proj, proj, proj, proj, proj, proj, proj, wf, bf, wb, bb, lvl_f, lvl_b, s0f, s0b)


def _swiglu_residual(x, h, gate, g_post, wg_ref, wu_ref, wd_ref, fc):
    acc = jnp.zeros(x.shape, F32)
    for c0 in range(0, wg_ref.shape[1], fc):
        g = jnp.dot(h, wg_ref[:, c0:c0 + fc], preferred_element_type=F32)
        u = jnp.dot(h, wu_ref[:, c0:c0 + fc], preferred_element_type=F32)
        act = (_silu(g) * u).astype(BF16)
        acc = acc + jnp.dot(act, wd_ref[c0:c0 + fc, :], preferred_element_type=F32)
    return x + gate * _rms(acc, g_post)


def _mix_ffn_kernel(a_ref, of_ref, ob_ref, r_ref, go_ref, x_ref, mod_ref, gpm_ref, wo_ref,
                    gpre_ref, gpost_ref, wg_ref, wu_ref, wd_ref, o_ref, *, fc):
    o = of_ref[...] + ob_ref[...]
    r = r_ref[...]
    go = go_ref[...]
    parts = []
    for h in range(GLA_HEADS):
        sl = slice(h * GLA_DV, (h + 1) * GLA_DV)
        parts.append(_rms(o[:, sl], go) * _silu(r[:, sl]))
    g = jnp.concatenate(parts, axis=-1).astype(BF16)
    na = a_ref.shape[1]
    y = (jnp.dot(a_ref[...], wo_ref[0:na, :], preferred_element_type=F32)
         + jnp.dot(g, wo_ref[na:, :], preferred_element_type=F32))
    mod = mod_ref[0]
    x = x_ref[...] + mod[0:1] * _rms(y, gpm_ref[...])
    h = _normmod(x, gpre_ref[...], mod[1:2], mod[2:3]).astype(BF16)
    o_ref[...] = _swiglu_residual(x, h, mod[3:4], gpost_ref[...], wg_ref, wu_ref, wd_ref, fc)


def _mix_ffn(a, o_f, o_b, proj, g_o, x, mod, g_post_mix, w_out, g_pre, g_post, wg, wu, wd, rows_per_mod, tm):
    n, d = x.shape
    hv = GLA_HEADS * GLA_DV
    row = lambda i: (i, 0)
    single = pl.Buffered(1)
    resident = lambda w: pl.BlockSpec(w.shape, lambda i: (0, 0), pipeline_mode=single)
    return pl.pallas_call(
        functools.partial(_mix_ffn_kernel, fc=256),
        grid=(n // tm,),
        in_specs=[pl.BlockSpec((tm, a.shape[1]), row),
                  pl.BlockSpec((tm, hv), row),
                  pl.BlockSpec((tm, hv), row),
                  pl.BlockSpec((tm, hv), lambda i: (i, 3)),
                  _const_spec((1, GLA_DV)),
                  pl.BlockSpec((tm, d), row),
                  pl.BlockSpec((1, mod.shape[1], d), lambda i: ((i * tm) // rows_per_mod, 0, 0)),
                  _const_spec((1, d)), resident(w_out),
                  _const_spec((1, d)), _const_spec((1, d)), resident(wg), resident(wu), resident(wd)],
        out_specs=pl.BlockSpec((tm, d), row),
        out_shape=jax.ShapeDtypeStruct((n, d), F32),
        compiler_params=_cparams(("arbitrary",)),
        name="mix_ffn",
    )(a, o_f, o_b, proj, g_o, x, mod, g_post_mix, w_out, g_pre, g_post, wg, wu, wd)


def _diff_prep_kernel(x_ref, g_ref, sh_ref, sc_ref, w_ref, c_ref, sa_ref, sb_ref, q_ref, k_ref, v_ref):
    h = _normmod(x_ref[...], g_ref[...], sh_ref[0], sc_ref[0]).astype(BF16)
    qkv = jnp.dot(h, w_ref[...], preferred_element_type=F32)
    cc, sa, sb = c_ref[...], sa_ref[...], sb_ref[...]
    width = q_ref.shape[1]

    def rope(v):
        return v * cc + pltpu.roll(v, 96, 1) * sa + pltpu.roll(v, 32, 1) * sb

    for j in range(width // LANES):
        sl = slice(j * LANES, (j + 1) * LANES)
        q_ref[:, sl] = (rope(qkv[:, sl]) * (DIFF_DIM ** -0.5 * LOG2E)).astype(BF16)
        k_ref[:, sl] = rope(qkv[:, width + j * LANES:width + (j + 1) * LANES]).astype(BF16)
    v_ref[...] = qkv[:, 2 * width:].astype(BF16)


def _diff_prep(x, g, shift, scale, w, cc, sa, sb, rows_per_mod, tm, rope_blocks):
    n, d = x.shape
    width = w.shape[1] // 3
    mod_idx = lambda i: ((i * tm) // rows_per_mod, 0, 0)
    rope_spec = pl.BlockSpec((tm, LANES), lambda i: (i % rope_blocks, 0))
    out_spec = pl.BlockSpec((tm, width), lambda i: (i, 0))
    return pl.pallas_call(
        _diff_prep_kernel,
        grid=(n // tm,),
        in_specs=[pl.BlockSpec((tm, d), lambda i: (i, 0)), _const_spec((1, d)),
                  pl.BlockSpec((1, 1, d), mod_idx), pl.BlockSpec((1, 1, d), mod_idx),
                  _const_spec(w.shape), rope_spec, rope_spec, rope_spec],
        out_specs=[out_spec, out_spec, out_spec],
        out_shape=[jax.ShapeDtypeStruct((n, width), BF16)] * 3,
        compiler_params=_cparams(("arbitrary",)),
        name="diff_prep",
    )(x, g, shift, scale, w, cc, sa, sb)


def _diff_attn_kernel(q_ref, kx_ref, kl_ref, vx_ref, vl_ref, lam_ref, go_ref, o_ref, k_scr, v_scr, *,
                      lam_init, sub):
    @pl.when(pl.program_id(2) == 0)
    def _():
        _fill_kv((kx_ref, kl_ref), (vx_ref, vl_ref), k_scr, v_scr)

    lv = lam_ref[...]
    lam = (jnp.exp(jnp.sum(lv[0:1] * lv[1:2], axis=-1, keepdims=True))
           - jnp.exp(jnp.sum(lv[2:3] * lv[3:4], axis=-1, keepdims=True)) + lam_init)
    k = k_scr[...]
    v1 = v_scr[...]
    go = go_ref[...]
    lane = lax.broadcasted_iota(I32, (sub, LANES), 1)
    for r0 in range(0, q_ref.shape[0], sub):
        q = q_ref[r0:r0 + sub, :]
        zero = jnp.zeros_like(q)
        o = (_softmax_pv(jnp.where(lane < DIFF_DIM, q, zero), k, v1, LANES)
             - lam * _softmax_pv(jnp.where(lane >= DIFF_DIM, q, zero), k, v1, LANES))
        o_ref[r0:r0 + sub, :] = (_rms(o, go) * (1.0 - lam_init)).astype(o_ref.dtype)


def _diff_attention(q, kx, kl, vx, vl, lamvec, g_o, batch, tq, sub, lam_init):
    n = q.shape[0]
    nq = n // batch // tq
    tx = kx.shape[0] // batch
    tl = kl.shape[0] // batch
    return pl.pallas_call(
        functools.partial(_diff_attn_kernel, lam_init=lam_init, sub=sub),
        grid=(batch, DIFF_HEADS, nq),
        in_specs=[pl.BlockSpec((tq, LANES), lambda b, h, i: (b * nq + i, h)),
                  pl.BlockSpec((tx, LANES), lambda b, h, i: (b, h)),
                  pl.BlockSpec((tl, LANES), lambda b, h, i: (b, h)),
                  pl.BlockSpec((tx, LANES), lambda b, h, i: (b, DIFF_HEADS + h)),
                  pl.BlockSpec((tl, LANES), lambda b, h, i: (b, h)),
                  _const_spec(lamvec.shape), _const_spec((1, LANES))],
        out_specs=pl.BlockSpec((tq, LANES), lambda b, h, i: (b * nq + i, h)),
        out_shape=jax.ShapeDtypeStruct((n, DIFF_HEADS * LANES), BF16),
        scratch_shapes=[pltpu.VMEM((tx + tl, LANES), BF16), pltpu.VMEM((tx + tl, 2 * LANES), BF16)],
        compiler_params=_cparams(("arbitrary", "arbitrary", "arbitrary")),
        name="diff_attention",
    )(q, kx, kl, vx, vl, lamvec, g_o)


def _out_c_kernel(a_ref, x_ref, gate_ref, gp_ref, w_ref, o_ref):
    y = jnp.dot(a_ref[...], w_ref[...], preferred_element_type=F32)
    o_ref[...] = x_ref[...] + gate_ref[0] * _rms(y, gp_ref[...])


def _out_proj_c(a, x, gate, g_post, w, rows_per_mod, tm):
    n, d = x.shape
    return pl.pallas_call(
        _out_c_kernel,
        grid=(n // tm,),
        in_specs=[pl.BlockSpec((tm, a.shape[1]), lambda i: (i, 0)),
                  pl.BlockSpec((tm, d), lambda i: (i, 0)),
                  pl.BlockSpec((1, 1, d), lambda i: ((i * tm) // rows_per_mod, 0, 0)),
                  _const_spec((1, d)),
                  _const_spec(w.shape)],
        out_specs=pl.BlockSpec((tm, d), lambda i: (i, 0)),
        out_shape=jax.ShapeDtypeStruct((n, d), F32),
        compiler_params=_cparams(("arbitrary",)),
        name="out_proj_c",
    )(a, x, gate, g_post, w)


def _router_kernel(x_ref, g_ref, sh_ref, sc_ref, wr_ref, h_ref, meta_ref, gates_ref, cnt_ref, carry_ref):
    i = pl.program_id(0)

    @pl.when(i == 0)
    def _():
        carry_ref[...] = jnp.zeros_like(carry_ref)

    h = _normmod(x_ref[...], g_ref[...], sh_ref[0], sc_ref[0])
    h_ref[...] = h
    tm = h.shape[0]
    logits = jnp.dot(h, wr_ref[...], precision=lax.Precision.HIGHEST, preferred_element_type=F32)
    lane = lax.broadcasted_iota(I32, logits.shape, 1).astype(F32)
    neg = jnp.float32(-jnp.inf)
    logits = jnp.where(lane < N_EXPERTS, logits, neg)
    m0 = jnp.max(logits, axis=-1, keepdims=True)
    i0 = jnp.min(jnp.where(logits == m0, lane, float(LANES)), axis=-1, keepdims=True)
    rest = jnp.where(lane == i0, neg, logits)
    m1 = jnp.max(rest, axis=-1, keepdims=True)
    i1 = jnp.min(jnp.where(rest == m1, lane, float(LANES)), axis=-1, keepdims=True)
    e = jnp.exp(m1 - m0)
    g0 = 1.0 / (1.0 + e)
    g1 = e / (1.0 + e)
    hit = jnp.where(lane == i0, 1.0, jnp.where(lane == i1, 1.0, 0.0)).astype(F32)
    ri = lax.broadcasted_iota(I32, (tm, tm), 0)
    ci = lax.broadcasted_iota(I32, (tm, tm), 1)
    below = jnp.where(ci < ri, 1.0, 0.0).astype(BF16)
    prefix = jnp.dot(below, hit.astype(BF16), preferred_element_type=F32) + carry_ref[...]
    r0 = jnp.sum(jnp.where(lane == i0, prefix, 0.0), axis=-1, keepdims=True)
    r1 = jnp.sum(jnp.where(lane == i1, prefix, 0.0), axis=-1, keepdims=True)
    carry_ref[...] = carry_ref[...] + jnp.sum(hit, axis=0, keepdims=True)
    meta = jnp.where(lane == 0.0, i0, jnp.where(lane == 1.0, i1, jnp.where(lane == 2.0, r0, r1)))
    meta_ref[...] = meta.astype(I32)
    gates_ref[...] = jnp.where(lane == 0.0, g0, g1)
    cnt_ref[...] = carry_ref[...]


def _router(x, g, shift, scale, w_router, rows_per_mod, tm):
    n, d = x.shape
    mod_idx = lambda i: ((i * tm) // rows_per_mod, 0, 0)
    row = lambda i: (i, 0)
    return pl.pallas_call(
        _router_kernel,
        grid=(n // tm,),
        in_specs=[pl.BlockSpec((tm, d), row), _const_spec((1, d)),
                  pl.BlockSpec((1, 1, d), mod_idx), pl.BlockSpec((1, 1, d), mod_idx),
                  _const_spec(w_router.shape)],
        out_specs=[pl.BlockSpec((tm, d), row), pl.BlockSpec((tm, LANES), row), pl.BlockSpec((tm, LANES), row),
                   _const_spec((1, LANES))],
        out_shape=[jax.ShapeDtypeStruct((n, d), F32), jax.ShapeDtypeStruct((n, LANES), I32),
                   jax.ShapeDtypeStruct((n, LANES), F32), jax.ShapeDtypeStruct((1, LANES), F32)],
        scratch_shapes=[pltpu.VMEM((1, LANES), F32)],
        compiler_params=_cparams(("arbitrary",)),
        name="moe_router",
    )(x, g, shift, scale, w_router)


def _expert_stream_kernel(be_ref, nb_ref, idx_in_ref, idx_out_ref, h_hbm, wg_ref, wu_ref, wd_ref, y_hbm,
                          xbuf, ybuf, sem_in, sem_out, *, sub, nb):
    j = pl.program_id(0)
    bm = xbuf.shape[1]
    nbu = nb_ref[0]
    c = j - 1
    live = (c >= 0) & (c < nbu)
    gather_ok = j < nbu
    scatter_ok = (j >= 2) & (j - 2 < nbu)
    in_slot = j % 2
    cur = (j + 1) % 2

    def start_in(r):
        t = idx_in_ref[0, 0, r]
        pltpu.make_async_copy(h_hbm.at[pl.ds(t, 1), :], xbuf.at[in_slot, pl.ds(r, 1), :],
                              sem_in.at[in_slot]).start()

    def start_out(r):
        t = idx_out_ref[0, 0, r]
        pltpu.make_async_copy(ybuf.at[in_slot, pl.ds(r, 1), :], y_hbm.at[pl.ds(t, 1), :],
                              sem_out.at[in_slot]).start()

    def wait_in(slot):
        pltpu.make_async_copy(h_hbm.at[pl.ds(0, bm), :], xbuf.at[slot], sem_in.at[slot]).wait()

    def wait_out(slot):
        pltpu.make_async_copy(ybuf.at[slot], y_hbm.at[pl.ds(0, bm), :], sem_out.at[slot]).wait()

    def loop(fn):
        def body(r, carry):
            fn(r)
            return carry
        lax.fori_loop(0, bm, body, 0, unroll=8)

    def expert(interleave):
        x = xbuf[cur].astype(BF16)
        f_dim = wg_ref.shape[2]
        n_chunks = f_dim // sub
        per = -(-bm // max(1, (2 * n_chunks) // 3))
        part = jnp.zeros((bm, wd_ref.shape[2]), F32)
        for ci in range(n_chunks):
            c0 = ci * sub
            g = jnp.dot(x, wg_ref[0, :, c0:c0 + sub], preferred_element_type=F32)
            u = jnp.dot(x, wu_ref[0, :, c0:c0 + sub], preferred_element_type=F32)
            act = (_silu(g) * u).astype(BF16)
            part = part + jnp.dot(act, wd_ref[0, c0:c0 + sub, :], preferred_element_type=F32)
            if interleave:
                for r in range(ci * per, min((ci + 1) * per, bm)):
                    start_in(r)
                    start_out(r)
        return part

    def store_result(part):
        @pl.when(j >= 3)
        def _():
            wait_out(cur)
        ybuf[cur] = part

    @pl.when(live)
    def _():
        wait_in(cur)

    steady = live & gather_ok & scatter_ok

    @pl.when(steady)
    def _():
        store_result(expert(True))

    @pl.when(jnp.logical_not(steady))
    def _():
        @pl.when(gather_ok)
        def _():
            loop(start_in)

        @pl.when(scatter_ok)
        def _():
            loop(start_out)

        @pl.when(live)
        def _():
            store_result(expert(False))

        @pl.when(jnp.logical_not(live) & (j >= 3) & (j - 3 < nbu))
        def _():
            wait_out(cur)

        @pl.when((c >= nbu) & (c < nb))
        def _():
            ybuf[cur] = jnp.zeros(ybuf.shape[1:], F32)
            fill = pltpu.make_async_copy(ybuf.at[cur], y_hbm.at[pl.ds(pl.multiple_of(c * bm, bm), bm), :],
                                         sem_out.at[cur])
            fill.start()
            fill.wait()

    @pl.when((j == nb + 1) & (nb - 1 < nbu))
    def _():
        wait_out(in_slot)


def _experts_stream(h, row_token, out_row, blk_e, nb_used, wg, wu, wd, n_out):
    n, d = h.shape
    bm = MOE_ROWS
    nb = row_token.shape[0] // bm
    single = pl.Buffered(1)

    def expert_of(j, be, nbu):
        return be[jnp.clip(j - 1, 0, nbu[0] - 1)]

    w_in_spec = pl.BlockSpec((1, d, wg.shape[2]), lambda j, be, nbu: (expert_of(j, be, nbu), 0, 0),
                             pipeline_mode=single)
    w_out_spec = pl.BlockSpec((1, wd.shape[1], d), lambda j, be, nbu: (expert_of(j, be, nbu), 0, 0),
                              pipeline_mode=single)
    return pl.pallas_call(
        functools.partial(_expert_stream_kernel, sub=256, nb=nb),
        grid_spec=pltpu.PrefetchScalarGridSpec(
            num_scalar_prefetch=2,
            grid=(nb + 2,),
            in_specs=[pl.BlockSpec((1, 1, bm), lambda j, be, nbu: (jnp.minimum(j, nb - 1), 0, 0),
                                   memory_space=pltpu.SMEM),
                      pl.BlockSpec((1, 1, bm), lambda j, be, nbu: (jnp.clip(j - 2, 0, nb - 1), 0, 0),
                                   memory_space=pltpu.SMEM),
                      pl.BlockSpec(memory_space=pl.ANY),
                      w_in_spec, w_in_spec, w_out_spec],
            out_specs=pl.BlockSpec(memory_space=pl.ANY),
            scratch_shapes=[pltpu.VMEM((2, bm, d), F32), pltpu.VMEM((2, bm, d), F32),
                            pltpu.SemaphoreType.DMA((2,)), pltpu.SemaphoreType.DMA((2,))]),
        out_shape=jax.ShapeDtypeStruct((n_out, d), F32),
        compiler_params=_cparams(("arbitrary",)),
        name="moe_experts",
    )(blk_e, nb_used, row_token.reshape(nb, 1, bm), out_row.reshape(nb, 1, bm), h, wg, wu, wd)


def _combine2_kernel(y0_ref, y1_ref, gates_ref, x_ref, gate_ref, gp_ref, o_ref):
    gt = gates_ref[...]
    f = y0_ref[...] * gt[:, 0:1] + y1_ref[...] * gt[:, 1:2]
    o_ref[...] = x_ref[...] + gate_ref[0] * _rms(f, gp_ref[...])


def _combine2(y, gates, x, gate, g_post, rows_per_mod, tm):
    n, d = x.shape
    nt = n // tm
    return pl.pallas_call(
        _combine2_kernel,
        grid=(nt,),
        in_specs=[pl.BlockSpec((tm, d), lambda i: (i, 0)),
                  pl.BlockSpec((tm, d), lambda i: (nt + i, 0)),
                  pl.BlockSpec((tm, LANES), lambda i: (i, 0)),
                  pl.BlockSpec((tm, d), lambda i: (i, 0)),
                  pl.BlockSpec((1, 1, d), lambda i: ((i * tm) // rows_per_mod, 0, 0)),
                  _const_spec((1, d))],
        out_specs=pl.BlockSpec((tm, d), lambda i: (i, 0)),
        out_shape=jax.ShapeDtypeStruct((n, d), F32),
        compiler_params=_cparams(("arbitrary",)),
        name="moe_combine",
    )(y, y, gates, x, gate, g_post)


def _invert_kernel(d0_ref, d1_ref, fill_hbm, info_ref, sem):
    i = pl.program_id(0)
    tb = d0_ref.shape[2]

    @pl.when(i == 0)
    def _():
        fill = pltpu.make_async_copy(fill_hbm, info_ref, sem)
        fill.start()
        fill.wait()

    base = 2 * i * tb

    def body(t, carry):
        code = base + 2 * t
        info_ref[d0_ref[0, 0, t]] = code
        info_ref[d1_ref[0, 0, t]] = code + 1
        return carry

    lax.fori_loop(0, tb, body, 0, unroll=8)


def _invert_routing(dest, cap, tb):
    n = dest.shape[0]
    nt = n // tb
    idx = pl.BlockSpec((1, 1, tb), lambda i: (i, 0, 0), memory_space=pltpu.SMEM)
    return pl.pallas_call(
        _invert_kernel,
        grid=(nt,),
        in_specs=[idx, idx, pl.BlockSpec(memory_space=pl.ANY)],
        out_specs=pl.BlockSpec(memory_space=pltpu.SMEM),
        out_shape=jax.ShapeDtypeStruct((cap,), I32),
        scratch_shapes=[pltpu.SemaphoreType.DMA(())],
        compiler_params=_cparams(("arbitrary",)),
        name="moe_invert",
    )(dest[:, 0].reshape(nt, 1, tb), dest[:, 1].reshape(nt, 1, tb), jnp.full((cap,), -1, I32))


def _moe(x, g_pre, shift, scale, gate, g_post, w_router, wg, wu, wd, rows_per_mod):
    n, d = x.shape
    bm = MOE_ROWS
    wr = jnp.zeros((d, LANES), F32).at[:, :N_EXPERTS].set(w_router)
    h, meta, gates, counts = _router(x, g_pre, shift, scale, wr, rows_per_mod, 512)
    cnt = counts[0, :N_EXPERTS].astype(I32)
    padded = (cnt + bm - 1) // bm * bm
    pend = jnp.cumsum(padded)
    pstart = pend - padded
    dest = pstart[meta[:, 0:2]] + meta[:, 2:4]
    cap = 2 * n + N_EXPERTS * bm
    nb = cap // bm
    info = _invert_routing(dest, cap, min(n, 8192))
    real = info >= 0
    row_token = jnp.where(real, info >> 1, 0)
    spill = 2 * n - 1 + jnp.cumsum(jnp.logical_not(real).astype(I32))
    out_row = jnp.where(real, (info & 1) * n + (info >> 1), spill)
    nb_used = (pend[-1] // bm).astype(I32).reshape(1)
    blk_start = jnp.arange(nb, dtype=I32)[:, None] * bm
    blk_e = jnp.minimum(jnp.sum((pend[None, :] <= blk_start).astype(I32), axis=1), N_EXPERTS - 1)
    y = _experts_stream(h, row_token, out_row, blk_e, nb_used, wg, wu, wd, cap)
    return _combine2(y, gates, x, gate, g_post, rows_per_mod, 512)


def _rope_angles(n_tokens, rot_dim):
    rows = n_tokens // GRID_W
    row = jnp.repeat(jnp.arange(rows, dtype=F32), GRID_W)
    col = jnp.tile(jnp.arange(GRID_W, dtype=F32), rows)
    n_freq = rot_dim // 4
    freq = ROPE_BASE ** (-jnp.arange(n_freq, dtype=F32) / n_freq)
    ang = jnp.concatenate([row[:, None] * freq, col[:, None] * freq], axis=-1)
    return jnp.cos(ang), jnp.sin(ang)


def _rot_cols(w):
    half = w.shape[-1] // 2
    return jnp.concatenate([-w[..., half:], w[..., :half]], axis=-1)


def kernel(x, c, ctx, c_ctx, w_mod, b_mod, g_pre_mix, g_post_mix, g_pre_ffn, g_post_ffn,
           w_in_a, mla_g_q, mla_w_uq, mla_g_kv, mla_w_ukv,
           gla_w_gate_f, gla_b_gate_f, gla_w_gate_b, gla_b_gate_b, gla_g_out, w_out_a,
           w_qkv_c, diff_lq1, diff_lk1, diff_lq2, diff_lk2, diff_g_out, w_out_c,
           ffn_w_gate, ffn_w_up, ffn_w_down,
           moe_w_router, moe_w_gate, moe_w_up, moe_w_down):
    bsz, t, d = x.shape
    tx = ctx.shape[1]
    n, nx = bsz * t, bsz * tx
    xl = x.reshape(n, d)
    xc = ctx.reshape(nx, d)

    rows = -(-(bsz + 1) // 8) * 8
    c_all = jnp.zeros((rows, d), F32).at[:bsz].set(c).at[bsz].set(c_ctx)
    mod_all = _modulation(c_all, w_mod, b_mod).reshape(2, rows, 6, d)

    def mods(i):
        lat = [mod_all[i, :bsz, k].reshape(bsz, 1, d) for k in range(6)]
        cx = [mod_all[i, bsz, k].reshape(1, 1, d) for k in range(6)]
        return lat, cx

    row2 = lambda v: v.reshape(1, -1)

    ml, mx = mods(0)
    wi = w_in_a[0]
    cq, ckv, kr, gq, gk, gv, gr, af, ab = jnp.split(
        wi, [256, 384, 448, 704, 960, 1472, 1984, 2000], axis=-1)
    w_in = jnp.concatenate([cq, ckv, kr, _rot_cols(kr), gq, gk, gv, gr, af, ab,
                            jnp.zeros((d, LANES - 2 * GLA_RANK), F32)], axis=-1).astype(BF16)
    uq = mla_w_uq[0].reshape(MLA_Q_RANK, MLA_HEADS, MLA_NOPE + MLA_ROPE)
    wq = jnp.concatenate([uq[..., :MLA_NOPE], uq[..., MLA_NOPE:], _rot_cols(uq[..., MLA_NOPE:])],
                         axis=-1).reshape(MLA_Q_RANK, MLA_HEADS * 2 * LANES).astype(BF16)
    ukv = mla_w_ukv[0].reshape(MLA_KV_RANK, MLA_HEADS, MLA_NOPE + MLA_V)
    wkv = jnp.concatenate([ukv[..., :MLA_NOPE].reshape(MLA_KV_RANK, -1),
                           ukv[..., MLA_NOPE:].reshape(MLA_KV_RANK, -1)], axis=-1).astype(BF16)
    cos_a, sin_a = _rope_angles(t, MLA_ROPE)
    zpad = jnp.zeros((t, LANES - MLA_ROPE), F32)
    c2 = jnp.concatenate([cos_a, cos_a, zpad], axis=-1)
    s2 = jnp.concatenate([sin_a, sin_a, zpad], axis=-1)
    lane = jnp.arange(LANES)
    c2x = jnp.broadcast_to(jnp.where(lane < MLA_ROPE, 1.0, 0.0).astype(F32), (tx, LANES))
    s2x = jnp.zeros((tx, LANES), F32)

    proj_l = _norm_proj(xl, row2(g_pre_mix[0]), ml[0], ml[1], w_in, t, 512, F32, "in_proj_a")
    proj_x = _norm_proj(xc, row2(g_pre_mix[0]), mx[0], mx[1], w_in, nx, 512, F32, "in_proj_a_ctx")
    gq_, gkv_ = row2(mla_g_q[0]), row2(mla_g_kv[0])
    q_l, k_l, v_l = _mla_prep(proj_l, gq_, gkv_, wq, wkv, c2, s2, 512, t // 512)
    q_x, k_x, v_x = _mla_prep(proj_x, gq_, gkv_, wq, wkv, c2x, s2x, tx, 0)
    a_x = _attention(q_x, [k_x], [v_x], bsz, MLA_HEADS, 2 * LANES, MLA_V, tx, tx, "mla_attention_ctx")

    hk = GLA_HEADS * GLA_DK
    hv = GLA_HEADS * GLA_DV
    wf = jnp.zeros((LANES, hk), F32).at[:GLA_RANK].set(gla_w_gate_f[0]).astype(BF16)
    wb = jnp.zeros((LANES, hk), F32).at[GLA_RANK:2 * GLA_RANK].set(gla_w_gate_b[0]).astype(BF16)
    lvl_f, lvl_b = _gla_level_maps()
    s0 = jnp.zeros((bsz, hv, hk), F32)
    gla_args = (wf, row2(gla_b_gate_f[0]), wb, row2(gla_b_gate_b[0]), lvl_f, lvl_b)
    ox_f, ox_b, sx_f, sx_b = _gla_scan(proj_x, *gla_args, s0, s0, bsz)
    a_l, ol_f, ol_b = _mla_gla(q_l, k_x, k_l, v_x, v_l, proj_l, *gla_args, sx_f, sx_b, bsz, ATTN_SUB)

    w_out = w_out_a[0].astype(BF16)
    g_o = row2(gla_g_out[0])
    wg, wu, wd = ffn_w_gate[0].astype(BF16), ffn_w_up[0].astype(BF16), ffn_w_down[0].astype(BF16)
    gpm, gpf, gqf = row2(g_post_mix[0]), row2(g_pre_ffn[0]), row2(g_post_ffn[0])
    tail_mod = lambda m: jnp.concatenate([m[2], m[3], m[4], m[5]], axis=1)
    xl = _mix_ffn(a_l, ol_f, ol_b, proj_l, g_o, xl, tail_mod(ml), gpm, w_out, gpf, gqf, wg, wu, wd, t, 512)
    xc = _mix_ffn(a_x, ox_f, ox_b, proj_x, g_o, xc, tail_mod(mx), gpm, w_out, gpf, gqf, wg, wu, wd, nx, 512)

    ml, mx = mods(1)
    lam_init = 0.8 - 0.6 * math.exp(-0.3 * 1)
    w_qkv = w_qkv_c[0].astype(BF16)
    width = DIFF_HEADS * 2 * DIFF_DIM
    cos_c, sin_c = _rope_angles(t, DIFF_DIM)
    z32 = jnp.zeros_like(sin_c)
    cc = jnp.concatenate([cos_c] * 4, axis=-1)
    sa = jnp.concatenate([-sin_c, z32, -sin_c, z32], axis=-1)
    sb = jnp.concatenate([z32, sin_c, z32, sin_c], axis=-1)
    q1, k1, v1 = _diff_prep(xl, row2(g_pre_mix[1]), ml[0], ml[1], w_qkv, cc, sa, sb, t, 512, t // 512)
    kv_x = _norm_proj(xc, row2(g_pre_mix[1]), mx[0], mx[1], w_qkv[:, width:], nx, 512, BF16, "diff_kv_ctx")
    lamvec = jnp.zeros((8, DIFF_DIM), F32).at[0].set(diff_lq1[0]).at[1].set(diff_lk1[0]) \
        .at[2].set(diff_lq2[0]).at[3].set(diff_lk2[0])
    a1 = _diff_attention(q1, kv_x, k1, kv_x, v1, lamvec, row2(diff_g_out[0]), bsz, min(t, ATTN_ROWS), ATTN_SUB,
                         lam_init)
    xl = _out_proj_c(a1, xl, ml[2], row2(g_post_mix[1]), w_out_c[0].astype(BF16), t, 512)

    xl = _moe(xl, row2(g_pre_ffn[1]), ml[3], ml[4], ml[5], row2(g_post_ffn[1]), moe_w_router[0],
              moe_w_gate[0].astype(BF16), moe_w_up[0].astype(BF16), moe_w_down[0].astype(BF16), t)
    return xl.reshape(bsz, t, d)
```

```python
import functools
import math

import jax
import jax.numpy as jnp
from jax import lax
from jax.experimental import pallas as pl
from jax.experimental.pallas import tpu as pltpu

F32 = jnp.float32
BF16 = jnp.bfloat16
I32 = jnp.int32

EPS = 1e-6
ROPE_BASE = 10000.0
GRID_W = 64

D_MODEL = 1024
MLA_HEADS = 4
MLA_Q_RANK = 256
MLA_KV_RANK = 128
MLA_NOPE = 128
MLA_ROPE = 64
MLA_V = 128
GLA_HEADS = 4
GLA_DK = 64
GLA_DV = 128
GLA_RANK = 16
GLA_TAU = 16.0
DIFF_HEADS = 8
DIFF_DIM = 64
N_EXPERTS = 8
LANES = 128
GLA_CHUNK = 128
GLA_LEVELS = (64, 32, 16, 8, 4, 2, 1)
MOE_ROWS = 512
ATTN_ROWS = 2048
ATTN_SUB = 256
VMEM_LIMIT = 56 * 1024 * 1024

LOG2E = math.log2(math.e)
_NT = (((1,), (1,)), ((), ()))


def _cparams(sem):
    return pltpu.CompilerParams(dimension_semantics=sem, vmem_limit_bytes=VMEM_LIMIT)


def _rms(xf, g):
    return xf * lax.rsqrt(jnp.mean(xf * xf, axis=-1, keepdims=True) + EPS) * g


def _normmod(x, g, shift, scale):
    return _rms(x.astype(F32), g) * (1.0 + scale) + shift


def _silu(x):
    return x * (1.0 / (1.0 + jnp.exp(-x)))


def _const_spec(shape):
    n = len(shape)
    return pl.BlockSpec(shape, lambda *_: (0,) * n)


def _mod_kernel(c_ref, w_ref, b_ref, o_ref):
    s = _silu(c_ref[...])
    o_ref[0] = jnp.dot(s.astype(BF16), w_ref[0].astype(BF16), preferred_element_type=F32) + b_ref[0]


def _modulation(c_all, w_mod, b_mod):
    depth, d, n6 = w_mod.shape
    rows = c_all.shape[0]
    tn = 1536
    return pl.pallas_call(
        _mod_kernel,
        grid=(depth, n6 // tn),
        in_specs=[pl.BlockSpec((rows, d), lambda i, j: (0, 0)),
                  pl.BlockSpec((1, d, tn), lambda i, j: (i, 0, j)),
                  pl.BlockSpec((1, 1, tn), lambda i, j: (i, 0, j))],
        out_specs=pl.BlockSpec((1, rows, tn), lambda i, j: (i, 0, j)),
        out_shape=jax.ShapeDtypeStruct((depth, rows, n6), F32),
        compiler_params=_cparams(("arbitrary", "arbitrary")),
        name="modulation",
    )(c_all, w_mod, b_mod.reshape(depth, 1, n6))


def _proj_kernel(x_ref, g_ref, sh_ref, sc_ref, w_ref, o_ref):
    h = _normmod(x_ref[...], g_ref[...], sh_ref[0], sc_ref[0])
    o_ref[...] = jnp.dot(h.astype(BF16), w_ref[...], preferred_element_type=F32).astype(o_ref.dtype)


def _norm_proj(x, g, shift, scale, w, rows_per_mod, tm, out_dtype, name):
    n, d = x.shape
    nout = w.shape[1]
    mod_idx = lambda i: ((i * tm) // rows_per_mod, 0, 0)
    return pl.pallas_call(
        _proj_kernel,
        grid=(n // tm,),
        in_specs=[pl.BlockSpec((tm, d), lambda i: (i, 0)),
                  _const_spec((1, d)),
                  pl.BlockSpec((1, 1, d), mod_idx),
                  pl.BlockSpec((1, 1, d), mod_idx),
                  _const_spec((d, nout))],
        out_specs=pl.BlockSpec((tm, nout), lambda i: (i, 0)),
        out_shape=jax.ShapeDtypeStruct((n, nout), out_dtype),
        compiler_params=_cparams(("arbitrary",)),
        name=name,
    )(x, g, shift, scale, w)


def _mla_prep_kernel(cq_ref, ckv_ref, kr_ref, gq_ref, gkv_ref, wq_ref, wkv_ref, c2_ref, s2_ref,
                     q_ref, k_ref, v_ref, *, scale):
    c2 = c2_ref[...]
    s2 = s2_ref[...]

    def rope2(v):
        return v * c2 + pltpu.roll(v, 64, 1) * s2

    q = jnp.dot(_rms(cq_ref[...], gq_ref[...]).astype(BF16), wq_ref[...], preferred_element_type=F32)
    kv = jnp.dot(_rms(ckv_ref[...], gkv_ref[...]).astype(BF16), wkv_ref[...], preferred_element_type=F32)
    krope = rope2(kr_ref[...]).astype(BF16)
    for h in range(MLA_HEADS):
        b = h * 2 * LANES
        q_ref[:, b:b + LANES] = (q[:, b:b + LANES] * scale).astype(BF16)
        q_ref[:, b + LANES:b + 2 * LANES] = (rope2(q[:, b + LANES:b + 2 * LANES]) * scale).astype(BF16)
        k_ref[:, b:b + LANES] = kv[:, h * LANES:(h + 1) * LANES].astype(BF16)
        k_ref[:, b + LANES:b + 2 * LANES] = krope
    v_ref[...] = kv[:, MLA_HEADS * LANES:].astype(BF16)


def _mla_prep(proj, gq, gkv, wq, wkv, c2, s2, tm, rope_blocks):
    n = proj.shape[0]
    hq = MLA_HEADS * 2 * LANES
    rope_idx = (lambda i: (i % rope_blocks, 0)) if rope_blocks else (lambda i: (0, 0))
    return pl.pallas_call(
        functools.partial(_mla_prep_kernel, scale=float((MLA_NOPE + MLA_ROPE) ** -0.5) * LOG2E),
        grid=(n // tm,),
        in_specs=[pl.BlockSpec((tm, 256), lambda i: (i, 0)),
                  pl.BlockSpec((tm, 128), lambda i: (i, 2)),
                  pl.BlockSpec((tm, 128), lambda i: (i, 3)),
                  _const_spec((1, MLA_Q_RANK)), _const_spec((1, MLA_KV_RANK)),
                  _const_spec(wq.shape), _const_spec(wkv.shape),
                  pl.BlockSpec((tm, LANES), rope_idx), pl.BlockSpec((tm, LANES), rope_idx)],
        out_specs=[pl.BlockSpec((tm, hq), lambda i: (i, 0)),
                   pl.BlockSpec((tm, hq), lambda i: (i, 0)),
                   pl.BlockSpec((tm, MLA_HEADS * MLA_V), lambda i: (i, 0))],
        out_shape=[jax.ShapeDtypeStruct((n, hq), BF16), jax.ShapeDtypeStruct((n, hq), BF16),
                   jax.ShapeDtypeStruct((n, MLA_HEADS * MLA_V), BF16)],
        compiler_params=_cparams(("arbitrary",)),
        name="mla_prep",
    )(proj, proj, proj, gq, gkv, wq, wkv, c2, s2)


def _fill_kv(k_refs, v_refs, k_scr, v_scr):
    r0 = 0
    dv = v_refs[0].shape[1]
    for k_ref, v_ref in zip(k_refs, v_refs):
        r1 = r0 + k_ref.shape[0]
        k_scr[r0:r1, :] = k_ref[...]
        v_scr[r0:r1, 0:dv] = v_ref[...]
        r0 = r1
    v_scr[:, dv:] = jnp.ones((v_scr.shape[0], v_scr.shape[1] - dv), v_scr.dtype)


def _softmax_pv(q, k, v1, dv):
    s = lax.dot_general(q, k, _NT, preferred_element_type=F32)
    p = jnp.exp2(s - jnp.max(s, axis=-1, keepdims=True)).astype(BF16)
    o = jnp.dot(p, v1, preferred_element_type=F32)
    return o[:, 0:dv] / o[:, dv:dv + 1]


def _attn_kernel(*refs, n_src, sub):
    q_ref = refs[0]
    k_refs = refs[1:1 + n_src]
    v_refs = refs[1 + n_src:1 + 2 * n_src]
    o_ref, k_scr, v_scr = refs[1 + 2 * n_src:]

    @pl.when(pl.program_id(2) == 0)
    def _():
        _fill_kv(k_refs, v_refs, k_scr, v_scr)

    k = k_scr[...]
    v1 = v_scr[...]
    dv = o_ref.shape[1]
    for r0 in range(0, q_ref.shape[0], sub):
        o_ref[r0:r0 + sub, :] = _softmax_pv(q_ref[r0:r0 + sub, :], k, v1, dv).astype(o_ref.dtype)


def _attention(q, ks, vs, batch, heads, dq, dv, tq, sub, name):
    nq = q.shape[0] // batch // tq
    n_src = len(ks)
    tk = sum(k.shape[0] for k in ks) // batch
    in_specs = [pl.BlockSpec((tq, dq), lambda b, h, i: (b * nq + i, h))]
    for k in ks:
        in_specs.append(pl.BlockSpec((k.shape[0] // batch, dq), lambda b, h, i: (b, h)))
    for v in vs:
        in_specs.append(pl.BlockSpec((v.shape[0] // batch, dv), lambda b, h, i: (b, h)))
    return pl.pallas_call(
        functools.partial(_attn_kernel, n_src=n_src, sub=sub),
        grid=(batch, heads, nq),
        in_specs=in_specs,
        out_specs=pl.BlockSpec((tq, dv), lambda b, h, i: (b * nq + i, h)),
        out_shape=jax.ShapeDtypeStruct((q.shape[0], heads * dv), BF16),
        scratch_shapes=[pltpu.VMEM((tk, dq), BF16), pltpu.VMEM((tk, 2 * dv), BF16)],
        compiler_params=_cparams(("arbitrary", "arbitrary", "arbitrary")),
        name=name,
    )(q, *ks, *vs)


def _log_sigmoid(z):
    return jnp.minimum(z, 0.0) - jnp.log(1.0 + jnp.exp(-jnp.abs(z)))


def _gla_anchor(cum, level, rev):
    c, hk = cum.shape
    two = 2 * level
    a = level if rev else level - 1
    if two >= 8:
        return jnp.concatenate(
            [jnp.broadcast_to(cum[b * two + a:b * two + a + 1], (two, hk)) for b in range(c // two)], axis=0)
    pos = lax.broadcasted_iota(I32, cum.shape, 0) & (two - 1)
    anc = cum
    for p in range(two):
        if p != a:
            anc = jnp.where(pos == p, pltpu.roll(cum, (p - a) % c, 0), anc)
    return anc


def _gla_chunks(streams):
    c = GLA_CHUNK
    hk = GLA_HEADS * GLA_DK
    ri = lax.broadcasted_iota(I32, (c, c), 0)
    ci = lax.broadcasted_iota(I32, (c, c), 1)
    row = lax.broadcasted_iota(I32, (c, hk), 0)
    pair_lane = lax.broadcasted_iota(I32, (c, LANES), 1)
    keep_first = jnp.where(pair_lane < GLA_DK, 1.0, 0.0).astype(BF16)
    keep_second = jnp.where(pair_lane < GLA_DK, 0.0, 1.0).astype(BF16)

    def head_grams(zb, keys):
        outs = []
        for p in range(hk // LANES):
            kp = keys[:, p * LANES:(p + 1) * LANES]
            rhs = jnp.concatenate([kp * keep_first, kp * keep_second], axis=0)
            outs.append(lax.dot_general(zb[:, p * LANES:(p + 1) * LANES], rhs, _NT, preferred_element_type=F32))
        return jnp.concatenate(outs, axis=1)

    def cumulative(la, rev):
        tri = jnp.where((ci >= ri) if rev else (ci <= ri), 1.0, 0.0).astype(BF16)
        la_hi = la.astype(BF16)
        rest = la - la_hi.astype(F32)
        la_mid = rest.astype(BF16)
        la_lo = (rest - la_mid.astype(F32)).astype(BF16)
        return (jnp.dot(tri, la_hi, preferred_element_type=F32) + jnp.dot(tri, la_mid, preferred_element_type=F32)
                + jnp.dot(tri, la_lo, preferred_element_type=F32))

    cums = [cumulative(la, rev) for (_, _, _, la, _, _, rev) in streams]
    qss = [q * (GLA_DK ** -0.5) for (q, *_) in streams]
    vbs = [v.astype(BF16) for (_, _, v, *_) in streams]

    o_inter = []
    for (q, k, v, la, s_ref, lvl, rev), cum, qs in zip(streams, cums, qss):
        tot = cum[0:1] if rev else cum[c - 1:c]
        s_old = s_ref[...]
        qh = (qs * jnp.exp(cum)).astype(BF16)
        o_inter.append(lax.dot_general(qh, s_old.astype(BF16), _NT, preferred_element_type=F32))
        kh = (k * jnp.exp(tot - cum)).astype(BF16)
        u = jnp.dot(v.T.astype(BF16), kh, preferred_element_type=F32)
        bd = ((lax.broadcasted_iota(I32, u.shape, 0) // GLA_DV)
              == (lax.broadcasted_iota(I32, u.shape, 1) // GLA_DK))
        s_ref[...] = s_old * jnp.exp(tot) + jnp.where(bd, u, 0.0)

    ones = jnp.ones((c, hk), BF16)
    atts = [jnp.where(lvl == 0, head_grams((qs * k).astype(BF16), ones), 0.0)
            for (q, k, v, la, s_ref, lvl, rev), qs in zip(streams, qss)]
    for level in GLA_LEVELS:
        upper = (row & level) != 0
        for i, ((q, k, v, la, s_ref, lvl, rev), cum, qs) in enumerate(zip(streams, cums, qss)):
            is_q = jnp.logical_not(upper) if rev else upper
            x = cum - _gla_anchor(cum, level, rev)
            zb = (jnp.where(is_q, qs, k) * jnp.exp(jnp.where(is_q, x, -x))).astype(BF16)
            atts[i] = jnp.where(lvl == level, head_grams(zb, zb), atts[i])
    outs = []
    for att, vb, oi in zip(atts, vbs, o_inter):
        attb = att.astype(BF16)
        outs.append(oi + jnp.concatenate(
            [jnp.dot(attb[:, h * c:(h + 1) * c], vb[:, h * GLA_DV:(h + 1) * GLA_DV], preferred_element_type=F32)
             for h in range(GLA_HEADS)], axis=1))
    return outs


def _gla_kernel(qf_ref, kf_ref, vf_ref, gf_ref, qb_ref, kb_ref, vb_ref, gb_ref,
                wf_ref, bf_ref, wb_ref, bb_ref, lvlf_ref, lvlb_ref, s0f_ref, s0b_ref,
                of_ref, ob_ref, sf_ref, sb_ref, stf, stb):
    j = pl.program_id(1)

    @pl.when(j == 0)
    def _():
        stf[...] = s0f_ref[0]
        stb[...] = s0b_ref[0]

    def log_decay(g_ref, w_ref, b_ref):
        z = jnp.dot(g_ref[...].astype(BF16), w_ref[...], preferred_element_type=F32) + b_ref[...]
        return _log_sigmoid(z) * (1.0 / GLA_TAU)

    o_f, o_b = _gla_chunks([
        (qf_ref[...], kf_ref[...], vf_ref[...], log_decay(gf_ref, wf_ref, bf_ref), stf, lvlf_ref[...], False),
        (qb_ref[...], kb_ref[...], vb_ref[...], log_decay(gb_ref, wb_ref, bb_ref), stb, lvlb_ref[...], True)])
    of_ref[...] = o_f
    ob_ref[...] = o_b

    @pl.when(j == pl.num_programs(1) - 1)
    def _():
        sf_ref[0] = stf[...]
        sb_ref[0] = stb[...]


def _gla_level_maps():
    c = GLA_CHUNK
    t = jnp.arange(c, dtype=I32)[:, None]
    s = jnp.arange(c, dtype=I32)[None, :]
    diff = t ^ s
    top = jnp.zeros((c, c), I32)
    for level in GLA_LEVELS:
        top = jnp.where((top == 0) & ((diff & level) != 0), level, top)
    fwd = jnp.where(t == s, 0, jnp.where(t > s, top, -1))
    bwd = jnp.where(t == s, 0, jnp.where(t < s, top, -1))
    return jnp.tile(fwd, (1, GLA_HEADS)), jnp.tile(bwd, (1, GLA_HEADS))


def _gla_scan(proj, wf, bf, wb, bb, lvl_f, lvl_b, s0f, s0b, batch):
    n = proj.shape[0]
    c = GLA_CHUNK
    nch = n // batch // c
    hk = GLA_HEADS * GLA_DK
    hv = GLA_HEADS * GLA_DV
    fwd = lambda b, j: b * nch + j
    bwd = lambda b, j: b * nch + (nch - 1 - j)

    def specs(row):
        return [pl.BlockSpec((c, hk), lambda b, j: (row(b, j), 2)),
                pl.BlockSpec((c, hk), lambda b, j: (row(b, j), 3)),
                pl.BlockSpec((c, hv), lambda b, j: (row(b, j), 2)),
                pl.BlockSpec((c, LANES), lambda b, j: (row(b, j), 16))]

    st_spec = pl.BlockSpec((1, hv, hk), lambda b, j: (b, 0, 0))
    return pl.pallas_call(
        _gla_kernel,
        grid=(batch, nch),
        in_specs=specs(fwd) + specs(bwd) + [
            _const_spec((LANES, hk)), _const_spec((1, hk)), _const_spec((LANES, hk)), _const_spec((1, hk)),
            _const_spec(lvl_f.shape), _const_spec(lvl_b.shape), st_spec, st_spec],
        out_specs=[pl.BlockSpec((c, hv), lambda b, j: (fwd(b, j), 0)),
                   pl.BlockSpec((c, hv), lambda b, j: (bwd(b, j), 0)),
                   st_spec, st_spec],
        out_shape=[jax.ShapeDtypeStruct((n, hv), F32), jax.ShapeDtypeStruct((n, hv), F32),
                   jax.ShapeDtypeStruct((batch, hv, hk), F32), jax.ShapeDtypeStruct((batch, hv, hk), F32)],
        scratch_shapes=[pltpu.VMEM((hv, hk), F32), pltpu.VMEM((hv, hk), F32)],
        compiler_params=_cparams(("arbitrary", "arbitrary")),
        name="gla_scan",
    )(proj, proj, proj, proj, proj, proj, proj, proj, wf, bf, wb, bb, lvl_f, lvl_b, s0f, s0b)


def _mla_gla_kernel(q_ref, kx_ref, kl_ref, vx_ref, vl_ref,
                    qf_ref, kf_ref, vf_ref, gf_ref, qb_ref, kb_ref, vb_ref, gb_ref,
                    wf_ref, bf_ref, wb_ref, bb_ref, lvlf_ref, lvlb_ref, s0f_ref, s0b_ref,
                    a_ref, of_ref, ob_ref, k_scr, v_scr, stf, stb, *, sub):
    @pl.when(pl.program_id(1) == 0)
    def _():
        stf[...] = s0f_ref[0]
        stb[...] = s0b_ref[0]

    _fill_kv((kx_ref, kl_ref), (vx_ref, vl_ref), k_scr, v_scr)
    k = k_scr[...]
    v1 = v_scr[...]
    dv = a_ref.shape[1]
    lvl_f = lvlf_ref[...]
    lvl_b = lvlb_ref[...]
    c = GLA_CHUNK
    groups = qf_ref.shape[0] // c
    chains = q_ref.shape[0] // sub // groups

    def log_decay(g, w_ref, b_ref):
        z = jnp.dot(g.astype(BF16), w_ref[...], preferred_element_type=F32) + b_ref[...]
        return _log_sigmoid(z) * (1.0 / GLA_TAU)

    for g in range(groups):
        for r0 in range(g * chains * sub, (g + 1) * chains * sub, sub):
            a_ref[r0:r0 + sub, :] = _softmax_pv(q_ref[r0:r0 + sub, :], k, v1, dv).astype(a_ref.dtype)
        rf = slice(g * c, (g + 1) * c)
        rb = slice((groups - 1 - g) * c, (groups - g) * c)
        o_f, o_b = _gla_chunks([
            (qf_ref[rf, :], kf_ref[rf, :], vf_ref[rf, :], log_decay(gf_ref[rf, :], wf_ref, bf_ref),
             stf, lvl_f, False),
            (qb_ref[rb, :], kb_ref[rb, :], vb_ref[rb, :], log_decay(gb_ref[rb, :], wb_ref, bb_ref),
             stb, lvl_b, True)])
        of_ref[rf, :] = o_f
        ob_ref[rb, :] = o_b


def _mla_gla(q, kx, kl, vx, vl, proj, wf, bf, wb, bb, lvl_f, lvl_b, s0f, s0b, batch, sub):
    n = q.shape[0]
    heads = MLA_HEADS
    t = n // batch
    tx = kx.shape[0] // batch
    dq, dv = 2 * LANES, MLA_V
    hk = GLA_HEADS * GLA_DK
    hv = GLA_HEADS * GLA_DV
    rows = t // heads
    fwd = lambda b, h: b * heads + h
    bwd = lambda b, h: b * heads + (heads - 1 - h)

    def scan_specs(row):
        return [pl.BlockSpec((rows, hk), lambda b, h: (row(b, h), 2)),
                pl.BlockSpec((rows, hk), lambda b, h: (row(b, h), 3)),
                pl.BlockSpec((rows, hv), lambda b, h: (row(b, h), 2)),
                pl.BlockSpec((rows, LANES), lambda b, h: (row(b, h), 16))]

    st_spec = pl.BlockSpec((1, hv, hk), lambda b, h: (b, 0, 0))
    return pl.pallas_call(
        functools.partial(_mla_gla_kernel, sub=sub),
        grid=(batch, heads),
        in_specs=[pl.BlockSpec((t, dq), lambda b, h: (b, h)),
                  pl.BlockSpec((tx, dq), lambda b, h: (b, h)),
                  pl.BlockSpec((t, dq), lambda b, h: (b, h)),
                  pl.BlockSpec((tx, dv), lambda b, h: (b, h)),
                  pl.BlockSpec((t, dv), lambda b, h: (b, h))]
        + scan_specs(fwd) + scan_specs(bwd) + [
            _const_spec((LANES, hk)), _const_spec((1, hk)), _const_spec((LANES, hk)), _const_spec((1, hk)),
            _const_spec(lvl_f.shape), _const_spec(lvl_b.shape), st_spec, st_spec],
        out_specs=[pl.BlockSpec((t, dv), lambda b, h: (b, h)),
                   pl.BlockSpec((rows, hv), lambda b, h: (fwd(b, h), 0)),
                   pl.BlockSpec((rows, hv), lambda b, h: (bwd(b, h), 0))],
        out_shape=[jax.ShapeDtypeStruct((n, heads * dv), BF16),
                   jax.ShapeDtypeStruct((n, hv), F32), jax.ShapeDtypeStruct((n, hv), F32)],
        scratch_shapes=[pltpu.VMEM((tx + t, dq), BF16), pltpu.VMEM((tx + t, 2 * dv), BF16),
                        pltpu.VMEM((hv, hk), F32), pltpu.VMEM((hv, hk), F32)],
        compiler_params=_cparams(("arbitrary", "arbitrary")),
        name="mla_gla",
    )(q, kx, kl, vx, vl, proj, proj, proj, proj, proj, proj, proj, proj, wf, bf, wb, bb, lvl_f, lvl_b, s0f, s0b)


def _swiglu_residual(x, h, gate, g_post, wg_ref, wu_ref, wd_ref, fc):
    acc = jnp.zeros(x.shape, F32)
    for c0 in range(0, wg_ref.shape[1], fc):
        g = jnp.dot(h, wg_ref[:, c0:c0 + fc], preferred_element_type=F32)
        u = jnp.dot(h, wu_ref[:, c0:c0 + fc], preferred_element_type=F32)
        act = (_silu(g) * u).astype(BF16)
        acc = acc + jnp.dot(act, wd_ref[c0:c0 + fc, :], preferred_element_type=F32)
    return x + gate * _rms(acc, g_post)


def _mix_ffn_kernel(a_ref, of_ref, ob_ref, r_ref, go_ref, x_ref, mod_ref, gpm_ref, wo_ref,
                    gpre_ref, gpost_ref, wg_ref, wu_ref, wd_ref, o_ref, *, fc):
    o = of_ref[...] + ob_ref[...]
    r = r_ref[...]
    go = go_ref[...]
    parts = []
    for h in range(GLA_HEADS):
        sl = slice(h * GLA_DV, (h + 1) * GLA_DV)
        parts.append(_rms(o[:, sl], go) * _silu(r[:, sl]))
    g = jnp.concatenate(parts, axis=-1).astype(BF16)
    na = a_ref.shape[1]
    y = (jnp.dot(a_ref[...], wo_ref[0:na, :], preferred_element_type=F32)
         + jnp.dot(g, wo_ref[na:, :], preferred_element_type=F32))
    mod = mod_ref[0]
    x = x_ref[...] + mod[0:1] * _rms(y, gpm_ref[...])
    h = _normmod(x, gpre_ref[...], mod[1:2], mod[2:3]).astype(BF16)
    o_ref[...] = _swiglu_residual(x, h, mod[3:4], gpost_ref[...], wg_ref, wu_ref, wd_ref, fc)


def _mix_ffn(a, o_f, o_b, proj, g_o, x, mod, g_post_mix, w_out, g_pre, g_post, wg, wu, wd, rows_per_mod, tm):
    n, d = x.shape
    hv = GLA_HEADS * GLA_DV
    row = lambda i: (i, 0)
    single = pl.Buffered(1)
    resident = lambda w: pl.BlockSpec(w.shape, lambda i: (0, 0), pipeline_mode=single)
    return pl.pallas_call(
        functools.partial(_mix_ffn_kernel, fc=256),
        grid=(n // tm,),
        in_specs=[pl.BlockSpec((tm, a.shape[1]), row),
                  pl.BlockSpec((tm, hv), row),
                  pl.BlockSpec((tm, hv), row),
                  pl.BlockSpec((tm, hv), lambda i: (i, 3)),
                  _const_spec((1, GLA_DV)),
                  pl.BlockSpec((tm, d), row),
                  pl.BlockSpec((1, mod.shape[1], d), lambda i: ((i * tm) // rows_per_mod, 0, 0)),
                  _const_spec((1, d)), resident(w_out),
                  _const_spec((1, d)), _const_spec((1, d)), resident(wg), resident(wu), resident(wd)],
        out_specs=pl.BlockSpec((tm, d), row),
        out_shape=jax.ShapeDtypeStruct((n, d), F32),
        compiler_params=_cparams(("arbitrary",)),
        name="mix_ffn",
    )(a, o_f, o_b, proj, g_o, x, mod, g_post_mix, w_out, g_pre, g_post, wg, wu, wd)


def _diff_prep_kernel(x_ref, g_ref, sh_ref, sc_ref, w_ref, c_ref, sa_ref, sb_ref, q_ref, k_ref, v_ref):
    h = _normmod(x_ref[...], g_ref[...], sh_ref[0], sc_ref[0]).astype(BF16)
    qkv = jnp.dot(h, w_ref[...], preferred_element_type=F32)
    cc, sa, sb = c_ref[...], sa_ref[...], sb_ref[...]
    width = q_ref.shape[1]

    def rope(v):
        return v * cc + pltpu.roll(v, 96, 1) * sa + pltpu.roll(v, 32, 1) * sb

    for j in range(width // LANES):
        sl = slice(j * LANES, (j + 1) * LANES)
        q_ref[:, sl] = (rope(qkv[:, sl]) * (DIFF_DIM ** -0.5 * LOG2E)).astype(BF16)
        k_ref[:, sl] = rope(qkv[:, width + j * LANES:width + (j + 1) * LANES]).astype(BF16)
    v_ref[...] = qkv[:, 2 * width:].astype(BF16)


def _diff_prep(x, g, shift, scale, w, cc, sa, sb, rows_per_mod, tm, rope_blocks):
    n, d = x.shape
    width = w.shape[1] // 3
    mod_idx = lambda i: ((i * tm) // rows_per_mod, 0, 0)
    rope_spec = pl.BlockSpec((tm, LANES), lambda i: (i % rope_blocks, 0))
    out_spec = pl.BlockSpec((tm, width), lambda i: (i, 0))
    return pl.pallas_call(
        _diff_prep_kernel,
        grid=(n // tm,),
        in_specs=[pl.BlockSpec((tm, d), lambda i: (i, 0)), _const_spec((1, d)),
                  pl.BlockSpec((1, 1, d), mod_idx), pl.BlockSpec((1, 1, d), mod_idx),
                  _const_spec(w.shape), rope_spec, rope_spec, rope_spec],
        out_specs=[out_spec, out_spec, out_spec],
        out_shape=[jax.ShapeDtypeStruct((n, width), BF16)] * 3,
        compiler_params=_cparams(("arbitrary",)),
        name="diff_prep",
    )(x, g, shift, scale, w, cc, sa, sb)


def _diff_attn_kernel(*refs, lam_init, sub, n_cast):
    q_ref, kx_ref, kl_ref, vx_ref, vl_ref, lam_ref, go_ref = refs[:7]
    cast_in = refs[7:7 + n_cast]
    o_ref = refs[7 + n_cast]
    cast_out = refs[8 + n_cast:8 + 2 * n_cast]
    k_scr, v_scr = refs[8 + 2 * n_cast:]

    for w_ref, c_ref in zip(cast_in, cast_out):
        c_ref[...] = w_ref[...].astype(c_ref.dtype)

    @pl.when(pl.program_id(2) == 0)
    def _():
        _fill_kv((kx_ref, kl_ref), (vx_ref, vl_ref), k_scr, v_scr)

    lv = lam_ref[...]
    lam = (jnp.exp(jnp.sum(lv[0:1] * lv[1:2], axis=-1, keepdims=True))
           - jnp.exp(jnp.sum(lv[2:3] * lv[3:4], axis=-1, keepdims=True)) + lam_init)
    k = k_scr[...]
    v1 = v_scr[...]
    go = go_ref[...]
    lane = lax.broadcasted_iota(I32, (sub, LANES), 1)
    for r0 in range(0, q_ref.shape[0], sub):
        q = q_ref[r0:r0 + sub, :]
        zero = jnp.zeros_like(q)
        o = (_softmax_pv(jnp.where(lane < DIFF_DIM, q, zero), k, v1, LANES)
             - lam * _softmax_pv(jnp.where(lane >= DIFF_DIM, q, zero), k, v1, LANES))
        o_ref[r0:r0 + sub, :] = (_rms(o, go) * (1.0 - lam_init)).astype(o_ref.dtype)


def _diff_attention(q, kx, kl, vx, vl, lamvec, g_o, batch, tq, sub, lam_init, to_cast):
    n = q.shape[0]
    nq = n // batch // tq
    tx = kx.shape[0] // batch
    tl = kl.shape[0] // batch
    steps = batch * DIFF_HEADS * nq
    width = D_MODEL
    slabs = [w.reshape(steps, w.size // (steps * width), width) for w in to_cast]
    slab_spec = lambda s: pl.BlockSpec((1,) + s.shape[1:], lambda b, h, i: ((b * DIFF_HEADS + h) * nq + i, 0, 0))
    outs = pl.pallas_call(
        functools.partial(_diff_attn_kernel, lam_init=lam_init, sub=sub, n_cast=len(slabs)),
        grid=(batch, DIFF_HEADS, nq),
        in_specs=[pl.BlockSpec((tq, LANES), lambda b, h, i: (b * nq + i, h)),
                  pl.BlockSpec((tx, LANES), lambda b, h, i: (b, h)),
                  pl.BlockSpec((tl, LANES), lambda b, h, i: (b, h)),
                  pl.BlockSpec((tx, LANES), lambda b, h, i: (b, DIFF_HEADS + h)),
                  pl.BlockSpec((tl, LANES), lambda b, h, i: (b, h)),
                  _const_spec(lamvec.shape), _const_spec((1, LANES))] + [slab_spec(s) for s in slabs],
        out_specs=[pl.BlockSpec((tq, LANES), lambda b, h, i: (b * nq + i, h))] + [slab_spec(s) for s in slabs],
        out_shape=[jax.ShapeDtypeStruct((n, DIFF_HEADS * LANES), BF16)]
        + [jax.ShapeDtypeStruct(s.shape, BF16) for s in slabs],
        scratch_shapes=[pltpu.VMEM((tx + tl, LANES), BF16), pltpu.VMEM((tx + tl, 2 * LANES), BF16)],
        compiler_params=_cparams(("arbitrary", "arbitrary", "arbitrary")),
        name="diff_attention",
    )(q, kx, kl, vx, vl, lamvec, g_o, *slabs)
    return outs[0], [c.reshape(w.shape) for c, w in zip(outs[1:], to_cast)]


def _out_c_kernel(a_ref, x_ref, gate_ref, gp_ref, w_ref, o_ref):
    y = jnp.dot(a_ref[...], w_ref[...], preferred_element_type=F32)
    o_ref[...] = x_ref[...] + gate_ref[0] * _rms(y, gp_ref[...])


def _out_proj_c(a, x, gate, g_post, w, rows_per_mod, tm):
    n, d = x.shape
    return pl.pallas_call(
        _out_c_kernel,
        grid=(n // tm,),
        in_specs=[pl.BlockSpec((tm, a.shape[1]), lambda i: (i, 0)),
                  pl.BlockSpec((tm, d), lambda i: (i, 0)),
                  pl.BlockSpec((1, 1, d), lambda i: ((i * tm) // rows_per_mod, 0, 0)),
                  _const_spec((1, d)),
                  _const_spec(w.shape)],
        out_specs=pl.BlockSpec((tm, d), lambda i: (i, 0)),
        out_shape=jax.ShapeDtypeStruct((n, d), F32),
        compiler_params=_cparams(("arbitrary",)),
        name="out_proj_c",
    )(a, x, gate, g_post, w)


def _router_kernel(x_ref, g_ref, sh_ref, sc_ref, wr_ref, h_ref, meta_ref, gates_ref, cnt_ref, carry_ref):
    i = pl.program_id(0)

    @pl.when(i == 0)
    def _():
        carry_ref[...] = jnp.zeros_like(carry_ref)

    h = _normmod(x_ref[...], g_ref[...], sh_ref[0], sc_ref[0])
    h_ref[...] = h
    tm = h.shape[0]
    logits = jnp.dot(h, wr_ref[...], precision=lax.Precision.HIGHEST, preferred_element_type=F32)
    lane = lax.broadcasted_iota(I32, logits.shape, 1).astype(F32)
    neg = jnp.float32(-jnp.inf)
    logits = jnp.where(lane < N_EXPERTS, logits, neg)
    m0 = jnp.max(logits, axis=-1, keepdims=True)
    i0 = jnp.min(jnp.where(logits == m0, lane, float(LANES)), axis=-1, keepdims=True)
    rest = jnp.where(lane == i0, neg, logits)
    m1 = jnp.max(rest, axis=-1, keepdims=True)
    i1 = jnp.min(jnp.where(rest == m1, lane, float(LANES)), axis=-1, keepdims=True)
    e = jnp.exp(m1 - m0)
    g0 = 1.0 / (1.0 + e)
    g1 = e / (1.0 + e)
    hit = jnp.where(lane == i0, 1.0, jnp.where(lane == i1, 1.0, 0.0)).astype(F32)
    ri = lax.broadcasted_iota(I32, (tm, tm), 0)
    ci = lax.broadcasted_iota(I32, (tm, tm), 1)
    below = jnp.where(ci < ri, 1.0, 0.0).astype(BF16)
    prefix = jnp.dot(below, hit.astype(BF16), preferred_element_type=F32) + carry_ref[...]
    r0 = jnp.sum(jnp.where(lane == i0, prefix, 0.0), axis=-1, keepdims=True)
    r1 = jnp.sum(jnp.where(lane == i1, prefix, 0.0), axis=-1, keepdims=True)
    carry_ref[...] = carry_ref[...] + jnp.sum(hit, axis=0, keepdims=True)
    meta = jnp.where(lane == 0.0, i0, jnp.where(lane == 1.0, i1, jnp.where(lane == 2.0, r0, r1)))
    meta_ref[...] = meta.astype(I32)
    gates_ref[...] = jnp.where(lane == 0.0, g0, g1)
    cnt_ref[...] = carry_ref[...]


def _router(x, g, shift, scale, w_router, rows_per_mod, tm):
    n, d = x.shape
    mod_idx = lambda i: ((i * tm) // rows_per_mod, 0, 0)
    row = lambda i: (i, 0)
    return pl.pallas_call(
        _router_kernel,
        grid=(n // tm,),
        in_specs=[pl.BlockSpec((tm, d), row), _const_spec((1, d)),
                  pl.BlockSpec((1, 1, d), mod_idx), pl.BlockSpec((1, 1, d), mod_idx),
                  _const_spec(w_router.shape)],
        out_specs=[pl.BlockSpec((tm, d), row), pl.BlockSpec((tm, LANES), row), pl.BlockSpec((tm, LANES), row),
                   _const_spec((1, LANES))],
        out_shape=[jax.ShapeDtypeStruct((n, d), F32), jax.ShapeDtypeStruct((n, LANES), I32),
                   jax.ShapeDtypeStruct((n, LANES), F32), jax.ShapeDtypeStruct((1, LANES), F32)],
        scratch_shapes=[pltpu.VMEM((1, LANES), F32)],
        compiler_params=_cparams(("arbitrary",)),
        name="moe_router",
    )(x, g, shift, scale, w_router)


def _expert_stream_kernel(be_ref, nb_ref, idx_in_ref, idx_out_ref, h_hbm, wg_ref, wu_ref, wd_ref, y_hbm,
                          xbuf, ybuf, sem_in, sem_out, *, sub, nb):
    j = pl.program_id(0)
    bm = xbuf.shape[1]
    nbu = nb_ref[0]
    c = j - 1
    live = (c >= 0) & (c < nbu)
    gather_ok = j < nbu
    scatter_ok = (j >= 2) & (j - 2 < nbu)
    in_slot = j % 2
    cur = (j + 1) % 2

    def start_in(r):
        t = idx_in_ref[0, 0, r]
        pltpu.make_async_copy(h_hbm.at[pl.ds(t, 1), :], xbuf.at[in_slot, pl.ds(r, 1), :],
                              sem_in.at[in_slot]).start()

    def start_out(r):
        t = idx_out_ref[0, 0, r]
        pltpu.make_async_copy(ybuf.at[in_slot, pl.ds(r, 1), :], y_hbm.at[pl.ds(t, 1), :],
                              sem_out.at[in_slot]).start()

    def wait_in(slot):
        pltpu.make_async_copy(h_hbm.at[pl.ds(0, bm), :], xbuf.at[slot], sem_in.at[slot]).wait()

    def wait_out(slot):
        pltpu.make_async_copy(ybuf.at[slot], y_hbm.at[pl.ds(0, bm), :], sem_out.at[slot]).wait()

    def loop(fn):
        def body(r, carry):
            fn(r)
            return carry
        lax.fori_loop(0, bm, body, 0, unroll=8)

    def expert(interleave):
        x = xbuf[cur].astype(BF16)
        f_dim = wg_ref.shape[2]
        n_chunks = f_dim // sub
        per = -(-bm // max(1, (2 * n_chunks) // 3))
        part = jnp.zeros((bm, wd_ref.shape[2]), F32)
        for ci in range(n_chunks):
            c0 = ci * sub
            g = jnp.dot(x, wg_ref[0, :, c0:c0 + sub], preferred_element_type=F32)
            u = jnp.dot(x, wu_ref[0, :, c0:c0 + sub], preferred_element_type=F32)
            act = (_silu(g) * u).astype(BF16)
            part = part + jnp.dot(act, wd_ref[0, c0:c0 + sub, :], preferred_element_type=F32)
            if interleave:
                for r in range(ci * per, min((ci + 1) * per, bm)):
                    start_in(r)
                    start_out(r)
        return part

    def store_result(part):
        @pl.when(j >= 3)
        def _():
            wait_out(cur)
        ybuf[cur] = part

    @pl.when(live)
    def _():
        wait_in(cur)

    steady = live & gather_ok & scatter_ok

    @pl.when(steady)
    def _():
        store_result(expert(True))

    @pl.when(jnp.logical_not(steady))
    def _():
        @pl.when(gather_ok)
        def _():
            loop(start_in)

        @pl.when(scatter_ok)
        def _():
            loop(start_out)

        @pl.when(live)
        def _():
            store_result(expert(False))

        @pl.when(jnp.logical_not(live) & (j >= 3) & (j - 3 < nbu))
        def _():
            wait_out(cur)

        @pl.when((c >= nbu) & (c < nb))
        def _():
            ybuf[cur] = jnp.zeros(ybuf.shape[1:], F32)
            fill = pltpu.make_async_copy(ybuf.at[cur], y_hbm.at[pl.ds(pl.multiple_of(c * bm, bm), bm), :],
                                         sem_out.at[cur])
            fill.start()
            fill.wait()

    @pl.when((j == nb + 1) & (nb - 1 < nbu))
    def _():
        wait_out(in_slot)


def _experts_stream(h, row_token, out_row, blk_e, nb_used, wg, wu, wd, n_out):
    n, d = h.shape
    bm = MOE_ROWS
    nb = row_token.shape[0] // bm
    single = pl.Buffered(1)

    def expert_of(j, be, nbu):
        return be[jnp.clip(j - 1, 0, nbu[0] - 1)]

    w_in_spec = pl.BlockSpec((1, d, wg.shape[2]), lambda j, be, nbu: (expert_of(j, be, nbu), 0, 0),
                             pipeline_mode=single)
    w_out_spec = pl.BlockSpec((1, wd.shape[1], d), lambda j, be, nbu: (expert_of(j, be, nbu), 0, 0),
                              pipeline_mode=single)
    return pl.pallas_call(
        functools.partial(_expert_stream_kernel, sub=256, nb=nb),
        grid_spec=pltpu.PrefetchScalarGridSpec(
            num_scalar_prefetch=2,
            grid=(nb + 2,),
            in_specs=[pl.BlockSpec((1, 1, bm), lambda j, be, nbu: (jnp.minimum(j, nb - 1), 0, 0),
                                   memory_space=pltpu.SMEM),
                      pl.BlockSpec((1, 1, bm), lambda j, be, nbu: (jnp.clip(j - 2, 0, nb - 1), 0, 0),
                                   memory_space=pltpu.SMEM),
                      pl.BlockSpec(memory_space=pl.ANY),
                      w_in_spec, w_in_spec, w_out_spec],
            out_specs=pl.BlockSpec(memory_space=pl.ANY),
            scratch_shapes=[pltpu.VMEM((2, bm, d), F32), pltpu.VMEM((2, bm, d), F32),
                            pltpu.SemaphoreType.DMA((2,)), pltpu.SemaphoreType.DMA((2,))]),
        out_shape=jax.ShapeDtypeStruct((n_out, d), F32),
        compiler_params=_cparams(("arbitrary",)),
        name="moe_experts",
    )(blk_e, nb_used, row_token.reshape(nb, 1, bm), out_row.reshape(nb, 1, bm), h, wg, wu, wd)


def _combine2_kernel(y0_ref, y1_ref, gates_ref, x_ref, gate_ref, gp_ref, o_ref):
    gt = gates_ref[...]
    f = y0_ref[...] * gt[:, 0:1] + y1_ref[...] * gt[:, 1:2]
    o_ref[...] = x_ref[...] + gate_ref[0] * _rms(f, gp_ref[...])


def _combine2(y, gates, x, gate, g_post, rows_per_mod, tm):
    n, d = x.shape
    nt = n // tm
    return pl.pallas_call(
        _combine2_kernel,
        grid=(nt,),
        in_specs=[pl.BlockSpec((tm, d), lambda i: (i, 0)),
                  pl.BlockSpec((tm, d), lambda i: (nt + i, 0)),
                  pl.BlockSpec((tm, LANES), lambda i: (i, 0)),
                  pl.BlockSpec((tm, d), lambda i: (i, 0)),
                  pl.BlockSpec((1, 1, d), lambda i: ((i * tm) // rows_per_mod, 0, 0)),
                  _const_spec((1, d))],
        out_specs=pl.BlockSpec((tm, d), lambda i: (i, 0)),
        out_shape=jax.ShapeDtypeStruct((n, d), F32),
        compiler_params=_cparams(("arbitrary",)),
        name="moe_combine",
    )(y, y, gates, x, gate, g_post)


def _invert_kernel(d0_ref, d1_ref, fill_hbm, info_ref, sem):
    i = pl.program_id(0)
    tb = d0_ref.shape[2]

    @pl.when(i == 0)
    def _():
        fill = pltpu.make_async_copy(fill_hbm, info_ref, sem)
        fill.start()
        fill.wait()

    base = 2 * i * tb

    def body(t, carry):
        code = base + 2 * t
        info_ref[d0_ref[0, 0, t]] = code
        info_ref[d1_ref[0, 0, t]] = code + 1
        return carry

    lax.fori_loop(0, tb, body, 0, unroll=8)


def _invert_routing(dest, cap, tb):
    n = dest.shape[0]
    nt = n // tb
    idx = pl.BlockSpec((1, 1, tb), lambda i: (i, 0, 0), memory_space=pltpu.SMEM)
    return pl.pallas_call(
        _invert_kernel,
        grid=(nt,),
        in_specs=[idx, idx, pl.BlockSpec(memory_space=pl.ANY)],
        out_specs=pl.BlockSpec(memory_space=pltpu.SMEM),
        out_shape=jax.ShapeDtypeStruct((cap,), I32),
        scratch_shapes=[pltpu.SemaphoreType.DMA(())],
        compiler_params=_cparams(("arbitrary",)),
        name="moe_invert",
    )(dest[:, 0].reshape(nt, 1, tb), dest[:, 1].reshape(nt, 1, tb), jnp.full((cap,), -1, I32))


def _moe(x, g_pre, shift, scale, gate, g_post, w_router, wg, wu, wd, rows_per_mod):
    n, d = x.shape
    bm = MOE_ROWS
    wr = jnp.zeros((d, LANES), F32).at[:, :N_EXPERTS].set(w_router)
    h, meta, gates, counts = _router(x, g_pre, shift, scale, wr, rows_per_mod, 512)
    cnt = counts[0, :N_EXPERTS].astype(I32)
    padded = (cnt + bm - 1) // bm * bm
    pend = jnp.cumsum(padded)
    pstart = pend - padded
    dest = pstart[meta[:, 0:2]] + meta[:, 2:4]
    cap = 2 * n + N_EXPERTS * bm
    nb = cap // bm
    info = _invert_routing(dest, cap, min(n, 8192))
    real = info >= 0
    row_token = jnp.where(real, info >> 1, 0)
    spill = 2 * n - 1 + jnp.cumsum(jnp.logical_not(real).astype(I32))
    out_row = jnp.where(real, (info & 1) * n + (info >> 1), spill)
    nb_used = (pend[-1] // bm).astype(I32).reshape(1)
    blk_start = jnp.arange(nb, dtype=I32)[:, None] * bm
    blk_e = jnp.minimum(jnp.sum((pend[None, :] <= blk_start).astype(I32), axis=1), N_EXPERTS - 1)
    y = _experts_stream(h, row_token, out_row, blk_e, nb_used, wg, wu, wd, cap)
    return _combine2(y, gates, x, gate, g_post, rows_per_mod, 512)


def _rope_angles(n_tokens, rot_dim):
    rows = n_tokens // GRID_W
    row = jnp.repeat(jnp.arange(rows, dtype=F32), GRID_W)
    col = jnp.tile(jnp.arange(GRID_W, dtype=F32), rows)
    n_freq = rot_dim // 4
    freq = ROPE_BASE ** (-jnp.arange(n_freq, dtype=F32) / n_freq)
    ang = jnp.concatenate([row[:, None] * freq, col[:, None] * freq], axis=-1)
    return jnp.cos(ang), jnp.sin(ang)


def _rot_cols(w):
    half = w.shape[-1] // 2
    return jnp.concatenate([-w[..., half:], w[..., :half]], axis=-1)


def kernel(x, c, ctx, c_ctx, w_mod, b_mod, g_pre_mix, g_post_mix, g_pre_ffn, g_post_ffn,
           w_in_a, mla_g_q, mla_w_uq, mla_g_kv, mla_w_ukv,
           gla_w_gate_f, gla_b_gate_f, gla_w_gate_b, gla_b_gate_b, gla_g_out, w_out_a,
           w_qkv_c, diff_lq1, diff_lk1, diff_lq2, diff_lk2, diff_g_out, w_out_c,
           ffn_w_gate, ffn_w_up, ffn_w_down,
           moe_w_router, moe_w_gate, moe_w_up, moe_w_down):
    bsz, t, d = x.shape
    tx = ctx.shape[1]
    n, nx = bsz * t, bsz * tx
    xl = x.reshape(n, d)
    xc = ctx.reshape(nx, d)

    rows = -(-(bsz + 1) // 8) * 8
    c_all = jnp.zeros((rows, d), F32).at[:bsz].set(c).at[bsz].set(c_ctx)
    mod_all = _modulation(c_all, w_mod, b_mod).reshape(2, rows, 6, d)

    def mods(i):
        lat = [mod_all[i, :bsz, k].reshape(bsz, 1, d) for k in range(6)]
        cx = [mod_all[i, bsz, k].reshape(1, 1, d) for k in range(6)]
        return lat, cx

    row2 = lambda v: v.reshape(1, -1)

    ml, mx = mods(0)
    wi = w_in_a[0]
    cq, ckv, kr, gq, gk, gv, gr, af, ab = jnp.split(
        wi, [256, 384, 448, 704, 960, 1472, 1984, 2000], axis=-1)
    w_in = jnp.concatenate([cq, ckv, kr, _rot_cols(kr), gq, gk, gv, gr, af, ab,
                            jnp.zeros((d, LANES - 2 * GLA_RANK), F32)], axis=-1).astype(BF16)
    uq = mla_w_uq[0].reshape(MLA_Q_RANK, MLA_HEADS, MLA_NOPE + MLA_ROPE)
    wq = jnp.concatenate([uq[..., :MLA_NOPE], uq[..., MLA_NOPE:], _rot_cols(uq[..., MLA_NOPE:])],
                         axis=-1).reshape(MLA_Q_RANK, MLA_HEADS * 2 * LANES).astype(BF16)
    ukv = mla_w_ukv[0].reshape(MLA_KV_RANK, MLA_HEADS, MLA_NOPE + MLA_V)
    wkv = jnp.concatenate([ukv[..., :MLA_NOPE].reshape(MLA_KV_RANK, -1),
                           ukv[..., MLA_NOPE:].reshape(MLA_KV_RANK, -1)], axis=-1).astype(BF16)
    cos_a, sin_a = _rope_angles(t, MLA_ROPE)
    zpad = jnp.zeros((t, LANES - MLA_ROPE), F32)
    c2 = jnp.concatenate([cos_a, cos_a, zpad], axis=-1)
    s2 = jnp.concatenate([sin_a, sin_a, zpad], axis=-1)
    lane = jnp.arange(LANES)
    c2x = jnp.broadcast_to(jnp.where(lane < MLA_ROPE, 1.0, 0.0).astype(F32), (tx, LANES))
    s2x = jnp.zeros((tx, LANES), F32)

    proj_l = _norm_proj(xl, row2(g_pre_mix[0]), ml[0], ml[1], w_in, t, 512, F32, "in_proj_a")
    proj_x = _norm_proj(xc, row2(g_pre_mix[0]), mx[0], mx[1], w_in, nx, 512, F32, "in_proj_a_ctx")
    gq_, gkv_ = row2(mla_g_q[0]), row2(mla_g_kv[0])
    q_l, k_l, v_l = _mla_prep(proj_l, gq_, gkv_, wq, wkv, c2, s2, 512, t // 512)
    q_x, k_x, v_x = _mla_prep(proj_x, gq_, gkv_, wq, wkv, c2x, s2x, tx, 0)
    a_x = _attention(q_x, [k_x], [v_x], bsz, MLA_HEADS, 2 * LANES, MLA_V, tx, tx, "mla_attention_ctx")

    hk = GLA_HEADS * GLA_DK
    hv = GLA_HEADS * GLA_DV
    wf = jnp.zeros((LANES, hk), F32).at[:GLA_RANK].set(gla_w_gate_f[0]).astype(BF16)
    wb = jnp.zeros((LANES, hk), F32).at[GLA_RANK:2 * GLA_RANK].set(gla_w_gate_b[0]).astype(BF16)
    lvl_f, lvl_b = _gla_level_maps()
    s0 = jnp.zeros((bsz, hv, hk), F32)
    gla_args = (wf, row2(gla_b_gate_f[0]), wb, row2(gla_b_gate_b[0]), lvl_f, lvl_b)
    ox_f, ox_b, sx_f, sx_b = _gla_scan(proj_x, *gla_args, s0, s0, bsz)
    a_l, ol_f, ol_b = _mla_gla(q_l, k_x, k_l, v_x, v_l, proj_l, *gla_args, sx_f, sx_b, bsz, ATTN_SUB)

    w_out = w_out_a[0].astype(BF16)
    g_o = row2(gla_g_out[0])
    wg, wu, wd = ffn_w_gate[0].astype(BF16), ffn_w_up[0].astype(BF16), ffn_w_down[0].astype(BF16)
    gpm, gpf, gqf = row2(g_post_mix[0]), row2(g_pre_ffn[0]), row2(g_post_ffn[0])
    tail_mod = lambda m: jnp.concatenate([m[2], m[3], m[4], m[5]], axis=1)
    xl = _mix_ffn(a_l, ol_f, ol_b, proj_l, g_o, xl, tail_mod(ml), gpm, w_out, gpf, gqf, wg, wu, wd, t, 512)
    xc = _mix_ffn(a_x, ox_f, ox_b, proj_x, g_o, xc, tail_mod(mx), gpm, w_out, gpf, gqf, wg, wu, wd, nx, 512)

    ml, mx = mods(1)
    lam_init = 0.8 - 0.6 * math.exp(-0.3 * 1)
    w_qkv = w_qkv_c[0].astype(BF16)
    width = DIFF_HEADS * 2 * DIFF_DIM
    cos_c, sin_c = _rope_angles(t, DIFF_DIM)
    z32 = jnp.zeros_like(sin_c)
    cc = jnp.concatenate([cos_c] * 4, axis=-1)
    sa = jnp.concatenate([-sin_c, z32, -sin_c, z32], axis=-1)
    sb = jnp.concatenate([z32, sin_c, z32, sin_c], axis=-1)
    q1, k1, v1 = _diff_prep(xl, row2(g_pre_mix[1]), ml[0], ml[1], w_qkv, cc, sa, sb, t, 512, t // 512)
    kv_x = _norm_proj(xc, row2(g_pre_mix[1]), mx[0], mx[1], w_qkv[:, width:], nx, 512, BF16, "diff_kv_ctx")
    lamvec = jnp.zeros((8, DIFF_DIM), F32).at[0].set(diff_lq1[0]).at[1].set(diff_lk1[0]) \
        .at[2].set(diff_lq2[0]).at[3].set(diff_lk2[0])
    a1, (moe_wg, moe_wu, moe_wd) = _diff_attention(
        q1, kv_x, k1, kv_x, v1, lamvec, row2(diff_g_out[0]), bsz, min(t, ATTN_ROWS), ATTN_SUB, lam_init,
        [moe_w_gate[0], moe_w_up[0], moe_w_down[0]])
    xl = _out_proj_c(a1, xl, ml[2], row2(g_post_mix[1]), w_out_c[0].astype(BF16), t, 512)

    xl = _moe(xl, row2(g_pre_ffn[1]), ml[3], ml[4], ml[5], row2(g_post_ffn[1]), moe_w_router[0],
              moe_wg, moe_wu, moe_wd, t)
    return xl.reshape(bsz, t, d)
```

```python
import functools
import math

import jax
import jax.numpy as jnp
from jax import lax
from jax.experimental import pallas as pl
from jax.experimental.pallas import tpu as pltpu

F32 = jnp.float32
BF16 = jnp.bfloat16
I32 = jnp.int32

EPS = 1e-6
ROPE_BASE = 10000.0
GRID_W = 64

D_MODEL = 1024
MLA_HEADS = 4
MLA_Q_RANK = 256
MLA_KV_RANK = 128
MLA_NOPE = 128
MLA_ROPE = 64
MLA_V = 128
GLA_HEADS = 4
GLA_DK = 64
GLA_DV = 128
GLA_RANK = 16
GLA_TAU = 16.0
DIFF_HEADS = 8
DIFF_DIM = 64
N_EXPERTS = 8
LANES = 128
GLA_CHUNK = 128
GLA_LEVELS = (64, 32, 16, 8, 4, 2, 1)
MOE_ROWS = 512
ATTN_ROWS = 2048
ATTN_SUB = 256
VMEM_LIMIT = 56 * 1024 * 1024

LOG2E = math.log2(math.e)
_NT = (((1,), (1,)), ((), ()))


def _cparams(sem):
    return pltpu.CompilerParams(dimension_semantics=sem, vmem_limit_bytes=VMEM_LIMIT)


def _rms(xf, g):
    return xf * lax.rsqrt(jnp.mean(xf * xf, axis=-1, keepdims=True) + EPS) * g


def _normmod(x, g, shift, scale):
    return _rms(x.astype(F32), g) * (1.0 + scale) + shift


def _silu(x):
    return x * (1.0 / (1.0 + jnp.exp(-x)))


def _const_spec(shape):
    n = len(shape)
    return pl.BlockSpec(shape, lambda *_: (0,) * n)


def _mod_kernel(c_ref, w_ref, b_ref, o_ref):
    s = _silu(c_ref[...])
    o_ref[0] = jnp.dot(s.astype(BF16), w_ref[0].astype(BF16), preferred_element_type=F32) + b_ref[0]


def _modulation(c_all, w_mod, b_mod):
    depth, d, n6 = w_mod.shape
    rows = c_all.shape[0]
    tn = 1536
    return pl.pallas_call(
        _mod_kernel,
        grid=(depth, n6 // tn),
        in_specs=[pl.BlockSpec((rows, d), lambda i, j: (0, 0)),
                  pl.BlockSpec((1, d, tn), lambda i, j: (i, 0, j)),
                  pl.BlockSpec((1, 1, tn), lambda i, j: (i, 0, j))],
        out_specs=pl.BlockSpec((1, rows, tn), lambda i, j: (i, 0, j)),
        out_shape=jax.ShapeDtypeStruct((depth, rows, n6), F32),
        compiler_params=_cparams(("arbitrary", "arbitrary")),
        name="modulation",
    )(c_all, w_mod, b_mod.reshape(depth, 1, n6))


def _proj_kernel(x_ref, g_ref, sh_ref, sc_ref, w_ref, o_ref):
    h = _normmod(x_ref[...], g_ref[...], sh_ref[0], sc_ref[0])
    o_ref[...] = jnp.dot(h.astype(BF16), w_ref[...], preferred_element_type=F32).astype(o_ref.dtype)


def _norm_proj(x, g, shift, scale, w, rows_per_mod, tm, out_dtype, name):
    n, d = x.shape
    nout = w.shape[1]
    mod_idx = lambda i: ((i * tm) // rows_per_mod, 0, 0)
    return pl.pallas_call(
        _proj_kernel,
        grid=(n // tm,),
        in_specs=[pl.BlockSpec((tm, d), lambda i: (i, 0)),
                  _const_spec((1, d)),
                  pl.BlockSpec((1, 1, d), mod_idx),
                  pl.BlockSpec((1, 1, d), mod_idx),
                  _const_spec((d, nout))],
        out_specs=pl.BlockSpec((tm, nout), lambda i: (i, 0)),
        out_shape=jax.ShapeDtypeStruct((n, nout), out_dtype),
        compiler_params=_cparams(("arbitrary",)),
        name=name,
    )(x, g, shift, scale, w)


def _mla_prep_kernel(cq_ref, ckv_ref, kr_ref, gq_ref, gkv_ref, wq_ref, wkv_ref, c2_ref, s2_ref,
                     q_ref, k_ref, v_ref, *, scale):
    c2 = c2_ref[...]
    s2 = s2_ref[...]

    def rope2(v):
        return v * c2 + pltpu.roll(v, 64, 1) * s2

    q = jnp.dot(_rms(cq_ref[...], gq_ref[...]).astype(BF16), wq_ref[...], preferred_element_type=F32)
    kv = jnp.dot(_rms(ckv_ref[...], gkv_ref[...]).astype(BF16), wkv_ref[...], preferred_element_type=F32)
    krope = rope2(kr_ref[...]).astype(BF16)
    for h in range(MLA_HEADS):
        b = h * 2 * LANES
        q_ref[:, b:b + LANES] = (q[:, b:b + LANES] * scale).astype(BF16)
        q_ref[:, b + LANES:b + 2 * LANES] = (rope2(q[:, b + LANES:b + 2 * LANES]) * scale).astype(BF16)
        k_ref[:, b:b + LANES] = kv[:, h * LANES:(h + 1) * LANES].astype(BF16)
        k_ref[:, b + LANES:b + 2 * LANES] = krope
    v_ref[...] = kv[:, MLA_HEADS * LANES:].astype(BF16)


def _mla_prep(proj, gq, gkv, wq, wkv, c2, s2, tm, rope_blocks):
    n = proj.shape[0]
    hq = MLA_HEADS * 2 * LANES
    rope_idx = (lambda i: (i % rope_blocks, 0)) if rope_blocks else (lambda i: (0, 0))
    return pl.pallas_call(
        functools.partial(_mla_prep_kernel, scale=float((MLA_NOPE + MLA_ROPE) ** -0.5) * LOG2E),
        grid=(n // tm,),
        in_specs=[pl.BlockSpec((tm, 256), lambda i: (i, 0)),
                  pl.BlockSpec((tm, 128), lambda i: (i, 2)),
                  pl.BlockSpec((tm, 128), lambda i: (i, 3)),
                  _const_spec((1, MLA_Q_RANK)), _const_spec((1, MLA_KV_RANK)),
                  _const_spec(wq.shape), _const_spec(wkv.shape),
                  pl.BlockSpec((tm, LANES), rope_idx), pl.BlockSpec((tm, LANES), rope_idx)],
        out_specs=[pl.BlockSpec((tm, hq), lambda i: (i, 0)),
                   pl.BlockSpec((tm, hq), lambda i: (i, 0)),
                   pl.BlockSpec((tm, MLA_HEADS * MLA_V), lambda i: (i, 0))],
        out_shape=[jax.ShapeDtypeStruct((n, hq), BF16), jax.ShapeDtypeStruct((n, hq), BF16),
                   jax.ShapeDtypeStruct((n, MLA_HEADS * MLA_V), BF16)],
        compiler_params=_cparams(("arbitrary",)),
        name="mla_prep",
    )(proj, proj, proj, gq, gkv, wq, wkv, c2, s2)


def _fill_kv(k_refs, v_refs, k_scr, v_scr):
    r0 = 0
    dv = v_refs[0].shape[1]
    for k_ref, v_ref in zip(k_refs, v_refs):
        r1 = r0 + k_ref.shape[0]
        k_scr[r0:r1, :] = k_ref[...]
        v_scr[r0:r1, 0:dv] = v_ref[...]
        r0 = r1
    v_scr[:, dv:] = jnp.ones((v_scr.shape[0], v_scr.shape[1] - dv), v_scr.dtype)


def _softmax_pv(q, k, v1, dv):
    s = lax.dot_general(q, k, _NT, preferred_element_type=F32)
    p = jnp.exp2(s - jnp.max(s, axis=-1, keepdims=True)).astype(BF16)
    o = jnp.dot(p, v1, preferred_element_type=F32)
    return o[:, 0:dv] / o[:, dv:dv + 1]


def _attn_kernel(*refs, n_src, sub):
    q_ref = refs[0]
    k_refs = refs[1:1 + n_src]
    v_refs = refs[1 + n_src:1 + 2 * n_src]
    o_ref, k_scr, v_scr = refs[1 + 2 * n_src:]

    @pl.when(pl.program_id(2) == 0)
    def _():
        _fill_kv(k_refs, v_refs, k_scr, v_scr)

    k = k_scr[...]
    v1 = v_scr[...]
    dv = o_ref.shape[1]
    for r0 in range(0, q_ref.shape[0], sub):
        o_ref[r0:r0 + sub, :] = _softmax_pv(q_ref[r0:r0 + sub, :], k, v1, dv).astype(o_ref.dtype)


def _attention(q, ks, vs, batch, heads, dq, dv, tq, sub, name):
    nq = q.shape[0] // batch // tq
    n_src = len(ks)
    tk = sum(k.shape[0] for k in ks) // batch
    in_specs = [pl.BlockSpec((tq, dq), lambda b, h, i: (b * nq + i, h))]
    for k in ks:
        in_specs.append(pl.BlockSpec((k.shape[0] // batch, dq), lambda b, h, i: (b, h)))
    for v in vs:
        in_specs.append(pl.BlockSpec((v.shape[0] // batch, dv), lambda b, h, i: (b, h)))
    return pl.pallas_call(
        functools.partial(_attn_kernel, n_src=n_src, sub=sub),
        grid=(batch, heads, nq),
        in_specs=in_specs,
        out_specs=pl.BlockSpec((tq, dv), lambda b, h, i: (b * nq + i, h)),
        out_shape=jax.ShapeDtypeStruct((q.shape[0], heads * dv), BF16),
        scratch_shapes=[pltpu.VMEM((tk, dq), BF16), pltpu.VMEM((tk, 2 * dv), BF16)],
        compiler_params=_cparams(("arbitrary", "arbitrary", "arbitrary")),
        name=name,
    )(q, *ks, *vs)


def _log_sigmoid(z):
    return jnp.minimum(z, 0.0) - jnp.log(1.0 + jnp.exp(-jnp.abs(z)))


def _gla_anchor(cum, level, rev):
    c, hk = cum.shape
    two = 2 * level
    a = level if rev else level - 1
    if two >= 8:
        return jnp.concatenate(
            [jnp.broadcast_to(cum[b * two + a:b * two + a + 1], (two, hk)) for b in range(c // two)], axis=0)
    pos = lax.broadcasted_iota(I32, cum.shape, 0) & (two - 1)
    anc = cum
    for p in range(two):
        if p != a:
            anc = jnp.where(pos == p, pltpu.roll(cum, (p - a) % c, 0), anc)
    return anc


def _gla_chunks(streams):
    c = GLA_CHUNK
    hk = GLA_HEADS * GLA_DK
    ri = lax.broadcasted_iota(I32, (c, c), 0)
    ci = lax.broadcasted_iota(I32, (c, c), 1)
    row = lax.broadcasted_iota(I32, (c, hk), 0)
    pair_lane = lax.broadcasted_iota(I32, (c, LANES), 1)
    keep_first = jnp.where(pair_lane < GLA_DK, 1.0, 0.0).astype(BF16)
    keep_second = jnp.where(pair_lane < GLA_DK, 0.0, 1.0).astype(BF16)

    def head_grams(zb, keys):
        outs = []
        for p in range(hk // LANES):
            kp = keys[:, p * LANES:(p + 1) * LANES]
            rhs = jnp.concatenate([kp * keep_first, kp * keep_second], axis=0)
            outs.append(lax.dot_general(zb[:, p * LANES:(p + 1) * LANES], rhs, _NT, preferred_element_type=F32))
        return jnp.concatenate(outs, axis=1)

    def cumulative(la, rev):
        tri = jnp.where((ci >= ri) if rev else (ci <= ri), 1.0, 0.0).astype(BF16)
        la_hi = la.astype(BF16)
        rest = la - la_hi.astype(F32)
        la_mid = rest.astype(BF16)
        la_lo = (rest - la_mid.astype(F32)).astype(BF16)
        return (jnp.dot(tri, la_hi, preferred_element_type=F32) + jnp.dot(tri, la_mid, preferred_element_type=F32)
                + jnp.dot(tri, la_lo, preferred_element_type=F32))

    cums = [cumulative(la, rev) for (_, _, _, la, _, _, rev) in streams]
    qss = [q * (GLA_DK ** -0.5) for (q, *_) in streams]
    vbs = [v.astype(BF16) for (_, _, v, *_) in streams]

    o_inter = []
    for (q, k, v, la, s_ref, lvl, rev), cum, qs in zip(streams, cums, qss):
        tot = cum[0:1] if rev else cum[c - 1:c]
        s_old = s_ref[...]
        qh = (qs * jnp.exp(cum)).astype(BF16)
        o_inter.append(lax.dot_general(qh, s_old.astype(BF16), _NT, preferred_element_type=F32))
        kh = (k * jnp.exp(tot - cum)).astype(BF16)
        u = jnp.dot(v.T.astype(BF16), kh, preferred_element_type=F32)
        bd = ((lax.broadcasted_iota(I32, u.shape, 0) // GLA_DV)
              == (lax.broadcasted_iota(I32, u.shape, 1) // GLA_DK))
        s_ref[...] = s_old * jnp.exp(tot) + jnp.where(bd, u, 0.0)

    ones = jnp.ones((c, hk), BF16)
    atts = [jnp.where(lvl == 0, head_grams((qs * k).astype(BF16), ones), 0.0)
            for (q, k, v, la, s_ref, lvl, rev), qs in zip(streams, qss)]
    for level in GLA_LEVELS:
        upper = (row & level) != 0
        for i, ((q, k, v, la, s_ref, lvl, rev), cum, qs) in enumerate(zip(streams, cums, qss)):
            is_q = jnp.logical_not(upper) if rev else upper
            x = cum - _gla_anchor(cum, level, rev)
            zb = (jnp.where(is_q, qs, k) * jnp.exp(jnp.where(is_q, x, -x))).astype(BF16)
            atts[i] = jnp.where(lvl == level, head_grams(zb, zb), atts[i])
    outs = []
    for att, vb, oi in zip(atts, vbs, o_inter):
        attb = att.astype(BF16)
        outs.append(oi + jnp.concatenate(
            [jnp.dot(attb[:, h * c:(h + 1) * c], vb[:, h * GLA_DV:(h + 1) * GLA_DV], preferred_element_type=F32)
             for h in range(GLA_HEADS)], axis=1))
    return outs


def _gla_kernel(qf_ref, kf_ref, vf_ref, gf_ref, qb_ref, kb_ref, vb_ref, gb_ref,
                wf_ref, bf_ref, wb_ref, bb_ref, lvlf_ref, lvlb_ref, s0f_ref, s0b_ref,
                of_ref, ob_ref, sf_ref, sb_ref, stf, stb):
    j = pl.program_id(1)

    @pl.when(j == 0)
    def _():
        stf[...] = s0f_ref[0]
        stb[...] = s0b_ref[0]

    def log_decay(g_ref, w_ref, b_ref):
        z = jnp.dot(g_ref[...].astype(BF16), w_ref[...], preferred_element_type=F32) + b_ref[...]
        return _log_sigmoid(z) * (1.0 / GLA_TAU)

    o_f, o_b = _gla_chunks([
        (qf_ref[...], kf_ref[...], vf_ref[...], log_decay(gf_ref, wf_ref, bf_ref), stf, lvlf_ref[...], False),
        (qb_ref[...], kb_ref[...], vb_ref[...], log_decay(gb_ref, wb_ref, bb_ref), stb, lvlb_ref[...], True)])
    of_ref[...] = o_f
    ob_ref[...] = o_b

    @pl.when(j == pl.num_programs(1) - 1)
    def _():
        sf_ref[0] = stf[...]
        sb_ref[0] = stb[...]


def _gla_level_maps():
    c = GLA_CHUNK
    t = jnp.arange(c, dtype=I32)[:, None]
    s = jnp.arange(c, dtype=I32)[None, :]
    diff = t ^ s
    top = jnp.zeros((c, c), I32)
    for level in GLA_LEVELS:
        top = jnp.where((top == 0) & ((diff & level) != 0), level, top)
    fwd = jnp.where(t == s, 0, jnp.where(t > s, top, -1))
    bwd = jnp.where(t == s, 0, jnp.where(t < s, top, -1))
    return jnp.tile(fwd, (1, GLA_HEADS)), jnp.tile(bwd, (1, GLA_HEADS))


def _gla_scan(proj, wf, bf, wb, bb, lvl_f, lvl_b, s0f, s0b, batch):
    n = proj.shape[0]
    c = GLA_CHUNK
    nch = n // batch // c
    hk = GLA_HEADS * GLA_DK
    hv = GLA_HEADS * GLA_DV
    fwd = lambda b, j: b * nch + j
    bwd = lambda b, j: b * nch + (nch - 1 - j)

    def specs(row):
        return [pl.BlockSpec((c, hk), lambda b, j: (row(b, j), 2)),
                pl.BlockSpec((c, hk), lambda b, j: (row(b, j), 3)),
                pl.BlockSpec((c, hv), lambda b, j: (row(b, j), 2)),
                pl.BlockSpec((c, LANES), lambda b, j: (row(b, j), 16))]

    st_spec = pl.BlockSpec((1, hv, hk), lambda b, j: (b, 0, 0))
    return pl.pallas_call(
        _gla_kernel,
        grid=(batch, nch),
        in_specs=specs(fwd) + specs(bwd) + [
            _const_spec((LANES, hk)), _const_spec((1, hk)), _const_spec((LANES, hk)), _const_spec((1, hk)),
            _const_spec(lvl_f.shape), _const_spec(lvl_b.shape), st_spec, st_spec],
        out_specs=[pl.BlockSpec((c, hv), lambda b, j: (fwd(b, j), 0)),
                   pl.BlockSpec((c, hv), lambda b, j: (bwd(b, j), 0)),
                   st_spec, st_spec],
        out_shape=[jax.ShapeDtypeStruct((n, hv), F32), jax.ShapeDtypeStruct((n, hv), F32),
                   jax.ShapeDtypeStruct((batch, hv, hk), F32), jax.ShapeDtypeStruct((batch, hv, hk), F32)],
        scratch_shapes=[pltpu.VMEM((hv, hk), F32), pltpu.VMEM((hv, hk), F32)],
        compiler_params=_cparams(("arbitrary", "arbitrary")),
        name="gla_scan",
    )(proj, proj, proj, proj, proj, proj, proj, proj, wf, bf, wb, bb, lvl_f, lvl_b, s0f, s0b)


def _mla_gla_kernel(q_ref, kx_ref, kl_ref, vx_ref, vl_ref,
                    qf_ref, kf_ref, vf_ref, gf_ref, qb_ref, kb_ref, vb_ref, gb_ref,
                    wf_ref, bf_ref, wb_ref, bb_ref, lvlf_ref, lvlb_ref, s0f_ref, s0b_ref,
                    a_ref, of_ref, ob_ref, k_scr, v_scr, stf, stb, *, sub):
    @pl.when(pl.program_id(1) == 0)
    def _():
        stf[...] = s0f_ref[0]
        stb[...] = s0b_ref[0]

    _fill_kv((kx_ref, kl_ref), (vx_ref, vl_ref), k_scr, v_scr)
    k = k_scr[...]
    v1 = v_scr[...]
    dv = a_ref.shape[1]
    lvl_f = lvlf_ref[...]
    lvl_b = lvlb_ref[...]
    c = GLA_CHUNK
    groups = qf_ref.shape[0] // c
    chains = q_ref.shape[0] // sub // groups

    def log_decay(g, w_ref, b_ref):
        z = jnp.dot(g.astype(BF16), w_ref[...], preferred_element_type=F32) + b_ref[...]
        return _log_sigmoid(z) * (1.0 / GLA_TAU)

    for g in range(groups):
        for r0 in range(g * chains * sub, (g + 1) * chains * sub, sub):
            a_ref[r0:r0 + sub, :] = _softmax_pv(q_ref[r0:r0 + sub, :], k, v1, dv).astype(a_ref.dtype)
        rf = slice(g * c, (g + 1) * c)
        rb = slice((groups - 1 - g) * c, (groups - g) * c)
        o_f, o_b = _gla_chunks([
            (qf_ref[rf, :], kf_ref[rf, :], vf_ref[rf, :], log_decay(gf_ref[rf, :], wf_ref, bf_ref),
             stf, lvl_f, False),
            (qb_ref[rb, :], kb_ref[rb, :], vb_ref[rb, :], log_decay(gb_ref[rb, :], wb_ref, bb_ref),
             stb, lvl_b, True)])
        of_ref[rf, :] = o_f
        ob_ref[rb, :] = o_b


def _mla_gla(q, kx, kl, vx, vl, proj, wf, bf, wb, bb, lvl_f, lvl_b, s0f, s0b, batch, sub):
    n = q.shape[0]
    heads = MLA_HEADS
    t = n // batch
    tx = kx.shape[0] // batch
    dq, dv = 2 * LANES, MLA_V
    hk = GLA_HEADS * GLA_DK
    hv = GLA_HEADS * GLA_DV
    rows = t // heads
    fwd = lambda b, h: b * heads + h
    bwd = lambda b, h: b * heads + (heads - 1 - h)

    def scan_specs(row):
        return [pl.BlockSpec((rows, hk), lambda b, h: (row(b, h), 2)),
                pl.BlockSpec((rows, hk), lambda b, h: (row(b, h), 3)),
                pl.BlockSpec((rows, hv), lambda b, h: (row(b, h), 2)),
                pl.BlockSpec((rows, LANES), lambda b, h: (row(b, h), 16))]

    st_spec = pl.BlockSpec((1, hv, hk), lambda b, h: (b, 0, 0))
    return pl.pallas_call(
        functools.partial(_mla_gla_kernel, sub=sub),
        grid=(batch, heads),
        in_specs=[pl.BlockSpec((t, dq), lambda b, h: (b, h)),
                  pl.BlockSpec((tx, dq), lambda b, h: (b, h)),
                  pl.BlockSpec((t, dq), lambda b, h: (b, h)),
                  pl.BlockSpec((tx, dv), lambda b, h: (b, h)),
                  pl.BlockSpec((t, dv), lambda b, h: (b, h))]
        + scan_specs(fwd) + scan_specs(bwd) + [
            _const_spec((LANES, hk)), _const_spec((1, hk)), _const_spec((LANES, hk)), _const_spec((1, hk)),
            _const_spec(lvl_f.shape), _const_spec(lvl_b.shape), st_spec, st_spec],
        out_specs=[pl.BlockSpec((t, dv), lambda b, h: (b, h)),
                   pl.BlockSpec((rows, hv), lambda b, h: (fwd(b, h), 0)),
                   pl.BlockSpec((rows, hv), lambda b, h: (bwd(b, h), 0))],
        out_shape=[jax.ShapeDtypeStruct((n, heads * dv), BF16),
                   jax.ShapeDtypeStruct((n, hv), F32), jax.ShapeDtypeStruct((n, hv), F32)],
        scratch_shapes=[pltpu.VMEM((tx + t, dq), BF16), pltpu.VMEM((tx + t, 2 * dv), BF16),
                        pltpu.VMEM((hv, hk), F32), pltpu.VMEM((hv, hk), F32)],
        compiler_params=_cparams(("arbitrary", "arbitrary")),
        name="mla_gla",
    )(q, kx, kl, vx, vl, proj, proj, proj, proj, proj, proj, proj, proj, wf, bf, wb, bb, lvl_f, lvl_b, s0f, s0b)


def _swiglu_residual(x, h, gate, g_post, wg_ref, wu_ref, wd_ref, fc):
    acc = jnp.zeros(x.shape, F32)
    for c0 in range(0, wg_ref.shape[1], fc):
        g = jnp.dot(h, wg_ref[:, c0:c0 + fc], preferred_element_type=F32)
        u = jnp.dot(h, wu_ref[:, c0:c0 + fc], preferred_element_type=F32)
        act = (_silu(g) * u).astype(BF16)
        acc = acc + jnp.dot(act, wd_ref[c0:c0 + fc, :], preferred_element_type=F32)
    return x + gate * _rms(acc, g_post)


def _mix_ffn_kernel(a_ref, of_ref, ob_ref, r_ref, go_ref, x_ref, mod_ref, gpm_ref, wo_ref,
                    gpre_ref, gpost_ref, wg_ref, wu_ref, wd_ref, o_ref, *, fc):
    o = of_ref[...] + ob_ref[...]
    r = r_ref[...]
    go = go_ref[...]
    parts = []
    for h in range(GLA_HEADS):
        sl = slice(h * GLA_DV, (h + 1) * GLA_DV)
        parts.append(_rms(o[:, sl], go) * _silu(r[:, sl]))
    g = jnp.concatenate(parts, axis=-1).astype(BF16)
    na = a_ref.shape[1]
    y = (jnp.dot(a_ref[...], wo_ref[0:na, :], preferred_element_type=F32)
         + jnp.dot(g, wo_ref[na:, :], preferred_element_type=F32))
    mod = mod_ref[0]
    x = x_ref[...] + mod[0:1] * _rms(y, gpm_ref[...])
    h = _normmod(x, gpre_ref[...], mod[1:2], mod[2:3]).astype(BF16)
    o_ref[...] = _swiglu_residual(x, h, mod[3:4], gpost_ref[...], wg_ref, wu_ref, wd_ref, fc)


def _mix_ffn(a, o_f, o_b, proj, g_o, x, mod, g_post_mix, w_out, g_pre, g_post, wg, wu, wd, rows_per_mod, tm):
    n, d = x.shape
    hv = GLA_HEADS * GLA_DV
    row = lambda i: (i, 0)
    single = pl.Buffered(1)
    resident = lambda w: pl.BlockSpec(w.shape, lambda i: (0, 0), pipeline_mode=single)
    return pl.pallas_call(
        functools.partial(_mix_ffn_kernel, fc=256),
        grid=(n // tm,),
        in_specs=[pl.BlockSpec((tm, a.shape[1]), row),
                  pl.BlockSpec((tm, hv), row),
                  pl.BlockSpec((tm, hv), row),
                  pl.BlockSpec((tm, hv), lambda i: (i, 3)),
                  _const_spec((1, GLA_DV)),
                  pl.BlockSpec((tm, d), row),
                  pl.BlockSpec((1, mod.shape[1], d), lambda i: ((i * tm) // rows_per_mod, 0, 0)),
                  _const_spec((1, d)), resident(w_out),
                  _const_spec((1, d)), _const_spec((1, d)), resident(wg), resident(wu), resident(wd)],
        out_specs=pl.BlockSpec((tm, d), row),
        out_shape=jax.ShapeDtypeStruct((n, d), F32),
        compiler_params=_cparams(("arbitrary",)),
        name="mix_ffn",
    )(a, o_f, o_b, proj, g_o, x, mod, g_post_mix, w_out, g_pre, g_post, wg, wu, wd)


def _diff_prep_kernel(x_ref, g_ref, sh_ref, sc_ref, w_ref, c_ref, sa_ref, sb_ref, q_ref, k_ref, v_ref):
    h = _normmod(x_ref[...], g_ref[...], sh_ref[0], sc_ref[0]).astype(BF16)
    qkv = jnp.dot(h, w_ref[...], preferred_element_type=F32)
    cc, sa, sb = c_ref[...], sa_ref[...], sb_ref[...]
    width = q_ref.shape[1]

    def rope(v):
        return v * cc + pltpu.roll(v, 96, 1) * sa + pltpu.roll(v, 32, 1) * sb

    for j in range(width // LANES):
        sl = slice(j * LANES, (j + 1) * LANES)
        q_ref[:, sl] = (rope(qkv[:, sl]) * (DIFF_DIM ** -0.5 * LOG2E)).astype(BF16)
        k_ref[:, sl] = rope(qkv[:, width + j * LANES:width + (j + 1) * LANES]).astype(BF16)
    v_ref[...] = qkv[:, 2 * width:].astype(BF16)


def _diff_prep(x, g, shift, scale, w, cc, sa, sb, rows_per_mod, tm, rope_blocks):
    n, d = x.shape
    width = w.shape[1] // 3
    mod_idx = lambda i: ((i * tm) // rows_per_mod, 0, 0)
    rope_spec = pl.BlockSpec((tm, LANES), lambda i: (i % rope_blocks, 0))
    out_spec = pl.BlockSpec((tm, width), lambda i: (i, 0))
    return pl.pallas_call(
        _diff_prep_kernel,
        grid=(n // tm,),
        in_specs=[pl.BlockSpec((tm, d), lambda i: (i, 0)), _const_spec((1, d)),
                  pl.BlockSpec((1, 1, d), mod_idx), pl.BlockSpec((1, 1, d), mod_idx),
                  _const_spec(w.shape), rope_spec, rope_spec, rope_spec],
        out_specs=[out_spec, out_spec, out_spec],
        out_shape=[jax.ShapeDtypeStruct((n, width), BF16)] * 3,
        compiler_params=_cparams(("arbitrary",)),
        name="diff_prep",
    )(x, g, shift, scale, w, cc, sa, sb)


def _diff_attn_kernel(*refs, lam_init, sub, n_cast):
    q_ref, kx_ref, kl_ref, vx_ref, vl_ref, lam_ref, go_ref = refs[:7]
    cast_in = refs[7:7 + n_cast]
    o_ref = refs[7 + n_cast]
    cast_out = refs[8 + n_cast:8 + 2 * n_cast]
    k_scr, v_scr = refs[8 + 2 * n_cast:]

    for w_ref, c_ref in zip(cast_in, cast_out):
        c_ref[...] = w_ref[...].astype(c_ref.dtype)

    @pl.when(pl.program_id(2) == 0)
    def _():
        _fill_kv((kx_ref, kl_ref), (vx_ref, vl_ref), k_scr, v_scr)

    lv = lam_ref[...]
    lam = (jnp.exp(jnp.sum(lv[0:1] * lv[1:2], axis=-1, keepdims=True))
           - jnp.exp(jnp.sum(lv[2:3] * lv[3:4], axis=-1, keepdims=True)) + lam_init)
    k = k_scr[...]
    v1 = v_scr[...]
    go = go_ref[...]
    lane = lax.broadcasted_iota(I32, (sub, LANES), 1)
    for r0 in range(0, q_ref.shape[0], sub):
        q = q_ref[r0:r0 + sub, :]
        zero = jnp.zeros_like(q)
        o = (_softmax_pv(jnp.where(lane < DIFF_DIM, q, zero), k, v1, LANES)
             - lam * _softmax_pv(jnp.where(lane >= DIFF_DIM, q, zero), k, v1, LANES))
        o_ref[r0:r0 + sub, :] = (_rms(o, go) * (1.0 - lam_init)).astype(o_ref.dtype)


def _diff_attention(q, kx, kl, vx, vl, lamvec, g_o, batch, tq, sub, lam_init, to_cast):
    n = q.shape[0]
    nq = n // batch // tq
    tx = kx.shape[0] // batch
    tl = kl.shape[0] // batch
    steps = batch * DIFF_HEADS * nq
    slabs = [w.reshape(steps, w.size // (steps * w.shape[-1]), w.shape[-1]) for w in to_cast]
    slab_spec = lambda s: pl.BlockSpec((1,) + s.shape[1:], lambda b, h, i: ((b * DIFF_HEADS + h) * nq + i, 0, 0))
    outs = pl.pallas_call(
        functools.partial(_diff_attn_kernel, lam_init=lam_init, sub=sub, n_cast=len(slabs)),
        grid=(batch, DIFF_HEADS, nq),
        in_specs=[pl.BlockSpec((tq, LANES), lambda b, h, i: (b * nq + i, h)),
                  pl.BlockSpec((tx, LANES), lambda b, h, i: (b, h)),
                  pl.BlockSpec((tl, LANES), lambda b, h, i: (b, h)),
                  pl.BlockSpec((tx, LANES), lambda b, h, i: (b, DIFF_HEADS + h)),
                  pl.BlockSpec((tl, LANES), lambda b, h, i: (b, h)),
                  _const_spec(lamvec.shape), _const_spec((1, LANES))] + [slab_spec(s) for s in slabs],
        out_specs=[pl.BlockSpec((tq, LANES), lambda b, h, i: (b * nq + i, h))] + [slab_spec(s) for s in slabs],
        out_shape=[jax.ShapeDtypeStruct((n, DIFF_HEADS * LANES), BF16)]
        + [jax.ShapeDtypeStruct(s.shape, BF16) for s in slabs],
        scratch_shapes=[pltpu.VMEM((tx + tl, LANES), BF16), pltpu.VMEM((tx + tl, 2 * LANES), BF16)],
        compiler_params=_cparams(("arbitrary", "arbitrary", "arbitrary")),
        name="diff_attention",
    )(q, kx, kl, vx, vl, lamvec, g_o, *slabs)
    return outs[0], [c.reshape(w.shape) for c, w in zip(outs[1:], to_cast)]


def _out_c_kernel(a_ref, x_ref, gate_ref, gp_ref, w_ref, o_ref):
    y = jnp.dot(a_ref[...], w_ref[...], preferred_element_type=F32)
    o_ref[...] = x_ref[...] + gate_ref[0] * _rms(y, gp_ref[...])


def _out_proj_c(a, x, gate, g_post, w, rows_per_mod, tm):
    n, d = x.shape
    return pl.pallas_call(
        _out_c_kernel,
        grid=(n // tm,),
        in_specs=[pl.BlockSpec((tm, a.shape[1]), lambda i: (i, 0)),
                  pl.BlockSpec((tm, d), lambda i: (i, 0)),
                  pl.BlockSpec((1, 1, d), lambda i: ((i * tm) // rows_per_mod, 0, 0)),
                  _const_spec((1, d)),
                  _const_spec(w.shape)],
        out_specs=pl.BlockSpec((tm, d), lambda i: (i, 0)),
        out_shape=jax.ShapeDtypeStruct((n, d), F32),
        compiler_params=_cparams(("arbitrary",)),
        name="out_proj_c",
    )(a, x, gate, g_post, w)


def _router_kernel(x_ref, g_ref, sh_ref, sc_ref, wr_ref, h_ref, meta_ref, gates_ref, cnt_ref, carry_ref):
    i = pl.program_id(0)

    @pl.when(i == 0)
    def _():
        carry_ref[...] = jnp.zeros_like(carry_ref)

    h = _normmod(x_ref[...], g_ref[...], sh_ref[0], sc_ref[0])
    h_ref[...] = h
    tm = h.shape[0]
    logits = jnp.dot(h, wr_ref[...], precision=lax.Precision.HIGHEST, preferred_element_type=F32)
    lane = lax.broadcasted_iota(I32, logits.shape, 1).astype(F32)
    neg = jnp.float32(-jnp.inf)
    logits = jnp.where(lane < N_EXPERTS, logits, neg)
    m0 = jnp.max(logits, axis=-1, keepdims=True)
    i0 = jnp.min(jnp.where(logits == m0, lane, float(LANES)), axis=-1, keepdims=True)
    rest = jnp.where(lane == i0, neg, logits)
    m1 = jnp.max(rest, axis=-1, keepdims=True)
    i1 = jnp.min(jnp.where(rest == m1, lane, float(LANES)), axis=-1, keepdims=True)
    e = jnp.exp(m1 - m0)
    g0 = 1.0 / (1.0 + e)
    g1 = e / (1.0 + e)
    hit = jnp.where(lane == i0, 1.0, jnp.where(lane == i1, 1.0, 0.0)).astype(F32)
    ri = lax.broadcasted_iota(I32, (tm, tm), 0)
    ci = lax.broadcasted_iota(I32, (tm, tm), 1)
    below = jnp.where(ci < ri, 1.0, 0.0).astype(BF16)
    prefix = jnp.dot(below, hit.astype(BF16), preferred_element_type=F32) + carry_ref[...]
    r0 = jnp.sum(jnp.where(lane == i0, prefix, 0.0), axis=-1, keepdims=True)
    r1 = jnp.sum(jnp.where(lane == i1, prefix, 0.0), axis=-1, keepdims=True)
    carry_ref[...] = carry_ref[...] + jnp.sum(hit, axis=0, keepdims=True)
    meta = jnp.where(lane == 0.0, i0, jnp.where(lane == 1.0, i1, jnp.where(lane == 2.0, r0, r1)))
    meta_ref[...] = meta.astype(I32)
    gates_ref[...] = jnp.where(lane == 0.0, g0, g1)
    cnt_ref[...] = carry_ref[...]


def _router(x, g, shift, scale, w_router, rows_per_mod, tm):
    n, d = x.shape
    mod_idx = lambda i: ((i * tm) // rows_per_mod, 0, 0)
    row = lambda i: (i, 0)
    return pl.pallas_call(
        _router_kernel,
        grid=(n // tm,),
        in_specs=[pl.BlockSpec((tm, d), row), _const_spec((1, d)),
                  pl.BlockSpec((1, 1, d), mod_idx), pl.BlockSpec((1, 1, d), mod_idx),
                  _const_spec(w_router.shape)],
        out_specs=[pl.BlockSpec((tm, d), row), pl.BlockSpec((tm, LANES), row), pl.BlockSpec((tm, LANES), row),
                   _const_spec((1, LANES))],
        out_shape=[jax.ShapeDtypeStruct((n, d), F32), jax.ShapeDtypeStruct((n, LANES), I32),
                   jax.ShapeDtypeStruct((n, LANES), F32), jax.ShapeDtypeStruct((1, LANES), F32)],
        scratch_shapes=[pltpu.VMEM((1, LANES), F32)],
        compiler_params=_cparams(("arbitrary",)),
        name="moe_router",
    )(x, g, shift, scale, w_router)


def _expert_stream_kernel(be_ref, nb_ref, idx_in_ref, idx_out_ref, h_hbm, wg_ref, wu_ref, wd_ref, y_hbm,
                          xbuf, ybuf, sem_in, sem_out, *, sub, nb):
    j = pl.program_id(0)
    bm = xbuf.shape[1]
    nbu = nb_ref[0]
    c = j - 1
    live = (c >= 0) & (c < nbu)
    gather_ok = j < nbu
    scatter_ok = (j >= 2) & (j - 2 < nbu)
    in_slot = j % 2
    cur = (j + 1) % 2

    def start_in(r):
        t = idx_in_ref[0, 0, r]
        pltpu.make_async_copy(h_hbm.at[pl.ds(t, 1), :], xbuf.at[in_slot, pl.ds(r, 1), :],
                              sem_in.at[in_slot]).start()

    def start_out(r):
        t = idx_out_ref[0, 0, r]
        pltpu.make_async_copy(ybuf.at[in_slot, pl.ds(r, 1), :], y_hbm.at[pl.ds(t, 1), :],
                              sem_out.at[in_slot]).start()

    def wait_in(slot):
        pltpu.make_async_copy(h_hbm.at[pl.ds(0, bm), :], xbuf.at[slot], sem_in.at[slot]).wait()

    def wait_out(slot):
        pltpu.make_async_copy(ybuf.at[slot], y_hbm.at[pl.ds(0, bm), :], sem_out.at[slot]).wait()

    def loop(fn):
        def body(r, carry):
            fn(r)
            return carry
        lax.fori_loop(0, bm, body, 0, unroll=8)

    def expert(interleave):
        x = xbuf[cur].astype(BF16)
        f_dim = wg_ref.shape[2]
        n_chunks = f_dim // sub
        per = -(-bm // max(1, (2 * n_chunks) // 3))
        part = jnp.zeros((bm, wd_ref.shape[2]), F32)
        for ci in range(n_chunks):
            c0 = ci * sub
            g = jnp.dot(x, wg_ref[0, :, c0:c0 + sub], preferred_element_type=F32)
            u = jnp.dot(x, wu_ref[0, :, c0:c0 + sub], preferred_element_type=F32)
            act = (_silu(g) * u).astype(BF16)
            part = part + jnp.dot(act, wd_ref[0, c0:c0 + sub, :], preferred_element_type=F32)
            if interleave:
                for r in range(ci * per, min((ci + 1) * per, bm)):
                    start_in(r)
                    start_out(r)
        return part

    def store_result(part):
        @pl.when(j >= 3)
        def _():
            wait_out(cur)
        ybuf[cur] = part

    @pl.when(live)
    def _():
        wait_in(cur)

    steady = live & gather_ok & scatter_ok

    @pl.when(steady)
    def _():
        store_result(expert(True))

    @pl.when(jnp.logical_not(steady))
    def _():
        @pl.when(gather_ok)
        def _():
            loop(start_in)

        @pl.when(scatter_ok)
        def _():
            loop(start_out)

        @pl.when(live)
        def _():
            store_result(expert(False))

        @pl.when(jnp.logical_not(live) & (j >= 3) & (j - 3 < nbu))
        def _():
            wait_out(cur)

        @pl.when((c >= nbu) & (c < nb))
        def _():
            ybuf[cur] = jnp.zeros(ybuf.shape[1:], F32)
            fill = pltpu.make_async_copy(ybuf.at[cur], y_hbm.at[pl.ds(pl.multiple_of(c * bm, bm), bm), :],
                                         sem_out.at[cur])
            fill.start()
            fill.wait()

    @pl.when((j == nb + 1) & (nb - 1 < nbu))
    def _():
        wait_out(in_slot)


def _experts_stream(h, row_token, out_row, blk_e, nb_used, wg, wu, wd, n_out):
    n, d = h.shape
    bm = MOE_ROWS
    nb = row_token.shape[0] // bm
    single = pl.Buffered(1)

    def expert_of(j, be, nbu):
        return be[jnp.clip(j - 1, 0, nbu[0] - 1)]

    w_in_spec = pl.BlockSpec((1, d, wg.shape[2]), lambda j, be, nbu: (expert_of(j, be, nbu), 0, 0),
                             pipeline_mode=single)
    w_out_spec = pl.BlockSpec((1, wd.shape[1], d), lambda j, be, nbu: (expert_of(j, be, nbu), 0, 0),
                              pipeline_mode=single)
    return pl.pallas_call(
        functools.partial(_expert_stream_kernel, sub=256, nb=nb),
        grid_spec=pltpu.PrefetchScalarGridSpec(
            num_scalar_prefetch=2,
            grid=(nb + 2,),
            in_specs=[pl.BlockSpec((1, 1, bm), lambda j, be, nbu: (jnp.minimum(j, nb - 1), 0, 0),
                                   memory_space=pltpu.SMEM),
                      pl.BlockSpec((1, 1, bm), lambda j, be, nbu: (jnp.clip(j - 2, 0, nb - 1), 0, 0),
                                   memory_space=pltpu.SMEM),
                      pl.BlockSpec(memory_space=pl.ANY),
                      w_in_spec, w_in_spec, w_out_spec],
            out_specs=pl.BlockSpec(memory_space=pl.ANY),
            scratch_shapes=[pltpu.VMEM((2, bm, d), F32), pltpu.VMEM((2, bm, d), F32),
                            pltpu.SemaphoreType.DMA((2,)), pltpu.SemaphoreType.DMA((2,))]),
        out_shape=jax.ShapeDtypeStruct((n_out, d), F32),
        compiler_params=_cparams(("arbitrary",)),
        name="moe_experts",
    )(blk_e, nb_used, row_token.reshape(nb, 1, bm), out_row.reshape(nb, 1, bm), h, wg, wu, wd)


def _combine2_kernel(y0_ref, y1_ref, gates_ref, x_ref, gate_ref, gp_ref, o_ref):
    gt = gates_ref[...]
    f = y0_ref[...] * gt[:, 0:1] + y1_ref[...] * gt[:, 1:2]
    o_ref[...] = x_ref[...] + gate_ref[0] * _rms(f, gp_ref[...])


def _combine2(y, gates, x, gate, g_post, rows_per_mod, tm):
    n, d = x.shape
    nt = n // tm
    return pl.pallas_call(
        _combine2_kernel,
        grid=(nt,),
        in_specs=[pl.BlockSpec((tm, d), lambda i: (i, 0)),
                  pl.BlockSpec((tm, d), lambda i: (nt + i, 0)),
                  pl.BlockSpec((tm, LANES), lambda i: (i, 0)),
                  pl.BlockSpec((tm, d), lambda i: (i, 0)),
                  pl.BlockSpec((1, 1, d), lambda i: ((i * tm) // rows_per_mod, 0, 0)),
                  _const_spec((1, d))],
        out_specs=pl.BlockSpec((tm, d), lambda i: (i, 0)),
        out_shape=jax.ShapeDtypeStruct((n, d), F32),
        compiler_params=_cparams(("arbitrary",)),
        name="moe_combine",
    )(y, y, gates, x, gate, g_post)


def _invert_kernel(d0_ref, d1_ref, fill_hbm, info_ref, sem):
    i = pl.program_id(0)
    tb = d0_ref.shape[2]

    @pl.when(i == 0)
    def _():
        fill = pltpu.make_async_copy(fill_hbm, info_ref, sem)
        fill.start()
        fill.wait()

    base = 2 * i * tb

    def body(t, carry):
        code = base + 2 * t
        info_ref[d0_ref[0, 0, t]] = code
        info_ref[d1_ref[0, 0, t]] = code + 1
        return carry

    lax.fori_loop(0, tb, body, 0, unroll=8)


def _invert_routing(dest, cap, tb):
    n = dest.shape[0]
    nt = n // tb
    idx = pl.BlockSpec((1, 1, tb), lambda i: (i, 0, 0), memory_space=pltpu.SMEM)
    return pl.pallas_call(
        _invert_kernel,
        grid=(nt,),
        in_specs=[idx, idx, pl.BlockSpec(memory_space=pl.ANY)],
        out_specs=pl.BlockSpec(memory_space=pltpu.SMEM),
        out_shape=jax.ShapeDtypeStruct((cap,), I32),
        scratch_shapes=[pltpu.SemaphoreType.DMA(())],
        compiler_params=_cparams(("arbitrary",)),
        name="moe_invert",
    )(dest[:, 0].reshape(nt, 1, tb), dest[:, 1].reshape(nt, 1, tb), jnp.full((cap,), -1, I32))


def _moe(x, g_pre, shift, scale, gate, g_post, w_router, wg, wu, wd, rows_per_mod):
    n, d = x.shape
    bm = MOE_ROWS
    wr = jnp.zeros((d, LANES), F32).at[:, :N_EXPERTS].set(w_router)
    h, meta, gates, counts = _router(x, g_pre, shift, scale, wr, rows_per_mod, 512)
    cnt = counts[0, :N_EXPERTS].astype(I32)
    padded = (cnt + bm - 1) // bm * bm
    pend = jnp.cumsum(padded)
    pstart = pend - padded
    dest = pstart[meta[:, 0:2]] + meta[:, 2:4]
    cap = 2 * n + N_EXPERTS * bm
    nb = cap // bm
    info = _invert_routing(dest, cap, min(n, 8192))
    real = info >= 0
    row_token = jnp.where(real, info >> 1, 0)
    spill = 2 * n - 1 + jnp.cumsum(jnp.logical_not(real).astype(I32))
    out_row = jnp.where(real, (info & 1) * n + (info >> 1), spill)
    nb_used = (pend[-1] // bm).astype(I32).reshape(1)
    blk_start = jnp.arange(nb, dtype=I32)[:, None] * bm
    blk_e = jnp.minimum(jnp.sum((pend[None, :] <= blk_start).astype(I32), axis=1), N_EXPERTS - 1)
    y = _experts_stream(h, row_token, out_row, blk_e, nb_used, wg, wu, wd, cap)
    return _combine2(y, gates, x, gate, g_post, rows_per_mod, 512)


def _rope_angles(n_tokens, rot_dim):
    rows = n_tokens // GRID_W
    row = jnp.repeat(jnp.arange(rows, dtype=F32), GRID_W)
    col = jnp.tile(jnp.arange(GRID_W, dtype=F32), rows)
    n_freq = rot_dim // 4
    freq = ROPE_BASE ** (-jnp.arange(n_freq, dtype=F32) / n_freq)
    ang = jnp.concatenate([row[:, None] * freq, col[:, None] * freq], axis=-1)
    return jnp.cos(ang), jnp.sin(ang)


def _rot_cols(w):
    half = w.shape[-1] // 2
    return jnp.concatenate([-w[..., half:], w[..., :half]], axis=-1)


def kernel(x, c, ctx, c_ctx, w_mod, b_mod, g_pre_mix, g_post_mix, g_pre_ffn, g_post_ffn,
           w_in_a, mla_g_q, mla_w_uq, mla_g_kv, mla_w_ukv,
           gla_w_gate_f, gla_b_gate_f, gla_w_gate_b, gla_b_gate_b, gla_g_out, w_out_a,
           w_qkv_c, diff_lq1, diff_lk1, diff_lq2, diff_lk2, diff_g_out, w_out_c,
           ffn_w_gate, ffn_w_up, ffn_w_down,
           moe_w_router, moe_w_gate, moe_w_up, moe_w_down):
    bsz, t, d = x.shape
    tx = ctx.shape[1]
    n, nx = bsz * t, bsz * tx
    xl = x.reshape(n, d)
    xc = ctx.reshape(nx, d)

    rows = -(-(bsz + 1) // 8) * 8
    c_all = jnp.zeros((rows, d), F32).at[:bsz].set(c).at[bsz].set(c_ctx)
    mod_all = _modulation(c_all, w_mod, b_mod).reshape(2, rows, 6, d)

    def mods(i):
        lat = [mod_all[i, :bsz, k].reshape(bsz, 1, d) for k in range(6)]
        cx = [mod_all[i, bsz, k].reshape(1, 1, d) for k in range(6)]
        return lat, cx

    row2 = lambda v: v.reshape(1, -1)

    ml, mx = mods(0)
    wi = w_in_a[0]
    cq, ckv, kr, gq, gk, gv, gr, af, ab = jnp.split(
        wi, [256, 384, 448, 704, 960, 1472, 1984, 2000], axis=-1)
    w_in = jnp.concatenate([cq, ckv, kr, _rot_cols(kr), gq, gk, gv, gr, af, ab,
                            jnp.zeros((d, LANES - 2 * GLA_RANK), F32)], axis=-1).astype(BF16)
    uq = mla_w_uq[0].reshape(MLA_Q_RANK, MLA_HEADS, MLA_NOPE + MLA_ROPE)
    wq = jnp.concatenate([uq[..., :MLA_NOPE], uq[..., MLA_NOPE:], _rot_cols(uq[..., MLA_NOPE:])],
                         axis=-1).reshape(MLA_Q_RANK, MLA_HEADS * 2 * LANES).astype(BF16)
    ukv = mla_w_ukv[0].reshape(MLA_KV_RANK, MLA_HEADS, MLA_NOPE + MLA_V)
    wkv = jnp.concatenate([ukv[..., :MLA_NOPE].reshape(MLA_KV_RANK, -1),
                           ukv[..., MLA_NOPE:].reshape(MLA_KV_RANK, -1)], axis=-1).astype(BF16)
    cos_a, sin_a = _rope_angles(t, MLA_ROPE)
    zpad = jnp.zeros((t, LANES - MLA_ROPE), F32)
    c2 = jnp.concatenate([cos_a, cos_a, zpad], axis=-1)
    s2 = jnp.concatenate([sin_a, sin_a, zpad], axis=-1)
    lane = jnp.arange(LANES)
    c2x = jnp.broadcast_to(jnp.where(lane < MLA_ROPE, 1.0, 0.0).astype(F32), (tx, LANES))
    s2x = jnp.zeros((tx, LANES), F32)

    proj_l = _norm_proj(xl, row2(g_pre_mix[0]), ml[0], ml[1], w_in, t, 512, F32, "in_proj_a")
    proj_x = _norm_proj(xc, row2(g_pre_mix[0]), mx[0], mx[1], w_in, nx, 512, F32, "in_proj_a_ctx")
    gq_, gkv_ = row2(mla_g_q[0]), row2(mla_g_kv[0])
    q_l, k_l, v_l = _mla_prep(proj_l, gq_, gkv_, wq, wkv, c2, s2, 512, t // 512)
    q_x, k_x, v_x = _mla_prep(proj_x, gq_, gkv_, wq, wkv, c2x, s2x, tx, 0)
    a_x = _attention(q_x, [k_x], [v_x], bsz, MLA_HEADS, 2 * LANES, MLA_V, tx, tx, "mla_attention_ctx")

    hk = GLA_HEADS * GLA_DK
    hv = GLA_HEADS * GLA_DV
    wf = jnp.zeros((LANES, hk), F32).at[:GLA_RANK].set(gla_w_gate_f[0]).astype(BF16)
    wb = jnp.zeros((LANES, hk), F32).at[GLA_RANK:2 * GLA_RANK].set(gla_w_gate_b[0]).astype(BF16)
    lvl_f, lvl_b = _gla_level_maps()
    s0 = jnp.zeros((bsz, hv, hk), F32)
    gla_args = (wf, row2(gla_b_gate_f[0]), wb, row2(gla_b_gate_b[0]), lvl_f, lvl_b)
    ox_f, ox_b, sx_f, sx_b = _gla_scan(proj_x, *gla_args, s0, s0, bsz)
    a_l, ol_f, ol_b = _mla_gla(q_l, k_x, k_l, v_x, v_l, proj_l, *gla_args, sx_f, sx_b, bsz, ATTN_SUB)

    w_out = w_out_a[0].astype(BF16)
    g_o = row2(gla_g_out[0])
    wg, wu, wd = ffn_w_gate[0].astype(BF16), ffn_w_up[0].astype(BF16), ffn_w_down[0].astype(BF16)
    gpm, gpf, gqf = row2(g_post_mix[0]), row2(g_pre_ffn[0]), row2(g_post_ffn[0])
    tail_mod = lambda m: jnp.concatenate([m[2], m[3], m[4], m[5]], axis=1)
    xl = _mix_ffn(a_l, ol_f, ol_b, proj_l, g_o, xl, tail_mod(ml), gpm, w_out, gpf, gqf, wg, wu, wd, t, 512)
    xc = _mix_ffn(a_x, ox_f, ox_b, proj_x, g_o, xc, tail_mod(mx), gpm, w_out, gpf, gqf, wg, wu, wd, nx, 512)

    ml, mx = mods(1)
    lam_init = 0.8 - 0.6 * math.exp(-0.3 * 1)
    w_qkv = w_qkv_c[0].astype(BF16)
    width = DIFF_HEADS * 2 * DIFF_DIM
    cos_c, sin_c = _rope_angles(t, DIFF_DIM)
    z32 = jnp.zeros_like(sin_c)
    cc = jnp.concatenate([cos_c] * 4, axis=-1)
    sa = jnp.concatenate([-sin_c, z32, -sin_c, z32], axis=-1)
    sb = jnp.concatenate([z32, sin_c, z32, sin_c], axis=-1)
    q1, k1, v1 = _diff_prep(xl, row2(g_pre_mix[1]), ml[0], ml[1], w_qkv, cc, sa, sb, t, 512, t // 512)
    kv_x = _norm_proj(xc, row2(g_pre_mix[1]), mx[0], mx[1], w_qkv[:, width:], nx, 512, BF16, "diff_kv_ctx")
    lamvec = jnp.zeros((8, DIFF_DIM), F32).at[0].set(diff_lq1[0]).at[1].set(diff_lk1[0]) \
        .at[2].set(diff_lq2[0]).at[3].set(diff_lk2[0])
    a1, (moe_wg, moe_wu, moe_wd) = _diff_attention(
        q1, kv_x, k1, kv_x, v1, lamvec, row2(diff_g_out[0]), bsz, min(t, ATTN_ROWS), ATTN_SUB, lam_init,
        [moe_w_gate[0], moe_w_up[0], moe_w_down[0]])
    xl = _out_proj_c(a1, xl, ml[2], row2(g_post_mix[1]), w_out_c[0].astype(BF16), t, 512)

    xl = _moe(xl, row2(g_pre_ffn[1]), ml[3], ml[4], ml[5], row2(g_post_ffn[1]), moe_w_router[0],
              moe_wg, moe_wu, moe_wd, t)
    return xl.reshape(bsz, t, d)
```

```python
import functools
import math

import jax
import jax.numpy as jnp
from jax import lax
from jax.experimental import pallas as pl
from jax.experimental.pallas import tpu as pltpu

F32 = jnp.float32
BF16 = jnp.bfloat16
I32 = jnp.int32

EPS = 1e-6
ROPE_BASE = 10000.0
GRID_W = 64

D_MODEL = 1024
MLA_HEADS = 4
MLA_Q_RANK = 256
MLA_KV_RANK = 128
MLA_NOPE = 128
MLA_ROPE = 64
MLA_V = 128
GLA_HEADS = 4
GLA_DK = 64
GLA_DV = 128
GLA_RANK = 16
GLA_TAU = 16.0
DIFF_HEADS = 8
DIFF_DIM = 64
N_EXPERTS = 8
LANES = 128
GLA_CHUNK = 128
GLA_LEVELS = (64, 32, 16, 8, 4, 2, 1)
MOE_ROWS = 512
ATTN_ROWS = 2048
ATTN_SUB = 256
VMEM_LIMIT = 56 * 1024 * 1024

LOG2E = math.log2(math.e)
_NT = (((1,), (1,)), ((), ()))


def _cparams(sem):
    return pltpu.CompilerParams(dimension_semantics=sem, vmem_limit_bytes=VMEM_LIMIT)


def _rms(xf, g):
    return xf * lax.rsqrt(jnp.mean(xf * xf, axis=-1, keepdims=True) + EPS) * g


def _normmod(x, g, shift, scale):
    return _rms(x.astype(F32), g) * (1.0 + scale) + shift


def _silu(x):
    return x * (1.0 / (1.0 + jnp.exp(-x)))


def _const_spec(shape):
    n = len(shape)
    return pl.BlockSpec(shape, lambda *_: (0,) * n)


def _mod_kernel(c_ref, w_ref, b_ref, o_ref):
    s = _silu(c_ref[...])
    o_ref[0] = jnp.dot(s.astype(BF16), w_ref[0].astype(BF16), preferred_element_type=F32) + b_ref[0]


def _modulation(c_all, w_mod, b_mod):
    depth, d, n6 = w_mod.shape
    rows = c_all.shape[0]
    tn = 1536
    return pl.pallas_call(
        _mod_kernel,
        grid=(depth, n6 // tn),
        in_specs=[pl.BlockSpec((rows, d), lambda i, j: (0, 0)),
                  pl.BlockSpec((1, d, tn), lambda i, j: (i, 0, j)),
                  pl.BlockSpec((1, 1, tn), lambda i, j: (i, 0, j))],
        out_specs=pl.BlockSpec((1, rows, tn), lambda i, j: (i, 0, j)),
        out_shape=jax.ShapeDtypeStruct((depth, rows, n6), F32),
        compiler_params=_cparams(("arbitrary", "arbitrary")),
        name="modulation",
    )(c_all, w_mod, b_mod.reshape(depth, 1, n6))


def _proj_kernel(x_ref, g_ref, sh_ref, sc_ref, w_ref, o_ref):
    h = _normmod(x_ref[...], g_ref[...], sh_ref[0], sc_ref[0])
    o_ref[...] = jnp.dot(h.astype(BF16), w_ref[...], preferred_element_type=F32).astype(o_ref.dtype)


def _norm_proj(x, g, shift, scale, w, rows_per_mod, tm, out_dtype, name):
    n, d = x.shape
    nout = w.shape[1]
    mod_idx = lambda i: ((i * tm) // rows_per_mod, 0, 0)
    return pl.pallas_call(
        _proj_kernel,
        grid=(n // tm,),
        in_specs=[pl.BlockSpec((tm, d), lambda i: (i, 0)),
                  _const_spec((1, d)),
                  pl.BlockSpec((1, 1, d), mod_idx),
                  pl.BlockSpec((1, 1, d), mod_idx),
                  _const_spec((d, nout))],
        out_specs=pl.BlockSpec((tm, nout), lambda i: (i, 0)),
        out_shape=jax.ShapeDtypeStruct((n, nout), out_dtype),
        compiler_params=_cparams(("arbitrary",)),
        name=name,
    )(x, g, shift, scale, w)


def _mla_prep_kernel(cq_ref, ckv_ref, kr_ref, gq_ref, gkv_ref, wq_ref, wkv_ref, c2_ref, s2_ref,
                     q_ref, k_ref, v_ref, *, scale):
    c2 = c2_ref[...]
    s2 = s2_ref[...]

    def rope2(v):
        return v * c2 + pltpu.roll(v, 64, 1) * s2

    q = jnp.dot(_rms(cq_ref[...], gq_ref[...]).astype(BF16), wq_ref[...], preferred_element_type=F32)
    kv = jnp.dot(_rms(ckv_ref[...], gkv_ref[...]).astype(BF16), wkv_ref[...], preferred_element_type=F32)
    krope = rope2(kr_ref[...]).astype(BF16)
    for h in range(MLA_HEADS):
        b = h * 2 * LANES
        q_ref[:, b:b + LANES] = (q[:, b:b + LANES] * scale).astype(BF16)
        q_ref[:, b + LANES:b + 2 * LANES] = (rope2(q[:, b + LANES:b + 2 * LANES]) * scale).astype(BF16)
        k_ref[:, b:b + LANES] = kv[:, h * LANES:(h + 1) * LANES].astype(BF16)
        k_ref[:, b + LANES:b + 2 * LANES] = krope
    v_ref[...] = kv[:, MLA_HEADS * LANES:].astype(BF16)


def _mla_prep(proj, gq, gkv, wq, wkv, c2, s2, tm, rope_blocks):
    n = proj.shape[0]
    hq = MLA_HEADS * 2 * LANES
    rope_idx = (lambda i: (i % rope_blocks, 0)) if rope_blocks else (lambda i: (0, 0))
    return pl.pallas_call(
        functools.partial(_mla_prep_kernel, scale=float((MLA_NOPE + MLA_ROPE) ** -0.5) * LOG2E),
        grid=(n // tm,),
        in_specs=[pl.BlockSpec((tm, 256), lambda i: (i, 0)),
                  pl.BlockSpec((tm, 128), lambda i: (i, 2)),
                  pl.BlockSpec((tm, 128), lambda i: (i, 3)),
                  _const_spec((1, MLA_Q_RANK)), _const_spec((1, MLA_KV_RANK)),
                  _const_spec(wq.shape), _const_spec(wkv.shape),
                  pl.BlockSpec((tm, LANES), rope_idx), pl.BlockSpec((tm, LANES), rope_idx)],
        out_specs=[pl.BlockSpec((tm, hq), lambda i: (i, 0)),
                   pl.BlockSpec((tm, hq), lambda i: (i, 0)),
                   pl.BlockSpec((tm, MLA_HEADS * MLA_V), lambda i: (i, 0))],
        out_shape=[jax.ShapeDtypeStruct((n, hq), BF16), jax.ShapeDtypeStruct((n, hq), BF16),
                   jax.ShapeDtypeStruct((n, MLA_HEADS * MLA_V), BF16)],
        compiler_params=_cparams(("arbitrary",)),
        name="mla_prep",
    )(proj, proj, proj, gq, gkv, wq, wkv, c2, s2)


def _fill_kv(k_refs, v_refs, k_scr, v_scr):
    r0 = 0
    dv = v_refs[0].shape[1]
    for k_ref, v_ref in zip(k_refs, v_refs):
        r1 = r0 + k_ref.shape[0]
        k_scr[r0:r1, :] = k_ref[...]
        v_scr[r0:r1, 0:dv] = v_ref[...]
        r0 = r1
    v_scr[:, dv:] = jnp.ones((v_scr.shape[0], v_scr.shape[1] - dv), v_scr.dtype)


def _softmax_pv(q, k, v1, dv):
    s = lax.dot_general(q, k, _NT, preferred_element_type=F32)
    p = jnp.exp2(s - jnp.max(s, axis=-1, keepdims=True)).astype(BF16)
    o = jnp.dot(p, v1, preferred_element_type=F32)
    return o[:, 0:dv] / o[:, dv:dv + 1]


def _attn_kernel(*refs, n_src, sub):
    q_ref = refs[0]
    k_refs = refs[1:1 + n_src]
    v_refs = refs[1 + n_src:1 + 2 * n_src]
    o_ref, k_scr, v_scr = refs[1 + 2 * n_src:]

    @pl.when(pl.program_id(2) == 0)
    def _():
        _fill_kv(k_refs, v_refs, k_scr, v_scr)

    k = k_scr[...]
    v1 = v_scr[...]
    dv = o_ref.shape[1]
    for r0 in range(0, q_ref.shape[0], sub):
        o_ref[r0:r0 + sub, :] = _softmax_pv(q_ref[r0:r0 + sub, :], k, v1, dv).astype(o_ref.dtype)


def _attention(q, ks, vs, batch, heads, dq, dv, tq, sub, name):
    nq = q.shape[0] // batch // tq
    n_src = len(ks)
    tk = sum(k.shape[0] for k in ks) // batch
    in_specs = [pl.BlockSpec((tq, dq), lambda b, h, i: (b * nq + i, h))]
    for k in ks:
        in_specs.append(pl.BlockSpec((k.shape[0] // batch, dq), lambda b, h, i: (b, h)))
    for v in vs:
        in_specs.append(pl.BlockSpec((v.shape[0] // batch, dv), lambda b, h, i: (b, h)))
    return pl.pallas_call(
        functools.partial(_attn_kernel, n_src=n_src, sub=sub),
        grid=(batch, heads, nq),
        in_specs=in_specs,
        out_specs=pl.BlockSpec((tq, dv), lambda b, h, i: (b * nq + i, h)),
        out_shape=jax.ShapeDtypeStruct((q.shape[0], heads * dv), BF16),
        scratch_shapes=[pltpu.VMEM((tk, dq), BF16), pltpu.VMEM((tk, 2 * dv), BF16)],
        compiler_params=_cparams(("arbitrary", "arbitrary", "arbitrary")),
        name=name,
    )(q, *ks, *vs)


def _log_sigmoid(z):
    return jnp.minimum(z, 0.0) - jnp.log(1.0 + jnp.exp(-jnp.abs(z)))


def _gla_anchor(cum, level, rev):
    c, hk = cum.shape
    two = 2 * level
    a = level if rev else level - 1
    if two >= 8:
        return jnp.concatenate(
            [jnp.broadcast_to(cum[b * two + a:b * two + a + 1], (two, hk)) for b in range(c // two)], axis=0)
    pos = lax.broadcasted_iota(I32, cum.shape, 0) & (two - 1)
    anc = cum
    for p in range(two):
        if p != a:
            anc = jnp.where(pos == p, pltpu.roll(cum, (p - a) % c, 0), anc)
    return anc


def _gla_chunks(streams):
    c = GLA_CHUNK
    hk = GLA_HEADS * GLA_DK
    ri = lax.broadcasted_iota(I32, (c, c), 0)
    ci = lax.broadcasted_iota(I32, (c, c), 1)
    row = lax.broadcasted_iota(I32, (c, hk), 0)
    pair_lane = lax.broadcasted_iota(I32, (c, LANES), 1)
    keep_first = jnp.where(pair_lane < GLA_DK, 1.0, 0.0).astype(BF16)
    keep_second = jnp.where(pair_lane < GLA_DK, 0.0, 1.0).astype(BF16)

    def head_grams(zb, keys):
        outs = []
        for p in range(hk // LANES):
            kp = keys[:, p * LANES:(p + 1) * LANES]
            rhs = jnp.concatenate([kp * keep_first, kp * keep_second], axis=0)
            outs.append(lax.dot_general(zb[:, p * LANES:(p + 1) * LANES], rhs, _NT, preferred_element_type=F32))
        return jnp.concatenate(outs, axis=1)

    def cumulative(la, rev):
        tri = jnp.where((ci >= ri) if rev else (ci <= ri), 1.0, 0.0).astype(BF16)
        la_hi = la.astype(BF16)
        rest = la - la_hi.astype(F32)
        la_mid = rest.astype(BF16)
        la_lo = (rest - la_mid.astype(F32)).astype(BF16)
        return (jnp.dot(tri, la_hi, preferred_element_type=F32) + jnp.dot(tri, la_mid, preferred_element_type=F32)
                + jnp.dot(tri, la_lo, preferred_element_type=F32))

    cums = [cumulative(la, rev) for (_, _, _, la, _, _, rev) in streams]
    qss = [q * (GLA_DK ** -0.5) for (q, *_) in streams]
    vbs = [v.astype(BF16) for (_, _, v, *_) in streams]

    o_inter = []
    for (q, k, v, la, s_ref, lvl, rev), cum, qs in zip(streams, cums, qss):
        tot = cum[0:1] if rev else cum[c - 1:c]
        s_old = s_ref[...]
        qh = (qs * jnp.exp(cum)).astype(BF16)
        o_inter.append(lax.dot_general(qh, s_old.astype(BF16), _NT, preferred_element_type=F32))
        kh = (k * jnp.exp(tot - cum)).astype(BF16)
        u = jnp.dot(v.T.astype(BF16), kh, preferred_element_type=F32)
        bd = ((lax.broadcasted_iota(I32, u.shape, 0) // GLA_DV)
              == (lax.broadcasted_iota(I32, u.shape, 1) // GLA_DK))
        s_ref[...] = s_old * jnp.exp(tot) + jnp.where(bd, u, 0.0)

    ones = jnp.ones((c, hk), BF16)
    atts = [jnp.where(lvl == 0, head_grams((qs * k).astype(BF16), ones), 0.0)
            for (q, k, v, la, s_ref, lvl, rev), qs in zip(streams, qss)]
    for level in GLA_LEVELS:
        upper = (row & level) != 0
        for i, ((q, k, v, la, s_ref, lvl, rev), cum, qs) in enumerate(zip(streams, cums, qss)):
            is_q = jnp.logical_not(upper) if rev else upper
            x = cum - _gla_anchor(cum, level, rev)
            zb = (jnp.where(is_q, qs, k) * jnp.exp(jnp.where(is_q, x, -x))).astype(BF16)
            atts[i] = jnp.where(lvl == level, head_grams(zb, zb), atts[i])
    outs = []
    for att, vb, oi in zip(atts, vbs, o_inter):
        attb = att.astype(BF16)
        outs.append(oi + jnp.concatenate(
            [jnp.dot(attb[:, h * c:(h + 1) * c], vb[:, h * GLA_DV:(h + 1) * GLA_DV], preferred_element_type=F32)
             for h in range(GLA_HEADS)], axis=1))
    return outs


def _gla_kernel(qf_ref, kf_ref, vf_ref, gf_ref, qb_ref, kb_ref, vb_ref, gb_ref,
                wf_ref, bf_ref, wb_ref, bb_ref, lvlf_ref, lvlb_ref, s0f_ref, s0b_ref,
                of_ref, ob_ref, sf_ref, sb_ref, stf, stb):
    j = pl.program_id(1)

    @pl.when(j == 0)
    def _():
        stf[...] = s0f_ref[0]
        stb[...] = s0b_ref[0]

    def log_decay(g_ref, w_ref, b_ref):
        z = jnp.dot(g_ref[...].astype(BF16), w_ref[...], preferred_element_type=F32) + b_ref[...]
        return _log_sigmoid(z) * (1.0 / GLA_TAU)

    o_f, o_b = _gla_chunks([
        (qf_ref[...], kf_ref[...], vf_ref[...], log_decay(gf_ref, wf_ref, bf_ref), stf, lvlf_ref[...], False),
        (qb_ref[...], kb_ref[...], vb_ref[...], log_decay(gb_ref, wb_ref, bb_ref), stb, lvlb_ref[...], True)])
    of_ref[...] = o_f
    ob_ref[...] = o_b

    @pl.when(j == pl.num_programs(1) - 1)
    def _():
        sf_ref[0] = stf[...]
        sb_ref[0] = stb[...]


def _gla_level_maps():
    c = GLA_CHUNK
    t = jnp.arange(c, dtype=I32)[:, None]
    s = jnp.arange(c, dtype=I32)[None, :]
    diff = t ^ s
    top = jnp.zeros((c, c), I32)
    for level in GLA_LEVELS:
        top = jnp.where((top == 0) & ((diff & level) != 0), level, top)
    fwd = jnp.where(t == s, 0, jnp.where(t > s, top, -1))
    bwd = jnp.where(t == s, 0, jnp.where(t < s, top, -1))
    return jnp.tile(fwd, (1, GLA_HEADS)), jnp.tile(bwd, (1, GLA_HEADS))


def _gla_scan(proj, wf, bf, wb, bb, lvl_f, lvl_b, s0f, s0b, batch):
    n = proj.shape[0]
    c = GLA_CHUNK
    nch = n // batch // c
    hk = GLA_HEADS * GLA_DK
    hv = GLA_HEADS * GLA_DV
    fwd = lambda b, j: b * nch + j
    bwd = lambda b, j: b * nch + (nch - 1 - j)

    def specs(row):
        return [pl.BlockSpec((c, hk), lambda b, j: (row(b, j), 2)),
                pl.BlockSpec((c, hk), lambda b, j: (row(b, j), 3)),
                pl.BlockSpec((c, hv), lambda b, j: (row(b, j), 2)),
                pl.BlockSpec((c, LANES), lambda b, j: (row(b, j), 16))]

    st_spec = pl.BlockSpec((1, hv, hk), lambda b, j: (b, 0, 0))
    return pl.pallas_call(
        _gla_kernel,
        grid=(batch, nch),
        in_specs=specs(fwd) + specs(bwd) + [
            _const_spec((LANES, hk)), _const_spec((1, hk)), _const_spec((LANES, hk)), _const_spec((1, hk)),
            _const_spec(lvl_f.shape), _const_spec(lvl_b.shape), st_spec, st_spec],
        out_specs=[pl.BlockSpec((c, hv), lambda b, j: (fwd(b, j), 0)),
                   pl.BlockSpec((c, hv), lambda b, j: (bwd(b, j), 0)),
                   st_spec, st_spec],
        out_shape=[jax.ShapeDtypeStruct((n, hv), F32), jax.ShapeDtypeStruct((n, hv), F32),
                   jax.ShapeDtypeStruct((batch, hv, hk), F32), jax.ShapeDtypeStruct((batch, hv, hk), F32)],
        scratch_shapes=[pltpu.VMEM((hv, hk), F32), pltpu.VMEM((hv, hk), F32)],
        compiler_params=_cparams(("arbitrary", "arbitrary")),
        name="gla_scan",
    )(proj, proj, proj, proj, proj, proj, proj, proj, wf, bf, wb, bb, lvl_f, lvl_b, s0f, s0b)


def _mla_gla_kernel(q_ref, kx_ref, kl_ref, vx_ref, vl_ref,
                    qf_ref, kf_ref, vf_ref, gf_ref, qb_ref, kb_ref, vb_ref, gb_ref,
                    wf_ref, bf_ref, wb_ref, bb_ref, lvlf_ref, lvlb_ref, s0f_ref, s0b_ref,
                    a_ref, of_ref, ob_ref, k_scr, v_scr, stf, stb, *, sub):
    @pl.when(pl.program_id(1) == 0)
    def _():
        stf[...] = s0f_ref[0]
        stb[...] = s0b_ref[0]

    _fill_kv((kx_ref, kl_ref), (vx_ref, vl_ref), k_scr, v_scr)
    k = k_scr[...]
    v1 = v_scr[...]
    dv = a_ref.shape[1]
    lvl_f = lvlf_ref[...]
    lvl_b = lvlb_ref[...]
    c = GLA_CHUNK
    groups = qf_ref.shape[0] // c
    chains = q_ref.shape[0] // sub // groups

    def log_decay(g, w_ref, b_ref):
        z = jnp.dot(g.astype(BF16), w_ref[...], preferred_element_type=F32) + b_ref[...]
        return _log_sigmoid(z) * (1.0 / GLA_TAU)

    for g in range(groups):
        for r0 in range(g * chains * sub, (g + 1) * chains * sub, sub):
            a_ref[r0:r0 + sub, :] = _softmax_pv(q_ref[r0:r0 + sub, :], k, v1, dv).astype(a_ref.dtype)
        rf = slice(g * c, (g + 1) * c)
        rb = slice((groups - 1 - g) * c, (groups - g) * c)
        o_f, o_b = _gla_chunks([
            (qf_ref[rf, :], kf_ref[rf, :], vf_ref[rf, :], log_decay(gf_ref[rf, :], wf_ref, bf_ref),
             stf, lvl_f, False),
            (qb_ref[rb, :], kb_ref[rb, :], vb_ref[rb, :], log_decay(gb_ref[rb, :], wb_ref, bb_ref),
             stb, lvl_b, True)])
        of_ref[rf, :] = o_f
        ob_ref[rb, :] = o_b


def _mla_gla(q, kx, kl, vx, vl, proj, wf, bf, wb, bb, lvl_f, lvl_b, s0f, s0b, batch, sub):
    n = q.shape[0]
    heads = MLA_HEADS
    t = n // batch
    tx = kx.shape[0] // batch
    dq, dv = 2 * LANES, MLA_V
    hk = GLA_HEADS * GLA_DK
    hv = GLA_HEADS * GLA_DV
    rows = t // heads
    fwd = lambda b, h: b * heads + h
    bwd = lambda b, h: b * heads + (heads - 1 - h)

    def scan_specs(row):
        return [pl.BlockSpec((rows, hk), lambda b, h: (row(b, h), 2)),
                pl.BlockSpec((rows, hk), lambda b, h: (row(b, h), 3)),
                pl.BlockSpec((rows, hv), lambda b, h: (row(b, h), 2)),
                pl.BlockSpec((rows, LANES), lambda b, h: (row(b, h), 16))]

    st_spec = pl.BlockSpec((1, hv, hk), lambda b, h: (b, 0, 0))
    return pl.pallas_call(
        functools.partial(_mla_gla_kernel, sub=sub),
        grid=(batch, heads),
        in_specs=[pl.BlockSpec((t, dq), lambda b, h: (b, h)),
                  pl.BlockSpec((tx, dq), lambda b, h: (b, h)),
                  pl.BlockSpec((t, dq), lambda b, h: (b, h)),
                  pl.BlockSpec((tx, dv), lambda b, h: (b, h)),
                  pl.BlockSpec((t, dv), lambda b, h: (b, h))]
        + scan_specs(fwd) + scan_specs(bwd) + [
            _const_spec((LANES, hk)), _const_spec((1, hk)), _const_spec((LANES, hk)), _const_spec((1, hk)),
            _const_spec(lvl_f.shape), _const_spec(lvl_b.shape), st_spec, st_spec],
        out_specs=[pl.BlockSpec((t, dv), lambda b, h: (b, h)),
                   pl.BlockSpec((rows, hv), lambda b, h: (fwd(b, h), 0)),
                   pl.BlockSpec((rows, hv), lambda b, h: (bwd(b, h), 0))],
        out_shape=[jax.ShapeDtypeStruct((n, heads * dv), BF16),
                   jax.ShapeDtypeStruct((n, hv), F32), jax.ShapeDtypeStruct((n, hv), F32)],
        scratch_shapes=[pltpu.VMEM((tx + t, dq), BF16), pltpu.VMEM((tx + t, 2 * dv), BF16),
                        pltpu.VMEM((hv, hk), F32), pltpu.VMEM((hv, hk), F32)],
        compiler_params=_cparams(("arbitrary", "arbitrary")),
        name="mla_gla",
    )(q, kx, kl, vx, vl, proj, proj, proj, proj, proj, proj, proj, proj, wf, bf, wb, bb, lvl_f, lvl_b, s0f, s0b)


def _swiglu_residual(x, h, gate, g_post, wg_ref, wu_ref, wd_ref, fc):
    acc = jnp.zeros(x.shape, F32)
    for c0 in range(0, wg_ref.shape[1], fc):
        g = jnp.dot(h, wg_ref[:, c0:c0 + fc], preferred_element_type=F32)
        u = jnp.dot(h, wu_ref[:, c0:c0 + fc], preferred_element_type=F32)
        act = (_silu(g) * u).astype(BF16)
        acc = acc + jnp.dot(act, wd_ref[c0:c0 + fc, :], preferred_element_type=F32)
    return x + gate * _rms(acc, g_post)


def _mix_ffn_kernel(a_ref, of_ref, ob_ref, r_ref, go_ref, x_ref, mod_ref, gpm_ref, wo_ref,
                    gpre_ref, gpost_ref, wg_ref, wu_ref, wd_ref, o_ref, *, fc):
    o = of_ref[...] + ob_ref[...]
    r = r_ref[...]
    go = go_ref[...]
    parts = []
    for h in range(GLA_HEADS):
        sl = slice(h * GLA_DV, (h + 1) * GLA_DV)
        parts.append(_rms(o[:, sl], go) * _silu(r[:, sl]))
    g = jnp.concatenate(parts, axis=-1).astype(BF16)
    na = a_ref.shape[1]
    y = (jnp.dot(a_ref[...], wo_ref[0:na, :], preferred_element_type=F32)
         + jnp.dot(g, wo_ref[na:, :], preferred_element_type=F32))
    mod = mod_ref[0]
    x = x_ref[...] + mod[0:1] * _rms(y, gpm_ref[...])
    h = _normmod(x, gpre_ref[...], mod[1:2], mod[2:3]).astype(BF16)
    o_ref[...] = _swiglu_residual(x, h, mod[3:4], gpost_ref[...], wg_ref, wu_ref, wd_ref, fc)


def _mix_ffn(a, o_f, o_b, proj, g_o, x, mod, g_post_mix, w_out, g_pre, g_post, wg, wu, wd, rows_per_mod, tm):
    n, d = x.shape
    hv = GLA_HEADS * GLA_DV
    row = lambda i: (i, 0)
    single = pl.Buffered(1)
    resident = lambda w: pl.BlockSpec(w.shape, lambda i: (0, 0), pipeline_mode=single)
    return pl.pallas_call(
        functools.partial(_mix_ffn_kernel, fc=256),
        grid=(n // tm,),
        in_specs=[pl.BlockSpec((tm, a.shape[1]), row),
                  pl.BlockSpec((tm, hv), row),
                  pl.BlockSpec((tm, hv), row),
                  pl.BlockSpec((tm, hv), lambda i: (i, 3)),
                  _const_spec((1, GLA_DV)),
                  pl.BlockSpec((tm, d), row),
                  pl.BlockSpec((1, mod.shape[1], d), lambda i: ((i * tm) // rows_per_mod, 0, 0)),
                  _const_spec((1, d)), resident(w_out),
                  _const_spec((1, d)), _const_spec((1, d)), resident(wg), resident(wu), resident(wd)],
        out_specs=pl.BlockSpec((tm, d), row),
        out_shape=jax.ShapeDtypeStruct((n, d), F32),
        compiler_params=_cparams(("arbitrary",)),
        name="mix_ffn",
    )(a, o_f, o_b, proj, g_o, x, mod, g_post_mix, w_out, g_pre, g_post, wg, wu, wd)


def _diff_prep_kernel(x_ref, g_ref, sh_ref, sc_ref, w_ref, c_ref, sa_ref, sb_ref, q_ref, k_ref, v_ref):
    h = _normmod(x_ref[...], g_ref[...], sh_ref[0], sc_ref[0]).astype(BF16)
    qkv = jnp.dot(h, w_ref[...], preferred_element_type=F32)
    cc, sa, sb = c_ref[...], sa_ref[...], sb_ref[...]
    width = q_ref.shape[1]

    def rope(v):
        return v * cc + pltpu.roll(v, 96, 1) * sa + pltpu.roll(v, 32, 1) * sb

    for j in range(width // LANES):
        sl = slice(j * LANES, (j + 1) * LANES)
        q_ref[:, sl] = (rope(qkv[:, sl]) * (DIFF_DIM ** -0.5 * LOG2E)).astype(BF16)
        k_ref[:, sl] = rope(qkv[:, width + j * LANES:width + (j + 1) * LANES]).astype(BF16)
    v_ref[...] = qkv[:, 2 * width:].astype(BF16)


def _diff_prep(x, g, shift, scale, w, cc, sa, sb, rows_per_mod, tm, rope_blocks):
    n, d = x.shape
    width = w.shape[1] // 3
    mod_idx = lambda i: ((i * tm) // rows_per_mod, 0, 0)
    rope_spec = pl.BlockSpec((tm, LANES), lambda i: (i % rope_blocks, 0))
    out_spec = pl.BlockSpec((tm, width), lambda i: (i, 0))
    return pl.pallas_call(
        _diff_prep_kernel,
        grid=(n // tm,),
        in_specs=[pl.BlockSpec((tm, d), lambda i: (i, 0)), _const_spec((1, d)),
                  pl.BlockSpec((1, 1, d), mod_idx), pl.BlockSpec((1, 1, d), mod_idx),
                  _const_spec(w.shape), rope_spec, rope_spec, rope_spec],
        out_specs=[out_spec, out_spec, out_spec],
        out_shape=[jax.ShapeDtypeStruct((n, width), BF16)] * 3,
        compiler_params=_cparams(("arbitrary",)),
        name="diff_prep",
    )(x, g, shift, scale, w, cc, sa, sb)


def _diff_attn_kernel(*refs, lam_init, sub, n_cast):
    q_ref, kx_ref, kl_ref, vx_ref, vl_ref, lam_ref, go_ref = refs[:7]
    cast_in = refs[7:7 + n_cast]
    o_ref = refs[7 + n_cast]
    cast_out = refs[8 + n_cast:8 + 2 * n_cast]
    k_scr, v_scr = refs[8 + 2 * n_cast:]

    for w_ref, c_ref in zip(cast_in, cast_out):
        c_ref[...] = w_ref[...].astype(c_ref.dtype)

    @pl.when(pl.program_id(2) == 0)
    def _():
        _fill_kv((kx_ref, kl_ref), (vx_ref, vl_ref), k_scr, v_scr)

    lv = lam_ref[...]
    lam = (jnp.exp(jnp.sum(lv[0:1] * lv[1:2], axis=-1, keepdims=True))
           - jnp.exp(jnp.sum(lv[2:3] * lv[3:4], axis=-1, keepdims=True)) + lam_init)
    k = k_scr[...]
    v1 = v_scr[...]
    go = go_ref[...]
    lane = lax.broadcasted_iota(I32, (sub, LANES), 1)
    for r0 in range(0, q_ref.shape[0], sub):
        q = q_ref[r0:r0 + sub, :]
        zero = jnp.zeros_like(q)
        o = (_softmax_pv(jnp.where(lane < DIFF_DIM, q, zero), k, v1, LANES)
             - lam * _softmax_pv(jnp.where(lane >= DIFF_DIM, q, zero), k, v1, LANES))
        o_ref[r0:r0 + sub, :] = (_rms(o, go) * (1.0 - lam_init)).astype(o_ref.dtype)


def _diff_attention(q, kx, kl, vx, vl, lamvec, g_o, batch, tq, sub, lam_init, to_cast):
    n = q.shape[0]
    nq = n // batch // tq
    tx = kx.shape[0] // batch
    tl = kl.shape[0] // batch
    steps = batch * DIFF_HEADS * nq
    slabs = [w.reshape(steps, w.size // (steps * w.shape[-1]), w.shape[-1]) for w in to_cast]
    slab_spec = lambda s: pl.BlockSpec((1,) + s.shape[1:], lambda b, h, i: ((b * DIFF_HEADS + h) * nq + i, 0, 0))
    outs = pl.pallas_call(
        functools.partial(_diff_attn_kernel, lam_init=lam_init, sub=sub, n_cast=len(slabs)),
        grid=(batch, DIFF_HEADS, nq),
        in_specs=[pl.BlockSpec((tq, LANES), lambda b, h, i: (b * nq + i, h)),
                  pl.BlockSpec((tx, LANES), lambda b, h, i: (b, h)),
                  pl.BlockSpec((tl, LANES), lambda b, h, i: (b, h)),
                  pl.BlockSpec((tx, LANES), lambda b, h, i: (b, DIFF_HEADS + h)),
                  pl.BlockSpec((tl, LANES), lambda b, h, i: (b, h)),
                  _const_spec(lamvec.shape), _const_spec((1, LANES))] + [slab_spec(s) for s in slabs],
        out_specs=[pl.BlockSpec((tq, LANES), lambda b, h, i: (b * nq + i, h))] + [slab_spec(s) for s in slabs],
        out_shape=[jax.ShapeDtypeStruct((n, DIFF_HEADS * LANES), BF16)]
        + [jax.ShapeDtypeStruct(s.shape, BF16) for s in slabs],
        scratch_shapes=[pltpu.VMEM((tx + tl, LANES), BF16), pltpu.VMEM((tx + tl, 2 * LANES), BF16)],
        compiler_params=_cparams(("arbitrary", "arbitrary", "arbitrary")),
        name="diff_attention",
    )(q, kx, kl, vx, vl, lamvec, g_o, *slabs)
    return outs[0], [c.reshape(w.shape) for c, w in zip(outs[1:], to_cast)]


def _out_c_kernel(a_ref, x_ref, gate_ref, gp_ref, w_ref, o_ref):
    y = jnp.dot(a_ref[...], w_ref[...], preferred_element_type=F32)
    o_ref[...] = x_ref[...] + gate_ref[0] * _rms(y, gp_ref[...])


def _out_proj_c(a, x, gate, g_post, w, rows_per_mod, tm):
    n, d = x.shape
    return pl.pallas_call(
        _out_c_kernel,
        grid=(n // tm,),
        in_specs=[pl.BlockSpec((tm, a.shape[1]), lambda i: (i, 0)),
                  pl.BlockSpec((tm, d), lambda i: (i, 0)),
                  pl.BlockSpec((1, 1, d), lambda i: ((i * tm) // rows_per_mod, 0, 0)),
                  _const_spec((1, d)),
                  _const_spec(w.shape)],
        out_specs=pl.BlockSpec((tm, d), lambda i: (i, 0)),
        out_shape=jax.ShapeDtypeStruct((n, d), F32),
        compiler_params=_cparams(("arbitrary",)),
        name="out_proj_c",
    )(a, x, gate, g_post, w)


def _router_kernel(x_ref, g_ref, sh_ref, sc_ref, wr_ref, h_ref, meta_ref, gates_ref, cnt_ref, carry_ref):
    i = pl.program_id(0)

    @pl.when(i == 0)
    def _():
        carry_ref[...] = jnp.zeros_like(carry_ref)

    h = _normmod(x_ref[...], g_ref[...], sh_ref[0], sc_ref[0])
    h_ref[...] = h
    tm = h.shape[0]
    logits = jnp.dot(h, wr_ref[...], precision=lax.Precision.HIGHEST, preferred_element_type=F32)
    lane = lax.broadcasted_iota(I32, logits.shape, 1).astype(F32)
    neg = jnp.float32(-jnp.inf)
    logits = jnp.where(lane < N_EXPERTS, logits, neg)
    m0 = jnp.max(logits, axis=-1, keepdims=True)
    i0 = jnp.min(jnp.where(logits == m0, lane, float(LANES)), axis=-1, keepdims=True)
    rest = jnp.where(lane == i0, neg, logits)
    m1 = jnp.max(rest, axis=-1, keepdims=True)
    i1 = jnp.min(jnp.where(rest == m1, lane, float(LANES)), axis=-1, keepdims=True)
    e = jnp.exp(m1 - m0)
    g0 = 1.0 / (1.0 + e)
    g1 = e / (1.0 + e)
    hit = jnp.where(lane == i0, 1.0, jnp.where(lane == i1, 1.0, 0.0)).astype(F32)
    ri = lax.broadcasted_iota(I32, (tm, tm), 0)
    ci = lax.broadcasted_iota(I32, (tm, tm), 1)
    below = jnp.where(ci < ri, 1.0, 0.0).astype(BF16)
    prefix = jnp.dot(below, hit.astype(BF16), preferred_element_type=F32) + carry_ref[...]
    r0 = jnp.sum(jnp.where(lane == i0, prefix, 0.0), axis=-1, keepdims=True)
    r1 = jnp.sum(jnp.where(lane == i1, prefix, 0.0), axis=-1, keepdims=True)
    carry_ref[...] = carry_ref[...] + jnp.sum(hit, axis=0, keepdims=True)
    meta = jnp.where(lane == 0.0, i0, jnp.where(lane == 1.0, i1, jnp.where(lane == 2.0, r0, r1)))
    meta_ref[...] = meta.astype(I32)
    gates_ref[...] = jnp.where(lane == 0.0, g0, g1)
    cnt_ref[...] = carry_ref[...]


def _router(x, g, shift, scale, w_router, rows_per_mod, tm):
    n, d = x.shape
    mod_idx = lambda i: ((i * tm) // rows_per_mod, 0, 0)
    row = lambda i: (i, 0)
    return pl.pallas_call(
        _router_kernel,
        grid=(n // tm,),
        in_specs=[pl.BlockSpec((tm, d), row), _const_spec((1, d)),
                  pl.BlockSpec((1, 1, d), mod_idx), pl.BlockSpec((1, 1, d), mod_idx),
                  _const_spec(w_router.shape)],
        out_specs=[pl.BlockSpec((tm, d), row), pl.BlockSpec((tm, LANES), row), pl.BlockSpec((tm, LANES), row),
                   _const_spec((1, LANES))],
        out_shape=[jax.ShapeDtypeStruct((n, d), F32), jax.ShapeDtypeStruct((n, LANES), I32),
                   jax.ShapeDtypeStruct((n, LANES), F32), jax.ShapeDtypeStruct((1, LANES), F32)],
        scratch_shapes=[pltpu.VMEM((1, LANES), F32)],
        compiler_params=_cparams(("arbitrary",)),
        name="moe_router",
    )(x, g, shift, scale, w_router)


def _expert_stream_kernel(be_ref, nb_ref, idx_in_ref, idx_out_ref, h_hbm, wg_ref, wu_ref, wd_ref, y_hbm,
                          xbuf, ybuf, sem_in, sem_out, *, sub, nb):
    j = pl.program_id(0)
    bm = xbuf.shape[1]
    nbu = nb_ref[0]
    c = j - 1
    live = (c >= 0) & (c < nbu)
    gather_ok = j < nbu
    scatter_ok = (j >= 2) & (j - 2 < nbu)
    in_slot = j % 2
    cur = (j + 1) % 2

    def lane_of(r):
        return r % 2 if isinstance(r, int) else 0

    def start_in(r):
        t = idx_in_ref[0, 0, r]
        pltpu.make_async_copy(h_hbm.at[pl.ds(t, 1), :], xbuf.at[in_slot, pl.ds(r, 1), :],
                              sem_in.at[in_slot]).start(priority=lane_of(r))

    def start_out(r):
        t = idx_out_ref[0, 0, r]
        pltpu.make_async_copy(ybuf.at[in_slot, pl.ds(r, 1), :], y_hbm.at[pl.ds(t, 1), :],
                              sem_out.at[in_slot]).start(priority=lane_of(r))

    def wait_in(slot):
        pltpu.make_async_copy(h_hbm.at[pl.ds(0, bm), :], xbuf.at[slot], sem_in.at[slot]).wait()

    def wait_out(slot):
        pltpu.make_async_copy(ybuf.at[slot], y_hbm.at[pl.ds(0, bm), :], sem_out.at[slot]).wait()

    def loop(fn):
        def body(r, carry):
            fn(r)
            return carry
        lax.fori_loop(0, bm, body, 0, unroll=8)

    def expert(interleave):
        x = xbuf[cur].astype(BF16)
        f_dim = wg_ref.shape[2]
        n_chunks = f_dim // sub
        per = -(-bm // max(1, (2 * n_chunks) // 3))
        part = jnp.zeros((bm, wd_ref.shape[2]), F32)
        for ci in range(n_chunks):
            c0 = ci * sub
            g = jnp.dot(x, wg_ref[0, :, c0:c0 + sub], preferred_element_type=F32)
            u = jnp.dot(x, wu_ref[0, :, c0:c0 + sub], preferred_element_type=F32)
            act = (_silu(g) * u).astype(BF16)
            part = part + jnp.dot(act, wd_ref[0, c0:c0 + sub, :], preferred_element_type=F32)
            if interleave:
                for r in range(ci * per, min((ci + 1) * per, bm)):
                    start_in(r)
                    start_out(r)
        return part

    def store_result(part):
        @pl.when(j >= 3)
        def _():
            wait_out(cur)
        ybuf[cur] = part

    @pl.when(live)
    def _():
        wait_in(cur)

    steady = live & gather_ok & scatter_ok

    @pl.when(steady)
    def _():
        store_result(expert(True))

    @pl.when(jnp.logical_not(steady))
    def _():
        @pl.when(gather_ok)
        def _():
            loop(start_in)

        @pl.when(scatter_ok)
        def _():
            loop(start_out)

        @pl.when(live)
        def _():
            store_result(expert(False))

        @pl.when(jnp.logical_not(live) & (j >= 3) & (j - 3 < nbu))
        def _():
            wait_out(cur)

        @pl.when((c >= nbu) & (c < nb))
        def _():
            ybuf[cur] = jnp.zeros(ybuf.shape[1:], F32)
            fill = pltpu.make_async_copy(ybuf.at[cur], y_hbm.at[pl.ds(pl.multiple_of(c * bm, bm), bm), :],
                                         sem_out.at[cur])
            fill.start()
            fill.wait()

    @pl.when((j == nb + 1) & (nb - 1 < nbu))
    def _():
        wait_out(in_slot)


def _experts_stream(h, row_token, out_row, blk_e, nb_used, wg, wu, wd, n_out):
    n, d = h.shape
    bm = MOE_ROWS
    nb = row_token.shape[0] // bm
    single = pl.Buffered(1)

    def expert_of(j, be, nbu):
        return be[jnp.clip(j - 1, 0, nbu[0] - 1)]

    w_in_spec = pl.BlockSpec((1, d, wg.shape[2]), lambda j, be, nbu: (expert_of(j, be, nbu), 0, 0),
                             pipeline_mode=single)
    w_out_spec = pl.BlockSpec((1, wd.shape[1], d), lambda j, be, nbu: (expert_of(j, be, nbu), 0, 0),
                              pipeline_mode=single)
    return pl.pallas_call(
        functools.partial(_expert_stream_kernel, sub=256, nb=nb),
        grid_spec=pltpu.PrefetchScalarGridSpec(
            num_scalar_prefetch=2,
            grid=(nb + 2,),
            in_specs=[pl.BlockSpec((1, 1, bm), lambda j, be, nbu: (jnp.minimum(j, nb - 1), 0, 0),
                                   memory_space=pltpu.SMEM),
                      pl.BlockSpec((1, 1, bm), lambda j, be, nbu: (jnp.clip(j - 2, 0, nb - 1), 0, 0),
                                   memory_space=pltpu.SMEM),
                      pl.BlockSpec(memory_space=pl.ANY),
                      w_in_spec, w_in_spec, w_out_spec],
            out_specs=pl.BlockSpec(memory_space=pl.ANY),
            scratch_shapes=[pltpu.VMEM((2, bm, d), F32), pltpu.VMEM((2, bm, d), F32),
                            pltpu.SemaphoreType.DMA((2,)), pltpu.SemaphoreType.DMA((2,))]),
        out_shape=jax.ShapeDtypeStruct((n_out, d), F32),
        compiler_params=_cparams(("arbitrary",)),
        name="moe_experts",
    )(blk_e, nb_used, row_token.reshape(nb, 1, bm), out_row.reshape(nb, 1, bm), h, wg, wu, wd)


def _combine2_kernel(y0_ref, y1_ref, gates_ref, x_ref, gate_ref, gp_ref, o_ref):
    gt = gates_ref[...]
    f = y0_ref[...] * gt[:, 0:1] + y1_ref[...] * gt[:, 1:2]
    o_ref[...] = x_ref[...] + gate_ref[0] * _rms(f, gp_ref[...])


def _combine2(y, gates, x, gate, g_post, rows_per_mod, tm):
    n, d = x.shape
    nt = n // tm
    return pl.pallas_call(
        _combine2_kernel,
        grid=(nt,),
        in_specs=[pl.BlockSpec((tm, d), lambda i: (i, 0)),
                  pl.BlockSpec((tm, d), lambda i: (nt + i, 0)),
                  pl.BlockSpec((tm, LANES), lambda i: (i, 0)),
                  pl.BlockSpec((tm, d), lambda i: (i, 0)),
                  pl.BlockSpec((1, 1, d), lambda i: ((i * tm) // rows_per_mod, 0, 0)),
                  _const_spec((1, d))],
        out_specs=pl.BlockSpec((tm, d), lambda i: (i, 0)),
        out_shape=jax.ShapeDtypeStruct((n, d), F32),
        compiler_params=_cparams(("arbitrary",)),
        name="moe_combine",
    )(y, y, gates, x, gate, g_post)


def _invert_kernel(d0_ref, d1_ref, fill_hbm, info_ref, sem):
    i = pl.program_id(0)
    tb = d0_ref.shape[2]

    @pl.when(i == 0)
    def _():
        fill = pltpu.make_async_copy(fill_hbm, info_ref, sem)
        fill.start()
        fill.wait()

    base = 2 * i * tb

    def body(t, carry):
        code = base + 2 * t
        info_ref[d0_ref[0, 0, t]] = code
        info_ref[d1_ref[0, 0, t]] = code + 1
        return carry

    lax.fori_loop(0, tb, body, 0, unroll=8)


def _invert_routing(dest, cap, tb):
    n = dest.shape[0]
    nt = n // tb
    idx = pl.BlockSpec((1, 1, tb), lambda i: (i, 0, 0), memory_space=pltpu.SMEM)
    return pl.pallas_call(
        _invert_kernel,
        grid=(nt,),
        in_specs=[idx, idx, pl.BlockSpec(memory_space=pl.ANY)],
        out_specs=pl.BlockSpec(memory_space=pltpu.SMEM),
        out_shape=jax.ShapeDtypeStruct((cap,), I32),
        scratch_shapes=[pltpu.SemaphoreType.DMA(())],
        compiler_params=_cparams(("arbitrary",)),
        name="moe_invert",
    )(dest[:, 0].reshape(nt, 1, tb), dest[:, 1].reshape(nt, 1, tb), jnp.full((cap,), -1, I32))


def _moe(x, g_pre, shift, scale, gate, g_post, w_router, wg, wu, wd, rows_per_mod):
    n, d = x.shape
    bm = MOE_ROWS
    wr = jnp.zeros((d, LANES), F32).at[:, :N_EXPERTS].set(w_router)
    h, meta, gates, counts = _router(x, g_pre, shift, scale, wr, rows_per_mod, 512)
    cnt = counts[0, :N_EXPERTS].astype(I32)
    padded = (cnt + bm - 1) // bm * bm
    pend = jnp.cumsum(padded)
    pstart = pend - padded
    dest = pstart[meta[:, 0:2]] + meta[:, 2:4]
    cap = 2 * n + N_EXPERTS * bm
    nb = cap // bm
    info = _invert_routing(dest, cap, min(n, 8192))
    real = info >= 0
    row_token = jnp.where(real, info >> 1, 0)
    spill = 2 * n - 1 + jnp.cumsum(jnp.logical_not(real).astype(I32))
    out_row = jnp.where(real, (info & 1) * n + (info >> 1), spill)
    nb_used = (pend[-1] // bm).astype(I32).reshape(1)
    blk_start = jnp.arange(nb, dtype=I32)[:, None] * bm
    blk_e = jnp.minimum(jnp.sum((pend[None, :] <= blk_start).astype(I32), axis=1), N_EXPERTS - 1)
    y = _experts_stream(h, row_token, out_row, blk_e, nb_used, wg, wu, wd, cap)
    return _combine2(y, gates, x, gate, g_post, rows_per_mod, 512)


def _rope_angles(n_tokens, rot_dim):
    rows = n_tokens // GRID_W
    row = jnp.repeat(jnp.arange(rows, dtype=F32), GRID_W)
    col = jnp.tile(jnp.arange(GRID_W, dtype=F32), rows)
    n_freq = rot_dim // 4
    freq = ROPE_BASE ** (-jnp.arange(n_freq, dtype=F32) / n_freq)
    ang = jnp.concatenate([row[:, None] * freq, col[:, None] * freq], axis=-1)
    return jnp.cos(ang), jnp.sin(ang)


def _rot_cols(w):
    half = w.shape[-1] // 2
    return jnp.concatenate([-w[..., half:], w[..., :half]], axis=-1)


def kernel(x, c, ctx, c_ctx, w_mod, b_mod, g_pre_mix, g_post_mix, g_pre_ffn, g_post_ffn,
           w_in_a, mla_g_q, mla_w_uq, mla_g_kv, mla_w_ukv,
           gla_w_gate_f, gla_b_gate_f, gla_w_gate_b, gla_b_gate_b, gla_g_out, w_out_a,
           w_qkv_c, diff_lq1, diff_lk1, diff_lq2, diff_lk2, diff_g_out, w_out_c,
           ffn_w_gate, ffn_w_up, ffn_w_down,
           moe_w_router, moe_w_gate, moe_w_up, moe_w_down):
    bsz, t, d = x.shape
    tx = ctx.shape[1]
    n, nx = bsz * t, bsz * tx
    xl = x.reshape(n, d)
    xc = ctx.reshape(nx, d)

    rows = -(-(bsz + 1) // 8) * 8
    c_all = jnp.zeros((rows, d), F32).at[:bsz].set(c).at[bsz].set(c_ctx)
    mod_all = _modulation(c_all, w_mod, b_mod).reshape(2, rows, 6, d)

    def mods(i):
        lat = [mod_all[i, :bsz, k].reshape(bsz, 1, d) for k in range(6)]
        cx = [mod_all[i, bsz, k].reshape(1, 1, d) for k in range(6)]
        return lat, cx

    row2 = lambda v: v.reshape(1, -1)

    ml, mx = mods(0)
    wi = w_in_a[0]
    cq, ckv, kr, gq, gk, gv, gr, af, ab = jnp.split(
        wi, [256, 384, 448, 704, 960, 1472, 1984, 2000], axis=-1)
    w_in = jnp.concatenate([cq, ckv, kr, _rot_cols(kr), gq, gk, gv, gr, af, ab,
                            jnp.zeros((d, LANES - 2 * GLA_RANK), F32)], axis=-1).astype(BF16)
    uq = mla_w_uq[0].reshape(MLA_Q_RANK, MLA_HEADS, MLA_NOPE + MLA_ROPE)
    wq = jnp.concatenate([uq[..., :MLA_NOPE], uq[..., MLA_NOPE:], _rot_cols(uq[..., MLA_NOPE:])],
                         axis=-1).reshape(MLA_Q_RANK, MLA_HEADS * 2 * LANES).astype(BF16)
    ukv = mla_w_ukv[0].reshape(MLA_KV_RANK, MLA_HEADS, MLA_NOPE + MLA_V)
    wkv = jnp.concatenate([ukv[..., :MLA_NOPE].reshape(MLA_KV_RANK, -1),
                           ukv[..., MLA_NOPE:].reshape(MLA_KV_RANK, -1)], axis=-1).astype(BF16)
    cos_a, sin_a = _rope_angles(t, MLA_ROPE)
    zpad = jnp.zeros((t, LANES - MLA_ROPE), F32)
    c2 = jnp.concatenate([cos_a, cos_a, zpad], axis=-1)
    s2 = jnp.concatenate([sin_a, sin_a, zpad], axis=-1)
    lane = jnp.arange(LANES)
    c2x = jnp.broadcast_to(jnp.where(lane < MLA_ROPE, 1.0, 0.0).astype(F32), (tx, LANES))
    s2x = jnp.zeros((tx, LANES), F32)

    proj_l = _norm_proj(xl, row2(g_pre_mix[0]), ml[0], ml[1], w_in, t, 512, F32, "in_proj_a")
    proj_x = _norm_proj(xc, row2(g_pre_mix[0]), mx[0], mx[1], w_in, nx, 512, F32, "in_proj_a_ctx")
    gq_, gkv_ = row2(mla_g_q[0]), row2(mla_g_kv[0])
    q_l, k_l, v_l = _mla_prep(proj_l, gq_, gkv_, wq, wkv, c2, s2, 512, t // 512)
    q_x, k_x, v_x = _mla_prep(proj_x, gq_, gkv_, wq, wkv, c2x, s2x, tx, 0)
    a_x = _attention(q_x, [k_x], [v_x], bsz, MLA_HEADS, 2 * LANES, MLA_V, tx, tx, "mla_attention_ctx")

    hk = GLA_HEADS * GLA_DK
    hv = GLA_HEADS * GLA_DV
    wf = jnp.zeros((LANES, hk), F32).at[:GLA_RANK].set(gla_w_gate_f[0]).astype(BF16)
    wb = jnp.zeros((LANES, hk), F32).at[GLA_RANK:2 * GLA_RANK].set(gla_w_gate_b[0]).astype(BF16)
    lvl_f, lvl_b = _gla_level_maps()
    s0 = jnp.zeros((bsz, hv, hk), F32)
    gla_args = (wf, row2(gla_b_gate_f[0]), wb, row2(gla_b_gate_b[0]), lvl_f, lvl_b)
    ox_f, ox_b, sx_f, sx_b = _gla_scan(proj_x, *gla_args, s0, s0, bsz)
    a_l, ol_f, ol_b = _mla_gla(q_l, k_x, k_l, v_x, v_l, proj_l, *gla_args, sx_f, sx_b, bsz, ATTN_SUB)

    w_out = w_out_a[0].astype(BF16)
    g_o = row2(gla_g_out[0])
    wg, wu, wd = ffn_w_gate[0].astype(BF16), ffn_w_up[0].astype(BF16), ffn_w_down[0].astype(BF16)
    gpm, gpf, gqf = row2(g_post_mix[0]), row2(g_pre_ffn[0]), row2(g_post_ffn[0])
    tail_mod = lambda m: jnp.concatenate([m[2], m[3], m[4], m[5]], axis=1)
    xl = _mix_ffn(a_l, ol_f, ol_b, proj_l, g_o, xl, tail_mod(ml), gpm, w_out, gpf, gqf, wg, wu, wd, t, 512)
    xc = _mix_ffn(a_x, ox_f, ox_b, proj_x, g_o, xc, tail_mod(mx), gpm, w_out, gpf, gqf, wg, wu, wd, nx, 512)

    ml, mx = mods(1)
    lam_init = 0.8 - 0.6 * math.exp(-0.3 * 1)
    w_qkv = w_qkv_c[0].astype(BF16)
    width = DIFF_HEADS * 2 * DIFF_DIM
    cos_c, sin_c = _rope_angles(t, DIFF_DIM)
    z32 = jnp.zeros_like(sin_c)
    cc = jnp.concatenate([cos_c] * 4, axis=-1)
    sa = jnp.concatenate([-sin_c, z32, -sin_c, z32], axis=-1)
    sb = jnp.concatenate([z32, sin_c, z32, sin_c], axis=-1)
    q1, k1, v1 = _diff_prep(xl, row2(g_pre_mix[1]), ml[0], ml[1], w_qkv, cc, sa, sb, t, 512, t // 512)
    kv_x = _norm_proj(xc, row2(g_pre_mix[1]), mx[0], mx[1], w_qkv[:, width:], nx, 512, BF16, "diff_kv_ctx")
    lamvec = jnp.zeros((8, DIFF_DIM), F32).at[0].set(diff_lq1[0]).at[1].set(diff_lk1[0]) \
        .at[2].set(diff_lq2[0]).at[3].set(diff_lk2[0])
    a1, (moe_wg, moe_wu, moe_wd) = _diff_attention(
        q1, kv_x, k1, kv_x, v1, lamvec, row2(diff_g_out[0]), bsz, min(t, ATTN_ROWS), ATTN_SUB, lam_init,
        [moe_w_gate[0], moe_w_up[0], moe_w_down[0]])
    xl = _out_proj_c(a1, xl, ml[2], row2(g_post_mix[1]), w_out_c[0].astype(BF16), t, 512)

    xl = _moe(xl, row2(g_pre_ffn[1]), ml[3], ml[4], ml[5], row2(g_post_ffn[1]), moe_w_router[0],
              moe_wg, moe_wu, moe_wd, t)
    return xl.reshape(bsz, t, d)
```
